```python
import jax, jax.numpy as jnp
from jax import lax
import numpy as np

D_MODEL = 1024
BATCH = 4
SEQ = 4096
DEPTH = 2

MEM_LEN = 256
GRID_W = 64
Q_BLOCK = 128
EPS = 1e-6

ATTN_HEADS = 8
ATTN_KV_HEADS = 2
HEAD_DIM = 64
ATTN_Q_W = ATTN_HEADS * HEAD_DIM
ATTN_KV_W = ATTN_KV_HEADS * HEAD_DIM
ROPE_THETA = 10000.0

GMLP_W = 512
GMLP_GROUPS = 4
GMLP_GROUP_W = GMLP_W // GMLP_GROUPS
GMLP_CHUNK = 128

LRU_W = 512
LRU_HEADS = 8
LRU_HEAD_W = LRU_W // LRU_HEADS
CONV_W = 4
LRU_C = 8.0
N_DIR = 2

N_BRANCH = 3

XATTN_HEADS = 4
XATTN_HEAD_DIM = D_MODEL // XATTN_HEADS

D_FF = 2816

MIX_IN_W = ATTN_Q_W + 2 * ATTN_KV_W + 2 * GMLP_W + 2 * LRU_W + N_BRANCH * D_MODEL
MIX_IN_SPLITS = (
    ATTN_Q_W,
    ATTN_Q_W + ATTN_KV_W,
    ATTN_Q_W + 2 * ATTN_KV_W,
    ATTN_Q_W + 2 * ATTN_KV_W + GMLP_W,
    ATTN_Q_W + 2 * ATTN_KV_W + 2 * GMLP_W,
    ATTN_Q_W + 2 * ATTN_KV_W + 2 * GMLP_W + LRU_W,
    ATTN_Q_W + 2 * ATTN_KV_W + 2 * GMLP_W + 2 * LRU_W,
)

kernel_name = "hybrid_gated_parallel_encoder"


def rms_norm(x, g):
    xf = x.astype(jnp.float32)
    y = xf * lax.rsqrt(jnp.mean(xf * xf, axis=-1, keepdims=True) + EPS)
    return (y * g.astype(jnp.float32)).astype(x.dtype)


def swiglu(h, w_in, w_out):
    a, b = jnp.split(h @ w_in, 2, axis=-1)
    return (jax.nn.silu(a) * b) @ w_out


def axial_rope_tables(seq_len):
    rows = seq_len // GRID_W
    row = jnp.repeat(jnp.arange(rows), GRID_W).astype(jnp.float32)
    col = jnp.tile(jnp.arange(GRID_W), rows).astype(jnp.float32)
    n_freq = HEAD_DIM // 4
    inv_freq = ROPE_THETA ** (-jnp.arange(n_freq, dtype=jnp.float32) / n_freq)
    ang_r = row[:, None] * inv_freq[None, :]
    ang_c = col[:, None] * inv_freq[None, :]
    return (jnp.cos(ang_r), jnp.sin(ang_r), jnp.cos(ang_c), jnp.sin(ang_c))


def _rotate(x, cos, sin):
    x1, x2 = jnp.split(x, 2, axis=-1)
    c = cos[:, None, :]
    s = sin[:, None, :]
    return jnp.concatenate([x1 * c - x2 * s, x2 * c + x1 * s], axis=-1)


def apply_axial_rope(x, tabs):
    cos_r, sin_r, cos_c, sin_c = tabs
    xf = x.astype(jnp.float32)
    x_row, x_col = jnp.split(xf, 2, axis=-1)
    out = jnp.concatenate([_rotate(x_row, cos_r, sin_r), _rotate(x_col, cos_c, sin_c)], axis=-1)
    return out.astype(x.dtype)


def gqa_block_attention(q, k, v):
    b, s, _, _ = q.shape
    nb = s // Q_BLOCK
    grp = ATTN_HEADS // ATTN_KV_HEADS
    qb = q.reshape(b, nb, Q_BLOCK, ATTN_KV_HEADS, grp, HEAD_DIM).transpose(1, 0, 2, 3, 4, 5)
    scale = HEAD_DIM ** -0.5

    def one_block(qi):
        sc = jnp.einsum('bqkgd,bskd->bkgqs', qi, k).astype(jnp.float32) * scale
        p = jax.nn.softmax(sc, axis=-1).astype(v.dtype)
        return jnp.einsum('bkgqs,bskd->bqkgd', p, v)

    o = lax.map(one_block, qb)
    return o.transpose(1, 0, 2, 3, 4, 5).reshape(b, s, ATTN_Q_W)


def gmlp_branch(u, v, v_norm, ws, bs):
    b, s, _ = u.shape
    nc = s // GMLP_CHUNK
    u = jax.nn.gelu(u)
    v = rms_norm(jax.nn.gelu(v), v_norm)
    vc = v.reshape(b, nc, GMLP_CHUNK, GMLP_GROUPS, GMLP_GROUP_W)
    sv = jnp.einsum('gpq,bcqgd->bcpgd', ws, vc) + bs.T[:, :, None]
    return u * sv.reshape(b, s, GMLP_W)


def depthwise_conv_centred(x, w, bias):
    out = lax.conv_general_dilated(
        x, w[:, None, :], window_strides=(1,),
        padding=[(CONV_W // 2, CONV_W - 1 - CONV_W // 2)],
        dimension_numbers=('NWC', 'WIO', 'NWC'),
        feature_group_count=x.shape[-1])
    return out + bias


def block_diag(x, w, bias):
    b, s, _ = x.shape
    y = jnp.einsum('bshi,hio->bsho', x.reshape(b, s, LRU_HEADS, LRU_HEAD_W), w)
    return y.reshape(b, s, LRU_W) + bias


def rg_lru(x, wa, ba, wi, bi, lam, reverse):
    r = jax.nn.sigmoid(block_diag(x, wa, ba).astype(jnp.float32))
    i = jax.nn.sigmoid(block_diag(x, wi, bi).astype(jnp.float32))
    log_a = -LRU_C * r * jax.nn.softplus(-lam.astype(jnp.float32))
    a = jnp.exp(log_a)
    bx = jnp.sqrt(-jnp.expm1(2.0 * log_a)) * (i * x.astype(jnp.float32))

    def combine(e1, e2):
        a1, b1 = e1
        a2, b2 = e2
        return a1 * a2, a2 * b1 + b2

    _, h = lax.associative_scan(combine, (a, bx), axis=1, reverse=reverse)
    return h.astype(x.dtype)


def lru_branch(xl, yl, conv_w, conv_b, wa, ba, wi, bi, lam):
    xc = depthwise_conv_centred(xl, conv_w, conv_b)
    h = (rg_lru(xc, wa[0], ba[0], wi[0], bi[0], lam[0], False)
         + rg_lru(xc, wa[1], ba[1], wi[1], bi[1], lam[1], True))
    return h * jax.nn.gelu(yl)


def cross_attention(h, mem_n, wq, wkv, wo):
    b, s, _ = h.shape
    m = mem_n.shape[1]
    q = (h @ wq).reshape(b, s, XATTN_HEADS, XATTN_HEAD_DIM)
    k, v = jnp.split(mem_n @ wkv, 2, axis=-1)
    k = k.reshape(b, m, XATTN_HEADS, XATTN_HEAD_DIM)
    v = v.reshape(b, m, XATTN_HEADS, XATTN_HEAD_DIM)
    sc = jnp.einsum('bqhd,bmhd->bhqm', q, k).astype(jnp.float32) * (XATTN_HEAD_DIM ** -0.5)
    p = jax.nn.softmax(sc, axis=-1).astype(v.dtype)
    o = jnp.einsum('bhqm,bmhd->bqhd', p, v).reshape(b, s, D_MODEL)
    return o @ wo


def setup_inputs(seed: int = 0) -> dict:
    key = jax.random.key(seed)
    ks = iter(jax.random.split(key, 40))
    L = DEPTH

    def nrm(shape, scale):
        return jax.random.normal(next(ks), shape, jnp.float32) * scale

    def gain(shape):
        return 1.0 + nrm(shape, 0.01)

    x = nrm((BATCH, SEQ, D_MODEL), 1.0)
    mem = nrm((BATCH, MEM_LEN, D_MODEL), 1.0)
    ffn1_norm = gain((L, D_MODEL))
    ffn1_w_in = nrm((L, D_MODEL, 2 * D_FF), D_MODEL ** -0.5)
    ffn1_w_out = nrm((L, D_FF, D_MODEL), D_FF ** -0.5)
    mix_norm = gain((L, D_MODEL))
    w_mix_in = nrm((L, D_MODEL, MIX_IN_W), D_MODEL ** -0.5)
    b_gate = nrm((L, N_BRANCH * D_MODEL), 0.01)
    q_norm = gain((L, HEAD_DIM))
    k_norm = gain((L, HEAD_DIM))
    attn_up = nrm((L, ATTN_Q_W, D_MODEL), ATTN_Q_W ** -0.5)
    gmlp_v_norm = gain((L, GMLP_W))
    gmlp_ws = nrm((L, GMLP_GROUPS, GMLP_CHUNK, GMLP_CHUNK), 0.5 * GMLP_CHUNK ** -0.5)
    gmlp_bs = gain((L, GMLP_GROUPS, GMLP_CHUNK))
    gmlp_up = nrm((L, GMLP_W, D_MODEL), GMLP_W ** -0.5)
    lru_conv_w = nrm((L, CONV_W, LRU_W), CONV_W ** -0.5)
    lru_conv_b = nrm((L, LRU_W), 0.01)
    lru_wa = nrm((L, N_DIR, LRU_HEADS, LRU_HEAD_W, LRU_HEAD_W), LRU_HEAD_W ** -0.5)
    lru_ba = nrm((L, N_DIR, LRU_W), 0.01)
    lru_wi = nrm((L, N_DIR, LRU_HEADS, LRU_HEAD_W, LRU_HEAD_W), LRU_HEAD_W ** -0.5)
    lru_bi = nrm((L, N_DIR, LRU_W), 0.01)
    a_pow_c = jax.random.uniform(next(ks), (L, N_DIR, LRU_W), jnp.float32, 0.9, 0.999)
    a0 = a_pow_c ** (1.0 / LRU_C)
    lru_lambda = jnp.log(a0) - jnp.log1p(-a0)
    lru_up = nrm((L, LRU_W, D_MODEL), LRU_W ** -0.5)
    w_mix_out = nrm((L, D_MODEL, D_MODEL), D_MODEL ** -0.5)
    xattn_norm = gain((L, D_MODEL))
    mem_norm = gain((L, D_MODEL))
    xattn_wq = nrm((L, D_MODEL, D_MODEL), D_MODEL ** -0.5)
    xattn_wkv = nrm((L, D_MODEL, 2 * D_MODEL), D_MODEL ** -0.5)
    xattn_wo = nrm((L, D_MODEL, D_MODEL), D_MODEL ** -0.5)
    ffn2_norm = gain((L, D_MODEL))
    ffn2_w_in = nrm((L, D_MODEL, 2 * D_FF), D_MODEL ** -0.5)
    ffn2_w_out = nrm((L, D_FF, D_MODEL), D_FF ** -0.5)
    final_norm = gain((D_MODEL,))
    return {
        "x": x, "mem": mem,
        "ffn1_norm": ffn1_norm, "ffn1_w_in": ffn1_w_in, "ffn1_w_out": ffn1_w_out,
        "mix_norm": mix_norm, "w_mix_in": w_mix_in, "b_gate": b_gate,
        "q_norm": q_norm, "k_norm": k_norm, "attn_up": attn_up,
        "gmlp_v_norm": gmlp_v_norm, "gmlp_ws": gmlp_ws, "gmlp_bs": gmlp_bs, "gmlp_up": gmlp_up,
        "lru_conv_w": lru_conv_w, "lru_conv_b": lru_conv_b,
        "lru_wa": lru_wa, "lru_ba": lru_ba, "lru_wi": lru_wi, "lru_bi": lru_bi,
        "lru_lambda": lru_lambda, "lru_up": lru_up,
        "w_mix_out": w_mix_out,
        "xattn_norm": xattn_norm, "mem_norm": mem_norm,
        "xattn_wq": xattn_wq, "xattn_wkv": xattn_wkv, "xattn_wo": xattn_wo,
        "ffn2_norm": ffn2_norm, "ffn2_w_in": ffn2_w_in, "ffn2_w_out": ffn2_w_out,
        "final_norm": final_norm,
    }


def reference(x, mem, ffn1_norm, ffn1_w_in, ffn1_w_out, mix_norm, w_mix_in, b_gate,
              q_norm, k_norm, attn_up, gmlp_v_norm, gmlp_ws, gmlp_bs, gmlp_up,
              lru_conv_w, lru_conv_b, lru_wa, lru_ba, lru_wi, lru_bi, lru_lambda, lru_up,
              w_mix_out, xattn_norm, mem_norm, xattn_wq, xattn_wkv, xattn_wo,
              ffn2_norm, ffn2_w_in, ffn2_w_out, final_norm):
    b, s, _ = x.shape
    tabs = axial_rope_tables(s)
    for l in range(DEPTH):
        x = x + 0.5 * swiglu(rms_norm(x, ffn1_norm[l]), ffn1_w_in[l], ffn1_w_out[l])

        h = rms_norm(x, mix_norm[l])
        q, k, v, gu, gv, lx, ly, g = jnp.split(h @ w_mix_in[l], MIX_IN_SPLITS, axis=-1)

        q = apply_axial_rope(rms_norm(q.reshape(b, s, ATTN_HEADS, HEAD_DIM), q_norm[l]), tabs)
        k = apply_axial_rope(rms_norm(k.reshape(b, s, ATTN_KV_HEADS, HEAD_DIM), k_norm[l]), tabs)
        v = v.reshape(b, s, ATTN_KV_HEADS, HEAD_DIM)
        y_attn = gqa_block_attention(q, k, v) @ attn_up[l]

        y_gmlp = gmlp_branch(gu, gv, gmlp_v_norm[l], gmlp_ws[l], gmlp_bs[l]) @ gmlp_up[l]

        y_lru = lru_branch(lx, ly, lru_conv_w[l], lru_conv_b[l], lru_wa[l], lru_ba[l],
                           lru_wi[l], lru_bi[l], lru_lambda[l]) @ lru_up[l]

        gates = jax.nn.sigmoid((g + b_gate[l]).astype(jnp.float32)).astype(x.dtype)
        gates = gates.reshape(b, s, N_BRANCH, D_MODEL)
        merged = gates[:, :, 0] * y_attn + gates[:, :, 1] * y_gmlp + gates[:, :, 2] * y_lru
        x = x + merged @ w_mix_out[l]

        x = x + cross_attention(rms_norm(x, xattn_norm[l]), rms_norm(mem, mem_norm[l]),
                                xattn_wq[l], xattn_wkv[l], xattn_wo[l])

        x = x + 0.5 * swiglu(rms_norm(x, ffn2_norm[l]), ffn2_w_in[l], ffn2_w_out[l])
    return rms_norm(x, final_norm)
```

```python
import functools

import jax
import jax.numpy as jnp
from jax import lax
from jax.experimental import pallas as pl
from jax.experimental.pallas import tpu as pltpu

F32 = jnp.float32
BF16 = jnp.bfloat16

D_MODEL = 1024
BATCH = 4
SEQ = 4096
DEPTH = 2
TOKENS = BATCH * SEQ
MEM_LEN = 256
GRID_W = 64
EPS = 1e-6

ATTN_HEADS = 8
ATTN_KV_HEADS = 2
ATTN_GROUP = ATTN_HEADS // ATTN_KV_HEADS
HEAD_DIM = 64
ATTN_Q_W = ATTN_HEADS * HEAD_DIM
ATTN_KV_W = ATTN_KV_HEADS * HEAD_DIM
QK_W = ATTN_Q_W + ATTN_KV_W
GROUP_Q_W = ATTN_GROUP * HEAD_DIM
ROPE_THETA = 10000.0

GMLP_W = 512
GMLP_GROUPS = 4
GMLP_GROUP_W = GMLP_W // GMLP_GROUPS
GMLP_CHUNK = 128

LRU_W = 512
LRU_HEADS = 8
LRU_HEAD_W = LRU_W // LRU_HEADS
CONV_W = 4
LRU_C = 8.0
N_DIR = 2
N_BRANCH = 3

XATTN_HEADS = 4
XATTN_HEAD_DIM = D_MODEL // XATTN_HEADS
D_FF = 2816

LANES = 128
SUBLANES = 8
VMEM_LIMIT_BYTES = 56 * 1024 * 1024

FFN_TM = 512
FFN_CHUNK = 256
MIX_TM = 512
ATTN_TQ = 128
GMLP_TM = 512
MERGE_TM = 512
XATTN_TM = 512
LRU_ROWS = 512
LRU_SEG = 516
LRU_PAD_SEQ = SUBLANES * LRU_SEG
LRU_HALO = SUBLANES


def _resident(shape):
    nd = len(shape)
    return pl.BlockSpec(shape, lambda *_: (0,) * nd, pipeline_mode=pl.Buffered(1))


def _params(n_grid_axes):
    return pltpu.CompilerParams(
        dimension_semantics=("arbitrary",) * n_grid_axes,
        vmem_limit_bytes=VMEM_LIMIT_BYTES,
    )


def _rms(x, g):
    ms = jnp.mean(x * x, axis=-1, keepdims=True)
    return x * lax.rsqrt(ms + EPS) * g


def _dot(a, b):
    return jnp.dot(a, b, preferred_element_type=F32)


def _ffn_body(x_ref, g_ref, win_ref, wout_ref, fin_ref, o_ref, *, final):
    x = x_ref[...]
    xn = _rms(x, g_ref[...]).astype(BF16)
    acc = jnp.zeros(x.shape, F32)
    for c in range(D_FF // FFN_CHUNK):
        lo = c * FFN_CHUNK
        a = _dot(xn, win_ref[:, lo:lo + FFN_CHUNK])
        b = _dot(xn, win_ref[:, D_FF + lo:D_FF + lo + FFN_CHUNK])
        h = (a * jax.nn.sigmoid(a) * b).astype(BF16)
        acc = acc + _dot(h, wout_ref[lo:lo + FFN_CHUNK, :])
    y = x + 0.5 * acc
    if final:
        y = _rms(y, fin_ref[...])
    o_ref[...] = y


def _ffn(x, g, w_in, w_out, fin, *, final):
    tile = pl.BlockSpec((FFN_TM, D_MODEL), lambda i: (i, 0))
    return pl.pallas_call(
        functools.partial(_ffn_body, final=final),
        grid=(TOKENS // FFN_TM,),
        in_specs=[tile, _resident((1, D_MODEL)), _resident(w_in.shape),
                  _resident(w_out.shape), _resident((1, D_MODEL))],
        out_specs=tile,
        out_shape=jax.ShapeDtypeStruct((TOKENS, D_MODEL), F32),
        compiler_params=_params(1),
        name="ffn_final" if final else "ffn",
    )(x, g, w_in, w_out, fin)


def _swap_halves_of_32(x):
    lane = lax.broadcasted_iota(jnp.int32, x.shape, 1)
    up = pltpu.roll(x, LANES - 16, 1)
    down = pltpu.roll(x, 16, 1)
    return jnp.where((lane & 16) == 0, up, down)


def _mix_in_body(x_ref, g_ref, wqk_ref, wv_ref, wgm_ref, wlr_ref, wg_ref,
                 bg_ref, qkg_ref, gsum_ref, cos_ref, sin_ref,
                 q_ref, kt_ref, v_ref, gm_ref, lr_ref, gate_ref):
    xn = _rms(x_ref[...], g_ref[...]).astype(BF16)

    qk = _dot(xn, wqk_ref[...])
    sq = qk * qk
    sq_hi = sq.astype(BF16)
    sq_lo = (sq - sq_hi.astype(F32)).astype(BF16)
    ms = _dot(sq_hi, gsum_ref[...]) + _dot(sq_lo, gsum_ref[...])
    qkn = qk * lax.rsqrt(ms + EPS) * qkg_ref[...]
    cos = cos_ref[...]
    sin = sin_ref[...]
    cols = []
    for c in range(QK_W // LANES):
        t = qkn[:, c * LANES:(c + 1) * LANES]
        cols.append(t * cos + _swap_halves_of_32(t) * sin)
    q_ref[...] = (jnp.concatenate(cols[:ATTN_Q_W // LANES], axis=1)
                  * (HEAD_DIM ** -0.5)).astype(BF16)
    kt_ref[0] = cols[-1].T.astype(BF16)

    v_ref[...] = _dot(xn, wv_ref[...]).astype(BF16)
    gm_ref[...] = _dot(xn, wgm_ref[...])
    lr_ref[...] = _dot(xn, wlr_ref[...])
    g = _dot(xn, wg_ref[...]) + bg_ref[...]
    gate_ref[...] = jax.nn.sigmoid(g).astype(BF16)


def _mix_in(x, g, wqk, wv, wgm, wlr, wg, bg, qkg, gsum, cos, sin):
    seq_tiles = SEQ // MIX_TM
    row = lambda w: pl.BlockSpec((MIX_TM, w), lambda i: (i, 0))
    tab = pl.BlockSpec((MIX_TM, LANES), lambda i: (i % seq_tiles, 0))
    v_w = ATTN_KV_HEADS * GROUP_Q_W
    return pl.pallas_call(
        _mix_in_body,
        grid=(TOKENS // MIX_TM,),
        in_specs=[row(D_MODEL), _resident((1, D_MODEL)), _resident(wqk.shape),
                  _resident(wv.shape), _resident(wgm.shape), _resident(wlr.shape),
                  _resident(wg.shape), _resident(bg.shape), _resident(qkg.shape),
                  _resident(gsum.shape), tab, tab],
        out_specs=[
            row(ATTN_Q_W),
            pl.BlockSpec((1, ATTN_KV_W, MIX_TM),
                         lambda i: (i // seq_tiles, 0, i % seq_tiles)),
            row(v_w), row(2 * GMLP_W), row(2 * LRU_W), row(N_BRANCH * D_MODEL)],
        out_shape=[
            jax.ShapeDtypeStruct((TOKENS, ATTN_Q_W), BF16),
            jax.ShapeDtypeStruct((BATCH, ATTN_KV_W, SEQ), BF16),
            jax.ShapeDtypeStruct((TOKENS, v_w), BF16),
            jax.ShapeDtypeStruct((TOKENS, 2 * GMLP_W), F32),
            jax.ShapeDtypeStruct((TOKENS, 2 * LRU_W), F32),
            jax.ShapeDtypeStruct((TOKENS, N_BRANCH * D_MODEL), BF16)],
        compiler_params=_params(1),
        name="mix_in",
    )(x, g, wqk, wv, wgm, wlr, wg, bg, qkg, gsum, cos, sin)


def _attn_body(q_ref, kt_ref, v_ref, o_ref):
    q = q_ref[...]
    head_of_lane = lax.broadcasted_iota(jnp.int32, (1, GROUP_Q_W), 1) // HEAD_DIM
    zero = jnp.zeros_like(q)
    qs = jnp.concatenate(
        [jnp.where(head_of_lane == h, q, zero) for h in range(ATTN_GROUP)], axis=0)
    kt = kt_ref[0]
    kt_rep = jnp.concatenate([kt] * ATTN_GROUP, axis=0)
    s = _dot(qs, kt_rep)
    m = jnp.max(s, axis=-1, keepdims=True)
    p = jnp.exp(s - m)
    l = jnp.sum(p, axis=-1, keepdims=True)
    pv = _dot(p.astype(BF16), v_ref[...]) * (1.0 / l)
    out = jnp.zeros((ATTN_TQ, GROUP_Q_W), F32)
    for h in range(ATTN_GROUP):
        out = out + jnp.where(head_of_lane == h,
                              pv[h * ATTN_TQ:(h + 1) * ATTN_TQ], 0.0)
    o_ref[...] = out.astype(BF16)


def _attention(q, kt, v_rep):
    q_tiles = SEQ // ATTN_TQ
    return pl.pallas_call(
        _attn_body,
        grid=(BATCH, ATTN_KV_HEADS, q_tiles),
        in_specs=[
            pl.BlockSpec((ATTN_TQ, GROUP_Q_W), lambda b, j, i: (b * q_tiles + i, j)),
            pl.BlockSpec((1, HEAD_DIM, SEQ), lambda b, j, i: (b, j, 0)),
            pl.BlockSpec((SEQ, GROUP_Q_W), lambda b, j, i: (b, j)),
        ],
        out_specs=pl.BlockSpec((ATTN_TQ, GROUP_Q_W),
                               lambda b, j, i: (b * q_tiles + i, j)),
        out_shape=jax.ShapeDtypeStruct((TOKENS, ATTN_Q_W), BF16),
        compiler_params=_params(3),
        name="gqa_attention",
    )(q, kt, v_rep)


def _gmlp_body(gm_ref, vn_ref, ws_ref, bias_ref, o_ref):
    u = jax.nn.gelu(gm_ref[:, :GMLP_W])
    v = _rms(jax.nn.gelu(gm_ref[:, GMLP_W:]), vn_ref[...]).astype(BF16)
    bias = bias_ref[...]
    for c in range(GMLP_TM // GMLP_CHUNK):
        rows = slice(c * GMLP_CHUNK, (c + 1) * GMLP_CHUNK)
        sv = jnp.concatenate(
            [_dot(ws_ref[g], v[rows, g * GMLP_GROUP_W:(g + 1) * GMLP_GROUP_W])
             for g in range(GMLP_GROUPS)], axis=1)
        o_ref[rows, :] = (u[rows] * (sv + bias)).astype(BF16)


def _gmlp(gm, vn, ws, bias):
    return pl.pallas_call(
        _gmlp_body,
        grid=(TOKENS // GMLP_TM,),
        in_specs=[pl.BlockSpec((GMLP_TM, 2 * GMLP_W), lambda i: (i, 0)),
                  _resident(vn.shape), _resident(ws.shape), _resident(bias.shape)],
        out_specs=pl.BlockSpec((GMLP_TM, GMLP_W), lambda i: (i, 0)),
        out_shape=jax.ShapeDtypeStruct((TOKENS, GMLP_W), BF16),
        compiler_params=_params(1),
        name="gmlp",
    )(gm, vn, ws, bias)


def _lru_body(lx_ref, ly_ref, cw_ref, cb_ref, w_ref, b_ref, lam_ref, o_ref,
              xpad, a_f, b_f, a_b, b_b, carry_f, carry_b):
    n_chunks = SEQ // LRU_ROWS
    halo_zeros = jnp.zeros((LRU_HALO, LANES), F32)
    xpad[0:LRU_HALO, :] = halo_zeros
    xpad[LRU_HALO + SEQ:LRU_HALO + SEQ + LRU_HALO, :] = halo_zeros
    xpad[LRU_HALO:LRU_HALO + SEQ, :] = lx_ref[...]

    neg_sp = -jax.nn.softplus(-lam_ref[...])
    cw = cw_ref[...]
    cb = cb_ref[...]
    bias = b_ref[...]
    scan_bufs = ((a_f, b_f), (a_b, b_b))

    def dense(c, carry):
        r0 = pl.multiple_of(c * LRU_ROWS, LRU_ROWS)
        xc = cb
        for j in range(CONV_W):
            off = LRU_HALO + j - CONV_W // 2
            xc = xc + cw[j:j + 1, :] * xpad[pl.ds(r0 + off, LRU_ROWS), :]
        pre = _dot(xc.astype(BF16), w_ref[0]) + bias
        for d in range(N_DIR):
            r = jax.nn.sigmoid(pre[:, (2 * d) * LANES:(2 * d + 1) * LANES])
            i = jax.nn.sigmoid(pre[:, (2 * d + 1) * LANES:(2 * d + 2) * LANES])
            log_a = LRU_C * r * neg_sp[d:d + 1, :]
            a_buf, b_buf = scan_bufs[d]
            a_buf[pl.ds(r0, LRU_ROWS), :] = jnp.exp(log_a)
            b_buf[pl.ds(r0, LRU_ROWS), :] = (
                jnp.sqrt(1.0 - jnp.exp(2.0 * log_a)) * (i * xc))
        return carry

    lax.fori_loop(0, n_chunks, dense, 0)
    tail = LRU_PAD_SEQ - SEQ
    for a_buf, b_buf in scan_bufs:
        a_buf[SEQ:LRU_PAD_SEQ, :] = jnp.ones((tail, LANES), F32)
        b_buf[SEQ:LRU_PAD_SEQ, :] = jnp.zeros((tail, LANES), F32)

    def seg(t):
        return pl.ds(t, SUBLANES, stride=LRU_SEG)

    def scan_step(k, carry):
        hf, pf, hb, pb = carry
        tf = k
        af = a_f[seg(tf), :]
        hf = af * hf + b_f[seg(tf), :]
        pf = af * pf
        b_f[seg(tf), :] = hf
        a_f[seg(tf), :] = pf
        tb = LRU_SEG - 1 - k
        ab = a_b[seg(tb), :]
        hb = ab * hb + b_b[seg(tb), :]
        pb = ab * pb
        b_b[seg(tb), :] = hb
        a_b[seg(tb), :] = pb
        return hf, pf, hb, pb

    z = jnp.zeros((SUBLANES, LANES), F32)
    o = jnp.ones((SUBLANES, LANES), F32)
    lax.fori_loop(0, LRU_SEG, scan_step, (z, o, z, o))

    row0 = jnp.zeros((1, LANES), F32)
    carry_f[0:1, :] = row0
    c = row0
    for s in range(1, SUBLANES):
        last = s * LRU_SEG - 1
        c = b_f[last:last + 1, :] + a_f[last:last + 1, :] * c
        carry_f[s:s + 1, :] = c
    carry_b[SUBLANES - 1:SUBLANES, :] = row0
    c = row0
    for s in range(SUBLANES - 2, -1, -1):
        first = (s + 1) * LRU_SEG
        c = b_b[first:first + 1, :] + a_b[first:first + 1, :] * c
        carry_b[s:s + 1, :] = c

    cf = carry_f[...]
    cbk = carry_b[...]

    def fix_step(t, carry):
        b_f[seg(t), :] = b_f[seg(t), :] + a_f[seg(t), :] * cf
        b_b[seg(t), :] = b_b[seg(t), :] + a_b[seg(t), :] * cbk
        return carry

    lax.fori_loop(0, LRU_SEG, fix_step, 0)

    def emit(c, carry):
        r0 = pl.multiple_of(c * LRU_ROWS, LRU_ROWS)
        rows = pl.ds(r0, LRU_ROWS)
        h = b_f[rows, :] + b_b[rows, :]
        o_ref[rows, :] = (h * jax.nn.gelu(ly_ref[rows, :])).astype(BF16)
        return carry

    lax.fori_loop(0, n_chunks, emit, 0)


def _lru(lr, cw, cb, w, b, lam):
    n_blk = LRU_W // LANES
    scan_buf = pltpu.VMEM((LRU_PAD_SEQ, LANES), F32)
    return pl.pallas_call(
        _lru_body,
        grid=(BATCH, n_blk),
        in_specs=[
            pl.BlockSpec((SEQ, LANES), lambda b_, c: (b_, c)),
            pl.BlockSpec((SEQ, LANES), lambda b_, c: (b_, n_blk + c)),
            pl.BlockSpec((CONV_W, LANES), lambda b_, c: (0, c)),
            pl.BlockSpec((1, LANES), lambda b_, c: (0, c)),
            pl.BlockSpec((1, LANES, 2 * N_DIR * LANES), lambda b_, c: (c, 0, 0)),
            pl.BlockSpec((1, 2 * N_DIR * LANES), lambda b_, c: (0, c)),
            pl.BlockSpec((N_DIR, LANES), lambda b_, c: (0, c)),
        ],
        out_specs=pl.BlockSpec((SEQ, LANES), lambda b_, c: (b_, c)),
        out_shape=jax.ShapeDtypeStruct((TOKENS, LRU_W), BF16),
        scratch_shapes=[
            pltpu.VMEM((SEQ + 2 * LRU_HALO, LANES), F32),
            scan_buf, scan_buf, scan_buf, scan_buf,
            pltpu.VMEM((SUBLANES, LANES), F32),
            pltpu.VMEM((SUBLANES, LANES), F32),
        ],
        compiler_params=_params(2),
        name="rg_lru",
    )(lr, lr, cw, cb, w, b, lam)


def _merge_body(x_ref, a_ref, g_ref, l_ref, gate_ref, wa_ref, wg_ref, wl_ref,
                wo_ref, o_ref):
    merged = jnp.zeros((MERGE_TM, D_MODEL), F32)
    for k, (br, w) in enumerate(((a_ref, wa_ref), (g_ref, wg_ref), (l_ref, wl_ref))):
        gate = gate_ref[:, k * D_MODEL:(k + 1) * D_MODEL].astype(F32)
        merged = merged + gate * _dot(br[...], w[...])
    o_ref[...] = x_ref[...] + _dot(merged.astype(BF16), wo_ref[...])


def _merge(x, attn_o, gmlp_o, lru_o, gates, wa, wg, wl, wo):
    row = lambda w: pl.BlockSpec((MERGE_TM, w), lambda i: (i, 0))
    return pl.pallas_call(
        _merge_body,
        grid=(TOKENS // MERGE_TM,),
        in_specs=[row(D_MODEL), row(ATTN_Q_W), row(GMLP_W), row(LRU_W),
                  row(N_BRANCH * D_MODEL), _resident(wa.shape), _resident(wg.shape),
                  _resident(wl.shape), _resident(wo.shape)],
        out_specs=row(D_MODEL),
        out_shape=jax.ShapeDtypeStruct((TOKENS, D_MODEL), F32),
        compiler_params=_params(1),
        name="merge",
    )(x, attn_o, gmlp_o, lru_o, gates, wa, wg, wl, wo)


def _xkv_body(mem_ref, g_ref, wkv_ref, kt_ref, v_ref):
    mn = _rms(mem_ref[0], g_ref[...]).astype(BF16)
    kv = _dot(mn, wkv_ref[...])
    kt_ref[0] = (kv[:, :D_MODEL] * (XATTN_HEAD_DIM ** -0.5)).T.astype(BF16)
    v_ref[0] = kv[:, D_MODEL:].astype(BF16)


def _xkv(mem, g, wkv):
    return pl.pallas_call(
        _xkv_body,
        grid=(BATCH,),
        in_specs=[pl.BlockSpec((1, MEM_LEN, D_MODEL), lambda b: (b, 0, 0)),
                  _resident((1, D_MODEL)), _resident(wkv.shape)],
        out_specs=[pl.BlockSpec((1, D_MODEL, MEM_LEN), lambda b: (b, 0, 0)),
                   pl.BlockSpec((1, MEM_LEN, D_MODEL), lambda b: (b, 0, 0))],
        out_shape=[jax.ShapeDtypeStruct((BATCH, D_MODEL, MEM_LEN), BF16),
                   jax.ShapeDtypeStruct((BATCH, MEM_LEN, D_MODEL), BF16)],
        compiler_params=_params(1),
        name="xattn_kv",
    )(mem, g, wkv)


def _xattn_body(x_ref, g_ref, wq_ref, kt_ref, v_ref, wo_ref, o_ref):
    x = x_ref[...]
    q = _dot(_rms(x, g_ref[...]).astype(BF16), wq_ref[...]).astype(BF16)
    heads = []
    for h in range(XATTN_HEADS):
        cols = slice(h * XATTN_HEAD_DIM, (h + 1) * XATTN_HEAD_DIM)
        s = _dot(q[:, cols], kt_ref[0, cols, :])
        p = jnp.exp(s - jnp.max(s, axis=-1, keepdims=True))
        l = jnp.sum(p, axis=-1, keepdims=True)
        heads.append((_dot(p.astype(BF16), v_ref[0, :, cols]) * (1.0 / l)).astype(BF16))
    o_ref[...] = x + _dot(jnp.concatenate(heads, axis=1), wo_ref[...])


def _xattn(x, g, wq, kt, v, wo):
    tiles = SEQ // XATTN_TM
    row = pl.BlockSpec((XATTN_TM, D_MODEL), lambda i: (i, 0))
    return pl.pallas_call(
        _xattn_body,
        grid=(TOKENS // XATTN_TM,),
        in_specs=[row, _resident((1, D_MODEL)), _resident(wq.shape),
                  pl.BlockSpec((1, D_MODEL, MEM_LEN), lambda i: (i // tiles, 0, 0)),
                  pl.BlockSpec((1, MEM_LEN, D_MODEL), lambda i: (i // tiles, 0, 0)),
                  _resident(wo.shape)],
        out_specs=row,
        out_shape=jax.ShapeDtypeStruct((TOKENS, D_MODEL), F32),
        compiler_params=_params(1),
        name="xattn",
    )(x, g, wq, kt, v, wo)


def _rope_tables():
    rows = SEQ // GRID_W
    row = jnp.repeat(jnp.arange(rows), GRID_W).astype(F32)
    col = jnp.tile(jnp.arange(GRID_W), rows).astype(F32)
    n_freq = HEAD_DIM // 4
    inv_freq = ROPE_THETA ** (-jnp.arange(n_freq, dtype=F32) / n_freq)
    ang_r = row[:, None] * inv_freq[None, :]
    ang_c = col[:, None] * inv_freq[None, :]
    cos = jnp.concatenate([jnp.cos(ang_r)] * 2 + [jnp.cos(ang_c)] * 2, axis=1)
    sin = jnp.concatenate([-jnp.sin(ang_r), jnp.sin(ang_r),
                           -jnp.sin(ang_c), jnp.sin(ang_c)], axis=1)
    rep = LANES // HEAD_DIM
    return jnp.tile(cos, (1, rep)), jnp.tile(sin, (1, rep))


def _block_diag(w):
    n, a, b = w.shape
    eye = jnp.eye(n, dtype=w.dtype)
    return (eye[:, None, :, None] * w[:, :, None, :]).reshape(n * a, n * b)


def kernel(x, mem, ffn1_norm, ffn1_w_in, ffn1_w_out, mix_norm, w_mix_in, b_gate,
           q_norm, k_norm, attn_up, gmlp_v_norm, gmlp_ws, gmlp_bs, gmlp_up,
           lru_conv_w, lru_conv_b, lru_wa, lru_ba, lru_wi, lru_bi, lru_lambda, lru_up,
           w_mix_out, xattn_norm, mem_norm, xattn_wq, xattn_wkv, xattn_wo,
           ffn2_norm, ffn2_w_in, ffn2_w_out, final_norm):
    assert x.shape == (BATCH, SEQ, D_MODEL) and mem.shape == (BATCH, MEM_LEN, D_MODEL)
    cos, sin = _rope_tables()
    gsum = _block_diag(jnp.full((QK_W // HEAD_DIM, HEAD_DIM, HEAD_DIM),
                                1.0 / HEAD_DIM, BF16))
    fin = final_norm.reshape(1, D_MODEL)
    n_blk = LRU_W // LANES
    heads_per_blk = LANES // LRU_HEAD_W

    h = x.reshape(TOKENS, D_MODEL)
    for l in range(DEPTH):
        row = lambda a: a[l].reshape(1, -1)
        wmi = w_mix_in[l].astype(BF16)
        s0 = ATTN_Q_W + ATTN_KV_W
        s1 = s0 + ATTN_KV_W
        s2 = s1 + 2 * GMLP_W
        s3 = s2 + 2 * LRU_W
        wv = wmi[:, s0:s1].reshape(D_MODEL, ATTN_KV_HEADS, 1, HEAD_DIM)
        wv_rep = jnp.broadcast_to(
            wv, (D_MODEL, ATTN_KV_HEADS, ATTN_GROUP, HEAD_DIM)).reshape(D_MODEL, -1)
        qkg = jnp.concatenate([jnp.tile(q_norm[l], ATTN_HEADS),
                               jnp.tile(k_norm[l], ATTN_KV_HEADS)]).reshape(1, QK_W)

        h = _ffn(h, row(ffn1_norm), ffn1_w_in[l].astype(BF16),
                 ffn1_w_out[l].astype(BF16), fin, final=False)

        q, kt, v_rep, gm, lr, gates = _mix_in(
            h, row(mix_norm), wmi[:, :s0], wv_rep, wmi[:, s1:s2], wmi[:, s2:s3],
            wmi[:, s3:], row(b_gate), qkg, gsum, cos, sin)

        attn_o = _attention(q, kt, v_rep)

        gm_bias = jnp.repeat(gmlp_bs[l].T, GMLP_GROUP_W, axis=1)
        gmlp_o = _gmlp(gm, row(gmlp_v_norm), gmlp_ws[l].astype(BF16), gm_bias)

        def blk(w):
            return jax.vmap(_block_diag)(
                w.reshape(n_blk, heads_per_blk, LRU_HEAD_W, LRU_HEAD_W))
        lru_w = jnp.concatenate(
            [blk(lru_wa[l, 0]), blk(lru_wi[l, 0]), blk(lru_wa[l, 1]), blk(lru_wi[l, 1])],
            axis=2).astype(BF16)
        lru_b = jnp.stack(
            [lru_ba[l, 0].reshape(n_blk, LANES), lru_bi[l, 0].reshape(n_blk, LANES),
             lru_ba[l, 1].reshape(n_blk, LANES), lru_bi[l, 1].reshape(n_blk, LANES)],
            axis=1).reshape(1, -1)
        lru_o = _lru(lr, lru_conv_w[l], row(lru_conv_b), lru_w, lru_b, lru_lambda[l])

        h = _merge(h, attn_o, gmlp_o, lru_o, gates, attn_up[l].astype(BF16),
                   gmlp_up[l].astype(BF16), lru_up[l].astype(BF16),
                   w_mix_out[l].astype(BF16))

        xkt, xv = _xkv(mem, row(mem_norm), xattn_wkv[l].astype(BF16))
        h = _xattn(h, row(xattn_norm), xattn_wq[l].astype(BF16), xkt, xv,
                   xattn_wo[l].astype(BF16))

        h = _ffn(h, row(ffn2_norm), ffn2_w_in[l].astype(BF16),
                 ffn2_w_out[l].astype(BF16), fin, final=(l == DEPTH - 1))
    return h.reshape(BATCH, SEQ, D_MODEL)
```

```python
import functools

import jax
import jax.numpy as jnp
from jax import lax
from jax.experimental import pallas as pl
from jax.experimental.pallas import tpu as pltpu

F32 = jnp.float32
BF16 = jnp.bfloat16

D_MODEL = 1024
BATCH = 4
SEQ = 4096
DEPTH = 2
TOKENS = BATCH * SEQ
MEM_LEN = 256
GRID_W = 64
EPS = 1e-6

ATTN_HEADS = 8
ATTN_KV_HEADS = 2
ATTN_GROUP = ATTN_HEADS // ATTN_KV_HEADS
HEAD_DIM = 64
ATTN_Q_W = ATTN_HEADS * HEAD_DIM
ATTN_KV_W = ATTN_KV_HEADS * HEAD_DIM
QK_W = ATTN_Q_W + ATTN_KV_W
GROUP_Q_W = ATTN_GROUP * HEAD_DIM
ROPE_THETA = 10000.0
LOG2_E = 1.4426950408889634

GMLP_W = 512
GMLP_GROUPS = 4
GMLP_GROUP_W = GMLP_W // GMLP_GROUPS
GMLP_CHUNK = 128

LRU_W = 512
LRU_HEADS = 8
LRU_HEAD_W = LRU_W // LRU_HEADS
CONV_W = 4
LRU_C = 8.0
N_DIR = 2
N_BRANCH = 3

XATTN_HEADS = 4
XATTN_HEAD_DIM = D_MODEL // XATTN_HEADS
D_FF = 2816

LANES = 128
SUBLANES = 8
VMEM_LIMIT_BYTES = 56 * 1024 * 1024

FFN_TM = 512
FFN_CHUNK = 256
MIX_TM = 512
ATTN_TQ = 256
GMLP_TM = 512
MERGE_TM = 512
XATTN_TM = 512
LRU_ROWS = 512
LRU_SEG = 516
LRU_PAD_SEQ = SUBLANES * LRU_SEG
LRU_HALO = SUBLANES


def _resident(shape):
    nd = len(shape)
    return pl.BlockSpec(shape, lambda *_: (0,) * nd, pipeline_mode=pl.Buffered(1))


def _params(n_grid_axes):
    return pltpu.CompilerParams(
        dimension_semantics=("arbitrary",) * n_grid_axes,
        vmem_limit_bytes=VMEM_LIMIT_BYTES,
    )


def _rms(x, g):
    ms = jnp.mean(x * x, axis=-1, keepdims=True)
    return x * lax.rsqrt(ms + EPS) * g


def _dot(a, b):
    return jnp.dot(a, b, preferred_element_type=F32)


def _ffn_body(x_ref, g_ref, win_ref, wout_ref, fin_ref, o_ref, *, final):
    x = x_ref[...]
    xn = _rms(x, g_ref[...]).astype(BF16)
    acc = jnp.zeros(x.shape, F32)
    for c in range(D_FF // FFN_CHUNK):
        lo = c * FFN_CHUNK
        a = _dot(xn, win_ref[:, lo:lo + FFN_CHUNK])
        b = _dot(xn, win_ref[:, D_FF + lo:D_FF + lo + FFN_CHUNK])
        h = (a * jax.nn.sigmoid(a) * b).astype(BF16)
        acc = acc + _dot(h, wout_ref[lo:lo + FFN_CHUNK, :])
    y = x + 0.5 * acc
    if final:
        y = _rms(y, fin_ref[...])
    o_ref[...] = y


def _ffn(x, g, w_in, w_out, fin, *, final):
    tile = pl.BlockSpec((FFN_TM, D_MODEL), lambda i: (i, 0))
    return pl.pallas_call(
        functools.partial(_ffn_body, final=final),
        grid=(TOKENS // FFN_TM,),
        in_specs=[tile, _resident((1, D_MODEL)), _resident(w_in.shape),
                  _resident(w_out.shape), _resident((1, D_MODEL))],
        out_specs=tile,
        out_shape=jax.ShapeDtypeStruct((TOKENS, D_MODEL), F32),
        compiler_params=_params(1),
        name="ffn_final" if final else "ffn",
    )(x, g, w_in, w_out, fin)


def _swap_halves_of_32(x):
    lane = lax.broadcasted_iota(jnp.int32, x.shape, 1)
    up = pltpu.roll(x, LANES - 16, 1)
    down = pltpu.roll(x, 16, 1)
    return jnp.where((lane & 16) == 0, up, down)


def _mix_in_body(x_ref, g_ref, wqk_ref, wv_ref, wgm_ref, wlr_ref, wg_ref,
                 bg_ref, qkg_ref, gsum_ref, cos_ref, sin_ref,
                 q_ref, kt_ref, v_ref, gm_ref, lr_ref, gate_ref):
    xn = _rms(x_ref[...], g_ref[...]).astype(BF16)

    qk = _dot(xn, wqk_ref[...])
    sq = qk * qk
    sq_hi = sq.astype(BF16)
    sq_lo = (sq - sq_hi.astype(F32)).astype(BF16)
    ms = _dot(sq_hi, gsum_ref[...]) + _dot(sq_lo, gsum_ref[...])
    qkn = qk * lax.rsqrt(ms + EPS) * qkg_ref[...]
    cos = cos_ref[...]
    sin = sin_ref[...]
    cols = []
    for c in range(QK_W // LANES):
        t = qkn[:, c * LANES:(c + 1) * LANES]
        cols.append(t * cos + _swap_halves_of_32(t) * sin)
    q_ref[...] = (jnp.concatenate(cols[:ATTN_Q_W // LANES], axis=1)
                  * (HEAD_DIM ** -0.5 * LOG2_E)).astype(BF16)
    kt_ref[0] = cols[-1].T.astype(BF16)

    v_ref[...] = _dot(xn, wv_ref[...]).astype(BF16)
    gm_ref[...] = _dot(xn, wgm_ref[...])
    lr_ref[...] = _dot(xn, wlr_ref[...])
    g = _dot(xn, wg_ref[...]) + bg_ref[...]
    gate_ref[...] = jax.nn.sigmoid(g).astype(BF16)


def _mix_in(x, g, wqk, wv, wgm, wlr, wg, bg, qkg, gsum, cos, sin):
    seq_tiles = SEQ // MIX_TM
    row = lambda w: pl.BlockSpec((MIX_TM, w), lambda i: (i, 0))
    tab = pl.BlockSpec((MIX_TM, LANES), lambda i: (i % seq_tiles, 0))
    v_w = ATTN_KV_HEADS * GROUP_Q_W
    return pl.pallas_call(
        _mix_in_body,
        grid=(TOKENS // MIX_TM,),
        in_specs=[row(D_MODEL), _resident((1, D_MODEL)), _resident(wqk.shape),
                  _resident(wv.shape), _resident(wgm.shape), _resident(wlr.shape),
                  _resident(wg.shape), _resident(bg.shape), _resident(qkg.shape),
                  _resident(gsum.shape), tab, tab],
        out_specs=[
            row(ATTN_Q_W),
            pl.BlockSpec((1, ATTN_KV_W, MIX_TM),
                         lambda i: (i // seq_tiles, 0, i % seq_tiles)),
            row(v_w), row(2 * GMLP_W), row(2 * LRU_W), row(N_BRANCH * D_MODEL)],
        out_shape=[
            jax.ShapeDtypeStruct((TOKENS, ATTN_Q_W), BF16),
            jax.ShapeDtypeStruct((BATCH, ATTN_KV_W, SEQ), BF16),
            jax.ShapeDtypeStruct((TOKENS, v_w), BF16),
            jax.ShapeDtypeStruct((TOKENS, 2 * GMLP_W), F32),
            jax.ShapeDtypeStruct((TOKENS, 2 * LRU_W), F32),
            jax.ShapeDtypeStruct((TOKENS, N_BRANCH * D_MODEL), BF16)],
        compiler_params=_params(1),
        name="mix_in",
    )(x, g, wqk, wv, wgm, wlr, wg, bg, qkg, gsum, cos, sin)


def _attn_body(q_ref, kt_ref, v_ref, o_ref, pv_rhs):
    head_of_lane = lax.broadcasted_iota(jnp.int32, (1, GROUP_Q_W), 1) // HEAD_DIM

    @pl.when(pl.program_id(2) == 0)
    def _():
        v = v_ref[...]
        ones = jnp.ones_like(v)
        for h in range(ATTN_GROUP):
            pv_rhs[h] = jnp.where(head_of_lane == h, v, ones)

    q = q_ref[...]
    zero = jnp.zeros_like(q)
    kt = kt_ref[0]
    kt_rep = jnp.concatenate([kt] * ATTN_GROUP, axis=0)
    out = jnp.zeros((ATTN_TQ, GROUP_Q_W), F32)
    for h in range(ATTN_GROUP):
        s = _dot(jnp.where(head_of_lane == h, q, zero), kt_rep)
        p = jnp.exp2(s - jnp.max(s, axis=-1, keepdims=True)).astype(BF16)
        pv = _dot(p, pv_rhs[h])
        row_sum = pltpu.roll(pv, HEAD_DIM, 1)
        out = out + jnp.where(head_of_lane == h, pv / row_sum, 0.0)
    o_ref[...] = out.astype(BF16)


def _attention(q, kt, v_rep):
    q_tiles = SEQ // ATTN_TQ
    return pl.pallas_call(
        _attn_body,
        grid=(BATCH, ATTN_KV_HEADS, q_tiles),
        in_specs=[
            pl.BlockSpec((ATTN_TQ, GROUP_Q_W), lambda b, j, i: (b * q_tiles + i, j)),
            pl.BlockSpec((1, HEAD_DIM, SEQ), lambda b, j, i: (b, j, 0)),
            pl.BlockSpec((SEQ, GROUP_Q_W), lambda b, j, i: (b, j)),
        ],
        out_specs=pl.BlockSpec((ATTN_TQ, GROUP_Q_W),
                               lambda b, j, i: (b * q_tiles + i, j)),
        out_shape=jax.ShapeDtypeStruct((TOKENS, ATTN_Q_W), BF16),
        scratch_shapes=[pltpu.VMEM((ATTN_GROUP, SEQ, GROUP_Q_W), BF16)],
        compiler_params=_params(3),
        name="gqa_attention",
    )(q, kt, v_rep)


def _gmlp_body(gm_ref, vn_ref, ws_ref, bias_ref, o_ref):
    u = jax.nn.gelu(gm_ref[:, :GMLP_W])
    v = _rms(jax.nn.gelu(gm_ref[:, GMLP_W:]), vn_ref[...]).astype(BF16)
    bias = bias_ref[...]
    for c in range(GMLP_TM // GMLP_CHUNK):
        rows = slice(c * GMLP_CHUNK, (c + 1) * GMLP_CHUNK)
        sv = jnp.concatenate(
            [_dot(ws_ref[g], v[rows, g * GMLP_GROUP_W:(g + 1) * GMLP_GROUP_W])
             for g in range(GMLP_GROUPS)], axis=1)
        o_ref[rows, :] = (u[rows] * (sv + bias)).astype(BF16)


def _gmlp(gm, vn, ws, bias):
    return pl.pallas_call(
        _gmlp_body,
        grid=(TOKENS // GMLP_TM,),
        in_specs=[pl.BlockSpec((GMLP_TM, 2 * GMLP_W), lambda i: (i, 0)),
                  _resident(vn.shape), _resident(ws.shape), _resident(bias.shape)],
        out_specs=pl.BlockSpec((GMLP_TM, GMLP_W), lambda i: (i, 0)),
        out_shape=jax.ShapeDtypeStruct((TOKENS, GMLP_W), BF16),
        compiler_params=_params(1),
        name="gmlp",
    )(gm, vn, ws, bias)


def _lru_body(lx_ref, ly_ref, cw_ref, cb_ref, w_ref, b_ref, lam_ref, o_ref,
              xpad, a_f, b_f, a_b, b_b, carry_f, carry_b):
    n_chunks = SEQ // LRU_ROWS
    halo_zeros = jnp.zeros((LRU_HALO, LANES), F32)
    xpad[0:LRU_HALO, :] = halo_zeros
    xpad[LRU_HALO + SEQ:LRU_HALO + SEQ + LRU_HALO, :] = halo_zeros
    xpad[LRU_HALO:LRU_HALO + SEQ, :] = lx_ref[...]

    neg_sp = -jax.nn.softplus(-lam_ref[...])
    cw = cw_ref[...]
    cb = cb_ref[...]
    bias = b_ref[...]
    scan_bufs = ((a_f, b_f), (a_b, b_b))

    def dense(c, carry):
        r0 = pl.multiple_of(c * LRU_ROWS, LRU_ROWS)
        xc = cb
        for j in range(CONV_W):
            off = LRU_HALO + j - CONV_W // 2
            xc = xc + cw[j:j + 1, :] * xpad[pl.ds(r0 + off, LRU_ROWS), :]
        pre = _dot(xc.astype(BF16), w_ref[0]) + bias
        for d in range(N_DIR):
            r = jax.nn.sigmoid(pre[:, (2 * d) * LANES:(2 * d + 1) * LANES])
            i = jax.nn.sigmoid(pre[:, (2 * d + 1) * LANES:(2 * d + 2) * LANES])
            log_a = LRU_C * r * neg_sp[d:d + 1, :]
            a_buf, b_buf = scan_bufs[d]
            a_buf[pl.ds(r0, LRU_ROWS), :] = jnp.exp(log_a)
            b_buf[pl.ds(r0, LRU_ROWS), :] = (
                jnp.sqrt(1.0 - jnp.exp(2.0 * log_a)) * (i * xc))
        return carry

    lax.fori_loop(0, n_chunks, dense, 0)
    tail = LRU_PAD_SEQ - SEQ
    for a_buf, b_buf in scan_bufs:
        a_buf[SEQ:LRU_PAD_SEQ, :] = jnp.ones((tail, LANES), F32)
        b_buf[SEQ:LRU_PAD_SEQ, :] = jnp.zeros((tail, LANES), F32)

    def seg(t):
        return pl.ds(t, SUBLANES, stride=LRU_SEG)

    def scan_step(k, carry):
        hf, pf, hb, pb = carry
        tf = k
        af = a_f[seg(tf), :]
        hf = af * hf + b_f[seg(tf), :]
        pf = af * pf
        b_f[seg(tf), :] = hf
        a_f[seg(tf), :] = pf
        tb = LRU_SEG - 1 - k
        ab = a_b[seg(tb), :]
        hb = ab * hb + b_b[seg(tb), :]
        pb = ab * pb
        b_b[seg(tb), :] = hb
        a_b[seg(tb), :] = pb
        return hf, pf, hb, pb

    z = jnp.zeros((SUBLANES, LANES), F32)
    o = jnp.ones((SUBLANES, LANES), F32)
    lax.fori_loop(0, LRU_SEG, scan_step, (z, o, z, o))

    row0 = jnp.zeros((1, LANES), F32)
    carry_f[0:1, :] = row0
    c = row0
    for s in range(1, SUBLANES):
        last = s * LRU_SEG - 1
        c = b_f[last:last + 1, :] + a_f[last:last + 1, :] * c
        carry_f[s:s + 1, :] = c
    carry_b[SUBLANES - 1:SUBLANES, :] = row0
    c = row0
    for s in range(SUBLANES - 2, -1, -1):
        first = (s + 1) * LRU_SEG
        c = b_b[first:first + 1, :] + a_b[first:first + 1, :] * c
        carry_b[s:s + 1, :] = c

    cf = carry_f[...]
    cbk = carry_b[...]

    def fix_step(t, carry):
        b_f[seg(t), :] = b_f[seg(t), :] + a_f[seg(t), :] * cf
        b_b[seg(t), :] = b_b[seg(t), :] + a_b[seg(t), :] * cbk
        return carry

    lax.fori_loop(0, LRU_SEG, fix_step, 0)

    def emit(c, carry):
        r0 = pl.multiple_of(c * LRU_ROWS, LRU_ROWS)
        rows = pl.ds(r0, LRU_ROWS)
        h = b_f[rows, :] + b_b[rows, :]
        o_ref[rows, :] = (h * jax.nn.gelu(ly_ref[rows, :])).astype(BF16)
        return carry

    lax.fori_loop(0, n_chunks, emit, 0)


def _lru(lr, cw, cb, w, b, lam):
    n_blk = LRU_W // LANES
    scan_buf = pltpu.VMEM((LRU_PAD_SEQ, LANES), F32)
    return pl.pallas_call(
        _lru_body,
        grid=(BATCH, n_blk),
        in_specs=[
            pl.BlockSpec((SEQ, LANES), lambda b_, c: (b_, c)),
            pl.BlockSpec((SEQ, LANES), lambda b_, c: (b_, n_blk + c)),
            pl.BlockSpec((CONV_W, LANES), lambda b_, c: (0, c)),
            pl.BlockSpec((1, LANES), lambda b_, c: (0, c)),
            pl.BlockSpec((1, LANES, 2 * N_DIR * LANES), lambda b_, c: (c, 0, 0)),
            pl.BlockSpec((1, 2 * N_DIR * LANES), lambda b_, c: (0, c)),
            pl.BlockSpec((N_DIR, LANES), lambda b_, c: (0, c)),
        ],
        out_specs=pl.BlockSpec((SEQ, LANES), lambda b_, c: (b_, c)),
        out_shape=jax.ShapeDtypeStruct((TOKENS, LRU_W), BF16),
        scratch_shapes=[
            pltpu.VMEM((SEQ + 2 * LRU_HALO, LANES), F32),
            scan_buf, scan_buf, scan_buf, scan_buf,
            pltpu.VMEM((SUBLANES, LANES), F32),
            pltpu.VMEM((SUBLANES, LANES), F32),
        ],
        compiler_params=_params(2),
        name="rg_lru",
    )(lr, lr, cw, cb, w, b, lam)


def _merge_body(x_ref, a_ref, g_ref, l_ref, gate_ref, wa_ref, wg_ref, wl_ref,
                wo_ref, o_ref):
    merged = jnp.zeros((MERGE_TM, D_MODEL), F32)
    for k, (br, w) in enumerate(((a_ref, wa_ref), (g_ref, wg_ref), (l_ref, wl_ref))):
        gate = gate_ref[:, k * D_MODEL:(k + 1) * D_MODEL].astype(F32)
        merged = merged + gate * _dot(br[...], w[...])
    o_ref[...] = x_ref[...] + _dot(merged.astype(BF16), wo_ref[...])


def _merge(x, attn_o, gmlp_o, lru_o, gates, wa, wg, wl, wo):
    row = lambda w: pl.BlockSpec((MERGE_TM, w), lambda i: (i, 0))
    return pl.pallas_call(
        _merge_body,
        grid=(TOKENS // MERGE_TM,),
        in_specs=[row(D_MODEL), row(ATTN_Q_W), row(GMLP_W), row(LRU_W),
                  row(N_BRANCH * D_MODEL), _resident(wa.shape), _resident(wg.shape),
                  _resident(wl.shape), _resident(wo.shape)],
        out_specs=row(D_MODEL),
        out_shape=jax.ShapeDtypeStruct((TOKENS, D_MODEL), F32),
        compiler_params=_params(1),
        name="merge",
    )(x, attn_o, gmlp_o, lru_o, gates, wa, wg, wl, wo)


def _xkv_body(mem_ref, g_ref, wkv_ref, kt_ref, v_ref):
    mn = _rms(mem_ref[0], g_ref[...]).astype(BF16)
    kv = _dot(mn, wkv_ref[...])
    kt_ref[0] = (kv[:, :D_MODEL] * (XATTN_HEAD_DIM ** -0.5)).T.astype(BF16)
    v_ref[0] = kv[:, D_MODEL:].astype(BF16)


def _xkv(mem, g, wkv):
    return pl.pallas_call(
        _xkv_body,
        grid=(BATCH,),
        in_specs=[pl.BlockSpec((1, MEM_LEN, D_MODEL), lambda b: (b, 0, 0)),
                  _resident((1, D_MODEL)), _resident(wkv.shape)],
        out_specs=[pl.BlockSpec((1, D_MODEL, MEM_LEN), lambda b: (b, 0, 0)),
                   pl.BlockSpec((1, MEM_LEN, D_MODEL), lambda b: (b, 0, 0))],
        out_shape=[jax.ShapeDtypeStruct((BATCH, D_MODEL, MEM_LEN), BF16),
                   jax.ShapeDtypeStruct((BATCH, MEM_LEN, D_MODEL), BF16)],
        compiler_params=_params(1),
        name="xattn_kv",
    )(mem, g, wkv)


def _xattn_body(x_ref, g_ref, wq_ref, kt_ref, v_ref, wo_ref, o_ref):
    x = x_ref[...]
    q = _dot(_rms(x, g_ref[...]).astype(BF16), wq_ref[...]).astype(BF16)
    heads = []
    for h in range(XATTN_HEADS):
        cols = slice(h * XATTN_HEAD_DIM, (h + 1) * XATTN_HEAD_DIM)
        s = _dot(q[:, cols], kt_ref[0, cols, :])
        p = jnp.exp(s - jnp.max(s, axis=-1, keepdims=True))
        l = jnp.sum(p, axis=-1, keepdims=True)
        heads.append((_dot(p.astype(BF16), v_ref[0, :, cols]) * (1.0 / l)).astype(BF16))
    o_ref[...] = x + _dot(jnp.concatenate(heads, axis=1), wo_ref[...])


def _xattn(x, g, wq, kt, v, wo):
    tiles = SEQ // XATTN_TM
    row = pl.BlockSpec((XATTN_TM, D_MODEL), lambda i: (i, 0))
    return pl.pallas_call(
        _xattn_body,
        grid=(TOKENS // XATTN_TM,),
        in_specs=[row, _resident((1, D_MODEL)), _resident(wq.shape),
                  pl.BlockSpec((1, D_MODEL, MEM_LEN), lambda i: (i // tiles, 0, 0)),
                  pl.BlockSpec((1, MEM_LEN, D_MODEL), lambda i: (i // tiles, 0, 0)),
                  _resident(wo.shape)],
        out_specs=row,
        out_shape=jax.ShapeDtypeStruct((TOKENS, D_MODEL), F32),
        compiler_params=_params(1),
        name="xattn",
    )(x, g, wq, kt, v, wo)


def _rope_tables():
    rows = SEQ // GRID_W
    row = jnp.repeat(jnp.arange(rows), GRID_W).astype(F32)
    col = jnp.tile(jnp.arange(GRID_W), rows).astype(F32)
    n_freq = HEAD_DIM // 4
    inv_freq = ROPE_THETA ** (-jnp.arange(n_freq, dtype=F32) / n_freq)
    ang_r = row[:, None] * inv_freq[None, :]
    ang_c = col[:, None] * inv_freq[None, :]
    cos = jnp.concatenate([jnp.cos(ang_r)] * 2 + [jnp.cos(ang_c)] * 2, axis=1)
    sin = jnp.concatenate([-jnp.sin(ang_r), jnp.sin(ang_r),
                           -jnp.sin(ang_c), jnp.sin(ang_c)], axis=1)
    rep = LANES // HEAD_DIM
    return jnp.tile(cos, (1, rep)), jnp.tile(sin, (1, rep))


def _block_diag(w):
    n, a, b = w.shape
    eye = jnp.eye(n, dtype=w.dtype)
    return (eye[:, None, :, None] * w[:, :, None, :]).reshape(n * a, n * b)


def kernel(x, mem, ffn1_norm, ffn1_w_in, ffn1_w_out, mix_norm, w_mix_in, b_gate,
           q_norm, k_norm, attn_up, gmlp_v_norm, gmlp_ws, gmlp_bs, gmlp_up,
           lru_conv_w, lru_conv_b, lru_wa, lru_ba, lru_wi, lru_bi, lru_lambda, lru_up,
           w_mix_out, xattn_norm, mem_norm, xattn_wq, xattn_wkv, xattn_wo,
           ffn2_norm, ffn2_w_in, ffn2_w_out, final_norm):
    assert x.shape == (BATCH, SEQ, D_MODEL) and mem.shape == (BATCH, MEM_LEN, D_MODEL)
    cos, sin = _rope_tables()
    gsum = _block_diag(jnp.full((QK_W // HEAD_DIM, HEAD_DIM, HEAD_DIM),
                                1.0 / HEAD_DIM, BF16))
    fin = final_norm.reshape(1, D_MODEL)
    n_blk = LRU_W // LANES
    heads_per_blk = LANES // LRU_HEAD_W

    h = x.reshape(TOKENS, D_MODEL)
    for l in range(DEPTH):
        row = lambda a: a[l].reshape(1, -1)
        wmi = w_mix_in[l].astype(BF16)
        s0 = ATTN_Q_W + ATTN_KV_W
        s1 = s0 + ATTN_KV_W
        s2 = s1 + 2 * GMLP_W
        s3 = s2 + 2 * LRU_W
        wv = wmi[:, s0:s1].reshape(D_MODEL, ATTN_KV_HEADS, 1, HEAD_DIM)
        wv_rep = jnp.broadcast_to(
            wv, (D_MODEL, ATTN_KV_HEADS, ATTN_GROUP, HEAD_DIM)).reshape(D_MODEL, -1)
        qkg = jnp.concatenate([jnp.tile(q_norm[l], ATTN_HEADS),
                               jnp.tile(k_norm[l], ATTN_KV_HEADS)]).reshape(1, QK_W)

        h = _ffn(h, row(ffn1_norm), ffn1_w_in[l].astype(BF16),
                 ffn1_w_out[l].astype(BF16), fin, final=False)

        q, kt, v_rep, gm, lr, gates = _mix_in(
            h, row(mix_norm), wmi[:, :s0], wv_rep, wmi[:, s1:s2], wmi[:, s2:s3],
            wmi[:, s3:], row(b_gate), qkg, gsum, cos, sin)

        attn_o = _attention(q, kt, v_rep)

        gm_bias = jnp.repeat(gmlp_bs[l].T, GMLP_GROUP_W, axis=1)
        gmlp_o = _gmlp(gm, row(gmlp_v_norm), gmlp_ws[l].astype(BF16), gm_bias)

        def blk(w):
            return jax.vmap(_block_diag)(
                w.reshape(n_blk, heads_per_blk, LRU_HEAD_W, LRU_HEAD_W))
        lru_w = jnp.concatenate(
            [blk(lru_wa[l, 0]), blk(lru_wi[l, 0]), blk(lru_wa[l, 1]), blk(lru_wi[l, 1])],
            axis=2).astype(BF16)
        lru_b = jnp.stack(
            [lru_ba[l, 0].reshape(n_blk, LANES), lru_bi[l, 0].reshape(n_blk, LANES),
             lru_ba[l, 1].reshape(n_blk, LANES), lru_bi[l, 1].reshape(n_blk, LANES)],
            axis=1).reshape(1, -1)
        lru_o = _lru(lr, lru_conv_w[l], row(lru_conv_b), lru_w, lru_b, lru_lambda[l])

        h = _merge(h, attn_o, gmlp_o, lru_o, gates, attn_up[l].astype(BF16),
                   gmlp_up[l].astype(BF16), lru_up[l].astype(BF16),
                   w_mix_out[l].astype(BF16))

        xkt, xv = _xkv(mem, row(mem_norm), xattn_wkv[l].astype(BF16))
        h = _xattn(h, row(xattn_norm), xattn_wq[l].astype(BF16), xkt, xv,
                   xattn_wo[l].astype(BF16))

        h = _ffn(h, row(ffn2_norm), ffn2_w_in[l].astype(BF16),
                 ffn2_w_out[l].astype(BF16), fin, final=(l == DEPTH - 1))
    return h.reshape(BATCH, SEQ, D_MODEL)
```

```python
import functools

import jax
import jax.numpy as jnp
from jax import lax
from jax.experimental import pallas as pl
from jax.experimental.pallas import tpu as pltpu

F32 = jnp.float32
BF16 = jnp.bfloat16

D_MODEL = 1024
BATCH = 4
SEQ = 4096
DEPTH = 2
TOKENS = BATCH * SEQ
MEM_LEN = 256
GRID_W = 64
EPS = 1e-6

ATTN_HEADS = 8
ATTN_KV_HEADS = 2
ATTN_GROUP = ATTN_HEADS // ATTN_KV_HEADS
HEAD_DIM = 64
ATTN_Q_W = ATTN_HEADS * HEAD_DIM
ATTN_KV_W = ATTN_KV_HEADS * HEAD_DIM
QK_W = ATTN_Q_W + ATTN_KV_W
GROUP_Q_W = ATTN_GROUP * HEAD_DIM
ROPE_THETA = 10000.0
LOG2_E = 1.4426950408889634

GMLP_W = 512
GMLP_GROUPS = 4
GMLP_GROUP_W = GMLP_W // GMLP_GROUPS
GMLP_CHUNK = 128

LRU_W = 512
LRU_HEADS = 8
LRU_HEAD_W = LRU_W // LRU_HEADS
CONV_W = 4
LRU_C = 8.0
N_DIR = 2
N_BRANCH = 3

XATTN_HEADS = 4
XATTN_HEAD_DIM = D_MODEL // XATTN_HEADS
D_FF = 2816

LANES = 128
SUBLANES = 8
VMEM_LIMIT_BYTES = 56 * 1024 * 1024

FFN_TM = 512
FFN_CHUNK = 256
MIX_TM = 512
ATTN_TQ = 512
ATTN_CHAIN_Q = 256
ATTN_SLOTS = 4
GMLP_TM = 512
MERGE_TM = 512
XATTN_TM = 512
LRU_ROWS = 512
LRU_SCAN_UNROLL = 4
LRU_SEG = 516
LRU_PAD_SEQ = SUBLANES * LRU_SEG
LRU_HALO = SUBLANES


def _resident(shape):
    nd = len(shape)
    return pl.BlockSpec(shape, lambda *_: (0,) * nd, pipeline_mode=pl.Buffered(1))


def _params(n_grid_axes):
    return pltpu.CompilerParams(
        dimension_semantics=("arbitrary",) * n_grid_axes,
        vmem_limit_bytes=VMEM_LIMIT_BYTES,
    )


def _rms(x, g):
    ms = jnp.mean(x * x, axis=-1, keepdims=True)
    return x * lax.rsqrt(ms + EPS) * g


def _dot(a, b):
    return jnp.dot(a, b, preferred_element_type=F32)


def _ffn_body(x_ref, g_ref, win_ref, wout_ref, fin_ref, o_ref, *, final):
    x = x_ref[...]
    xn = _rms(x, g_ref[...]).astype(BF16)
    acc = jnp.zeros(x.shape, F32)
    for c in range(D_FF // FFN_CHUNK):
        lo = c * FFN_CHUNK
        a = _dot(xn, win_ref[:, lo:lo + FFN_CHUNK])
        b = _dot(xn, win_ref[:, D_FF + lo:D_FF + lo + FFN_CHUNK])
        h = (a * jax.nn.sigmoid(a) * b).astype(BF16)
        acc = acc + _dot(h, wout_ref[lo:lo + FFN_CHUNK, :])
    y = x + 0.5 * acc
    if final:
        y = _rms(y, fin_ref[...])
    o_ref[...] = y


def _ffn(x, g, w_in, w_out, fin, *, final):
    tile = pl.BlockSpec((FFN_TM, D_MODEL), lambda i: (i, 0))
    return pl.pallas_call(
        functools.partial(_ffn_body, final=final),
        grid=(TOKENS // FFN_TM,),
        in_specs=[tile, _resident((1, D_MODEL)), _resident(w_in.shape),
                  _resident(w_out.shape), _resident((1, D_MODEL))],
        out_specs=tile,
        out_shape=jax.ShapeDtypeStruct((TOKENS, D_MODEL), F32),
        compiler_params=_params(1),
        name="ffn_final" if final else "ffn",
    )(x, g, w_in, w_out, fin)


def _lane_partner(x, d):
    lane = lax.broadcasted_iota(jnp.int32, x.shape, 1)
    up = pltpu.roll(x, LANES - d, 1)
    down = pltpu.roll(x, d, 1)
    return jnp.where((lane & d) == 0, up, down)


def _head_mean(x):
    d = HEAD_DIM // 2
    while d >= 1:
        x = x + _lane_partner(x, d)
        d //= 2
    return x * (1.0 / HEAD_DIM)


def _mix_in_body(x_ref, g_ref, wqk_ref, wv_ref, wgm_ref, wlr_ref, wg_ref,
                 bg_ref, qkg_ref, cos_ref, sin_ref,
                 q_ref, k_ref, vt_ref, gm_ref, lr_ref, gate_ref):
    xn = _rms(x_ref[...], g_ref[...]).astype(BF16)

    qk = _dot(xn, wqk_ref[...])
    cos = cos_ref[...]
    sin = sin_ref[...]
    cols = []
    for c in range(QK_W // LANES):
        lanes = slice(c * LANES, (c + 1) * LANES)
        t = qk[:, lanes]
        t = t * lax.rsqrt(_head_mean(t * t) + EPS) * qkg_ref[:, lanes]
        cols.append(t * cos + _lane_partner(t, HEAD_DIM // 4) * sin)
    q_ref[...] = (jnp.concatenate(cols[:ATTN_Q_W // LANES], axis=1)
                  * (HEAD_DIM ** -0.5 * LOG2_E)).astype(BF16)
    k_ref[...] = cols[-1].astype(BF16)

    vt_ref[0] = _dot(xn, wv_ref[...]).T.astype(BF16)
    gm_ref[...] = _dot(xn, wgm_ref[...])
    lr_ref[...] = _dot(xn, wlr_ref[...])
    g = _dot(xn, wg_ref[...]) + bg_ref[...]
    gate_ref[...] = jax.nn.sigmoid(g).astype(BF16)


def _mix_in(x, g, wqk, wv, wgm, wlr, wg, bg, qkg, cos, sin):
    seq_tiles = SEQ // MIX_TM
    row = lambda w: pl.BlockSpec((MIX_TM, w), lambda i: (i, 0))
    tab = pl.BlockSpec((MIX_TM, LANES), lambda i: (i % seq_tiles, 0))
    return pl.pallas_call(
        _mix_in_body,
        grid=(TOKENS // MIX_TM,),
        in_specs=[row(D_MODEL), _resident((1, D_MODEL)), _resident(wqk.shape),
                  _resident(wv.shape), _resident(wgm.shape), _resident(wlr.shape),
                  _resident(wg.shape), _resident(bg.shape), _resident(qkg.shape),
                  tab, tab],
        out_specs=[
            row(ATTN_Q_W), row(ATTN_KV_W),
            pl.BlockSpec((1, ATTN_KV_W, MIX_TM),
                         lambda i: (i // seq_tiles, 0, i % seq_tiles)),
            row(2 * GMLP_W), row(2 * LRU_W), row(N_BRANCH * D_MODEL)],
        out_shape=[
            jax.ShapeDtypeStruct((TOKENS, ATTN_Q_W), BF16),
            jax.ShapeDtypeStruct((TOKENS, ATTN_KV_W), BF16),
            jax.ShapeDtypeStruct((BATCH, ATTN_KV_W, SEQ), BF16),
            jax.ShapeDtypeStruct((TOKENS, 2 * GMLP_W), F32),
            jax.ShapeDtypeStruct((TOKENS, 2 * LRU_W), F32),
            jax.ShapeDtypeStruct((TOKENS, N_BRANCH * D_MODEL), BF16)],
        compiler_params=_params(1),
        name="mix_in",
    )(x, g, wqk, wv, wgm, wlr, wg, bg, qkg, cos, sin)


ATTN_SUM_ROWS = 16


def _attn_body(q_ref, k_ref, vt_ref, o_ref, vt_aug, st_buf, pt_buf):
    @pl.when(pl.program_id(2) == 0)
    def _():
        vt_aug[0:HEAD_DIM, :] = vt_ref[0]
        vt_aug[HEAD_DIM:, :] = jnp.ones((ATTN_SUM_ROWS, SEQ), BF16)

    kv_head = lax.broadcasted_iota(jnp.int32, (ATTN_KV_W, 1), 0) // HEAD_DIM
    own_rows = kv_head == pl.program_id(1)
    qt = q_ref[...].astype(F32).T
    k = k_ref[...]

    n_sub = ATTN_TQ // ATTN_CHAIN_Q
    chains = [(sub, h) for sub in range(n_sub) for h in range(ATTN_GROUP)]
    col_max = {}

    def scores(c):
        sub, h = chains[c]
        qh = qt[h * HEAD_DIM:(h + 1) * HEAD_DIM,
                sub * ATTN_CHAIN_Q:(sub + 1) * ATTN_CHAIN_Q]
        w = jnp.where(own_rows, jnp.concatenate([qh] * ATTN_KV_HEADS, axis=0), 0.0)
        st = _dot(k, w.astype(BF16))
        col_max[c] = jnp.max(st, axis=0, keepdims=True)
        st_buf[c % ATTN_SLOTS] = st

    for c in range(min(ATTN_SLOTS, len(chains))):
        scores(c)
    outs = [[None] * ATTN_GROUP for _ in range(n_sub)]
    for c, (sub, h) in enumerate(chains):
        slot = c % ATTN_SLOTS
        pt_buf[slot] = jnp.exp2(st_buf[slot] - col_max[c]).astype(BF16)
        ot = _dot(vt_aug[...], pt_buf[slot])
        outs[sub][h] = ot[:HEAD_DIM] / ot[HEAD_DIM:HEAD_DIM + 1]
        if c + ATTN_SLOTS < len(chains):
            scores(c + ATTN_SLOTS)
    ot_all = jnp.concatenate([jnp.concatenate(o, axis=0) for o in outs], axis=1)
    o_ref[...] = ot_all.T.astype(BF16)


def _attention(q, k, vt):
    q_tiles = SEQ // ATTN_TQ
    return pl.pallas_call(
        _attn_body,
        grid=(BATCH, ATTN_KV_HEADS, q_tiles),
        in_specs=[
            pl.BlockSpec((ATTN_TQ, GROUP_Q_W), lambda b, j, i: (b * q_tiles + i, j)),
            pl.BlockSpec((SEQ, ATTN_KV_W), lambda b, j, i: (b, 0)),
            pl.BlockSpec((1, HEAD_DIM, SEQ), lambda b, j, i: (b, j, 0)),
        ],
        out_specs=pl.BlockSpec((ATTN_TQ, GROUP_Q_W),
                               lambda b, j, i: (b * q_tiles + i, j)),
        out_shape=jax.ShapeDtypeStruct((TOKENS, ATTN_Q_W), BF16),
        scratch_shapes=[pltpu.VMEM((HEAD_DIM + ATTN_SUM_ROWS, SEQ), BF16),
                        pltpu.VMEM((ATTN_SLOTS, SEQ, ATTN_CHAIN_Q), F32),
                        pltpu.VMEM((ATTN_SLOTS, SEQ, ATTN_CHAIN_Q), BF16)],
        compiler_params=_params(3),
        name="gqa_attention",
    )(q, k, vt)


def _gmlp_body(gm_ref, vn_ref, ws_ref, bias_ref, o_ref):
    u = jax.nn.gelu(gm_ref[:, :GMLP_W])
    v = _rms(jax.nn.gelu(gm_ref[:, GMLP_W:]), vn_ref[...]).astype(BF16)
    bias = bias_ref[...]
    for c in range(GMLP_TM // GMLP_CHUNK):
        rows = slice(c * GMLP_CHUNK, (c + 1) * GMLP_CHUNK)
        sv = jnp.concatenate(
            [_dot(ws_ref[g], v[rows, g * GMLP_GROUP_W:(g + 1) * GMLP_GROUP_W])
             for g in range(GMLP_GROUPS)], axis=1)
        o_ref[rows, :] = (u[rows] * (sv + bias)).astype(BF16)


def _gmlp(gm, vn, ws, bias):
    return pl.pallas_call(
        _gmlp_body,
        grid=(TOKENS // GMLP_TM,),
        in_specs=[pl.BlockSpec((GMLP_TM, 2 * GMLP_W), lambda i: (i, 0)),
                  _resident(vn.shape), _resident(ws.shape), _resident(bias.shape)],
        out_specs=pl.BlockSpec((GMLP_TM, GMLP_W), lambda i: (i, 0)),
        out_shape=jax.ShapeDtypeStruct((TOKENS, GMLP_W), BF16),
        compiler_params=_params(1),
        name="gmlp",
    )(gm, vn, ws, bias)


def _lru_body(lx_ref, ly_ref, cw_ref, cb_ref, w_ref, b_ref, lam_ref, o_ref,
              xpad, a_f, b_f, a_b, b_b, carry_f, carry_b):
    n_chunks = SEQ // LRU_ROWS
    halo_zeros = jnp.zeros((LRU_HALO, LANES), F32)
    xpad[0:LRU_HALO, :] = halo_zeros
    xpad[LRU_HALO + SEQ:LRU_HALO + SEQ + LRU_HALO, :] = halo_zeros
    xpad[LRU_HALO:LRU_HALO + SEQ, :] = lx_ref[...]

    neg_sp = -jax.nn.softplus(-lam_ref[...])
    cw = cw_ref[...]
    cb = cb_ref[...]
    bias = b_ref[...]
    scan_bufs = ((a_f, b_f), (a_b, b_b))

    def dense(c, carry):
        r0 = pl.multiple_of(c * LRU_ROWS, LRU_ROWS)
        xc = cb
        for j in range(CONV_W):
            off = LRU_HALO + j - CONV_W // 2
            xc = xc + cw[j:j + 1, :] * xpad[pl.ds(r0 + off, LRU_ROWS), :]
        pre = _dot(xc.astype(BF16), w_ref[0]) + bias
        for d in range(N_DIR):
            r = jax.nn.sigmoid(pre[:, (2 * d) * LANES:(2 * d + 1) * LANES])
            i = jax.nn.sigmoid(pre[:, (2 * d + 1) * LANES:(2 * d + 2) * LANES])
            a = jnp.exp(LRU_C * r * neg_sp[d:d + 1, :])
            g2 = 1.0 - a * a
            gain = jnp.where(g2 > 0.0, g2 * lax.rsqrt(g2), 0.0)
            a_buf, b_buf = scan_bufs[d]
            a_buf[pl.ds(r0, LRU_ROWS), :] = a
            b_buf[pl.ds(r0, LRU_ROWS), :] = gain * (i * xc)
        return carry

    lax.fori_loop(0, n_chunks, dense, 0)
    tail = LRU_PAD_SEQ - SEQ
    for a_buf, b_buf in scan_bufs:
        a_buf[SEQ:LRU_PAD_SEQ, :] = jnp.ones((tail, LANES), F32)
        b_buf[SEQ:LRU_PAD_SEQ, :] = jnp.zeros((tail, LANES), F32)

    def seg(t):
        return pl.ds(t, SUBLANES, stride=LRU_SEG)

    def scan_step(k, carry):
        hf, pf, hb, pb = carry
        tf = k
        af = a_f[seg(tf), :]
        hf = af * hf + b_f[seg(tf), :]
        pf = af * pf
        b_f[seg(tf), :] = hf
        a_f[seg(tf), :] = pf
        tb = LRU_SEG - 1 - k
        ab = a_b[seg(tb), :]
        hb = ab * hb + b_b[seg(tb), :]
        pb = ab * pb
        b_b[seg(tb), :] = hb
        a_b[seg(tb), :] = pb
        return hf, pf, hb, pb

    z = jnp.zeros((SUBLANES, LANES), F32)
    o = jnp.ones((SUBLANES, LANES), F32)
    lax.fori_loop(0, LRU_SEG, scan_step, (z, o, z, o), unroll=LRU_SCAN_UNROLL)

    row0 = jnp.zeros((1, LANES), F32)
    carry_f[0:1, :] = row0
    c = row0
    for s in range(1, SUBLANES):
        last = s * LRU_SEG - 1
        c = b_f[last:last + 1, :] + a_f[last:last + 1, :] * c
        carry_f[s:s + 1, :] = c
    carry_b[SUBLANES - 1:SUBLANES, :] = row0
    c = row0
    for s in range(SUBLANES - 2, -1, -1):
        first = (s + 1) * LRU_SEG
        c = b_b[first:first + 1, :] + a_b[first:first + 1, :] * c
        carry_b[s:s + 1, :] = c

    seg_shift = LRU_SEG - LRU_ROWS
    row_in_chunk = lax.broadcasted_iota(jnp.int32, (LRU_ROWS, LANES), 0)

    def emit(c, carry):
        r0 = pl.multiple_of(c * LRU_ROWS, LRU_ROWS)
        rows = pl.ds(r0, LRU_ROWS)
        prev = pl.ds(jnp.maximum(c - 1, 0), 1)
        in_prev = row_in_chunk < c * seg_shift
        cf = jnp.where(in_prev, carry_f[prev, :], carry_f[pl.ds(c, 1), :])
        cb_ = jnp.where(in_prev, carry_b[prev, :], carry_b[pl.ds(c, 1), :])
        h = (b_f[rows, :] + a_f[rows, :] * cf) + (b_b[rows, :] + a_b[rows, :] * cb_)
        o_ref[rows, :] = (h * jax.nn.gelu(ly_ref[rows, :])).astype(BF16)
        return carry

    lax.fori_loop(0, n_chunks, emit, 0)


def _lru(lr, cw, cb, w, b, lam):
    n_blk = LRU_W // LANES
    scan_buf = pltpu.VMEM((LRU_PAD_SEQ, LANES), F32)
    return pl.pallas_call(
        _lru_body,
        grid=(BATCH, n_blk),
        in_specs=[
            pl.BlockSpec((SEQ, LANES), lambda b_, c: (b_, c)),
            pl.BlockSpec((SEQ, LANES), lambda b_, c: (b_, n_blk + c)),
            pl.BlockSpec((CONV_W, LANES), lambda b_, c: (0, c)),
            pl.BlockSpec((1, LANES), lambda b_, c: (0, c)),
            pl.BlockSpec((1, LANES, 2 * N_DIR * LANES), lambda b_, c: (c, 0, 0)),
            pl.BlockSpec((1, 2 * N_DIR * LANES), lambda b_, c: (0, c)),
            pl.BlockSpec((N_DIR, LANES), lambda b_, c: (0, c)),
        ],
        out_specs=pl.BlockSpec((SEQ, LANES), lambda b_, c: (b_, c)),
        out_shape=jax.ShapeDtypeStruct((TOKENS, LRU_W), BF16),
        scratch_shapes=[
            pltpu.VMEM((SEQ + 2 * LRU_HALO, LANES), F32),
            scan_buf, scan_buf, scan_buf, scan_buf,
            pltpu.VMEM((SUBLANES, LANES), F32),
            pltpu.VMEM((SUBLANES, LANES), F32),
        ],
        compiler_params=_params(2),
        name="rg_lru",
    )(lr, lr, cw, cb, w, b, lam)


def _merge_body(x_ref, a_ref, g_ref, l_ref, gate_ref, wa_ref, wg_ref, wl_ref,
                wo_ref, o_ref):
    merged = jnp.zeros((MERGE_TM, D_MODEL), F32)
    for k, (br, w) in enumerate(((a_ref, wa_ref), (g_ref, wg_ref), (l_ref, wl_ref))):
        gate = gate_ref[:, k * D_MODEL:(k + 1) * D_MODEL].astype(F32)
        merged = merged + gate * _dot(br[...], w[...])
    o_ref[...] = x_ref[...] + _dot(merged.astype(BF16), wo_ref[...])


def _merge(x, attn_o, gmlp_o, lru_o, gates, wa, wg, wl, wo):
    row = lambda w: pl.BlockSpec((MERGE_TM, w), lambda i: (i, 0))
    return pl.pallas_call(
        _merge_body,
        grid=(TOKENS // MERGE_TM,),
        in_specs=[row(D_MODEL), row(ATTN_Q_W), row(GMLP_W), row(LRU_W),
                  row(N_BRANCH * D_MODEL), _resident(wa.shape), _resident(wg.shape),
                  _resident(wl.shape), _resident(wo.shape)],
        out_specs=row(D_MODEL),
        out_shape=jax.ShapeDtypeStruct((TOKENS, D_MODEL), F32),
        compiler_params=_params(1),
        name="merge",
    )(x, attn_o, gmlp_o, lru_o, gates, wa, wg, wl, wo)


def _xkv_body(mem_ref, g_ref, wkv_ref, kt_ref, v_ref):
    mn = _rms(mem_ref[0], g_ref[...]).astype(BF16)
    kv = _dot(mn, wkv_ref[...])
    kt_ref[0] = (kv[:, :D_MODEL] * (XATTN_HEAD_DIM ** -0.5)).T.astype(BF16)
    v_ref[0] = kv[:, D_MODEL:].astype(BF16)


def _xkv(mem, g, wkv):
    return pl.pallas_call(
        _xkv_body,
        grid=(BATCH,),
        in_specs=[pl.BlockSpec((1, MEM_LEN, D_MODEL), lambda b: (b, 0, 0)),
                  _resident((1, D_MODEL)), _resident(wkv.shape)],
        out_specs=[pl.BlockSpec((1, D_MODEL, MEM_LEN), lambda b: (b, 0, 0)),
                   pl.BlockSpec((1, MEM_LEN, D_MODEL), lambda b: (b, 0, 0))],
        out_shape=[jax.ShapeDtypeStruct((BATCH, D_MODEL, MEM_LEN), BF16),
                   jax.ShapeDtypeStruct((BATCH, MEM_LEN, D_MODEL), BF16)],
        compiler_params=_params(1),
        name="xattn_kv",
    )(mem, g, wkv)


def _xattn_body(x_ref, g_ref, wq_ref, kt_ref, v_ref, wo_ref, o_ref):
    x = x_ref[...]
    q = _dot(_rms(x, g_ref[...]).astype(BF16), wq_ref[...]).astype(BF16)
    heads = []
    for h in range(XATTN_HEADS):
        cols = slice(h * XATTN_HEAD_DIM, (h + 1) * XATTN_HEAD_DIM)
        s = _dot(q[:, cols], kt_ref[0, cols, :])
        p = jnp.exp(s - jnp.max(s, axis=-1, keepdims=True))
        l = jnp.sum(p, axis=-1, keepdims=True)
        heads.append((_dot(p.astype(BF16), v_ref[0, :, cols]) * (1.0 / l)).astype(BF16))
    o_ref[...] = x + _dot(jnp.concatenate(heads, axis=1), wo_ref[...])


def _xattn(x, g, wq, kt, v, wo):
    tiles = SEQ // XATTN_TM
    row = pl.BlockSpec((XATTN_TM, D_MODEL), lambda i: (i, 0))
    return pl.pallas_call(
        _xattn_body,
        grid=(TOKENS // XATTN_TM,),
        in_specs=[row, _resident((1, D_MODEL)), _resident(wq.shape),
                  pl.BlockSpec((1, D_MODEL, MEM_LEN), lambda i: (i // tiles, 0, 0)),
                  pl.BlockSpec((1, MEM_LEN, D_MODEL), lambda i: (i // tiles, 0, 0)),
                  _resident(wo.shape)],
        out_specs=row,
        out_shape=jax.ShapeDtypeStruct((TOKENS, D_MODEL), F32),
        compiler_params=_params(1),
        name="xattn",
    )(x, g, wq, kt, v, wo)


def _rope_tables():
    rows = SEQ // GRID_W
    row = jnp.repeat(jnp.arange(rows), GRID_W).astype(F32)
    col = jnp.tile(jnp.arange(GRID_W), rows).astype(F32)
    n_freq = HEAD_DIM // 4
    inv_freq = ROPE_THETA ** (-jnp.arange(n_freq, dtype=F32) / n_freq)
    ang_r = row[:, None] * inv_freq[None, :]
    ang_c = col[:, None] * inv_freq[None, :]
    cos = jnp.concatenate([jnp.cos(ang_r)] * 2 + [jnp.cos(ang_c)] * 2, axis=1)
    sin = jnp.concatenate([-jnp.sin(ang_r), jnp.sin(ang_r),
                           -jnp.sin(ang_c), jnp.sin(ang_c)], axis=1)
    rep = LANES // HEAD_DIM
    return jnp.tile(cos, (1, rep)), jnp.tile(sin, (1, rep))


def _block_diag(w):
    n, a, b = w.shape
    eye = jnp.eye(n, dtype=w.dtype)
    return (eye[:, None, :, None] * w[:, :, None, :]).reshape(n * a, n * b)


def kernel(x, mem, ffn1_norm, ffn1_w_in, ffn1_w_out, mix_norm, w_mix_in, b_gate,
           q_norm, k_norm, attn_up, gmlp_v_norm, gmlp_ws, gmlp_bs, gmlp_up,
           lru_conv_w, lru_conv_b, lru_wa, lru_ba, lru_wi, lru_bi, lru_lambda, lru_up,
           w_mix_out, xattn_norm, mem_norm, xattn_wq, xattn_wkv, xattn_wo,
           ffn2_norm, ffn2_w_in, ffn2_w_out, final_norm):
    assert x.shape == (BATCH, SEQ, D_MODEL) and mem.shape == (BATCH, MEM_LEN, D_MODEL)
    cos, sin = _rope_tables()
    fin = final_norm.reshape(1, D_MODEL)
    n_blk = LRU_W // LANES
    heads_per_blk = LANES // LRU_HEAD_W

    h = x.reshape(TOKENS, D_MODEL)
    for l in range(DEPTH):
        row = lambda a: a[l].reshape(1, -1)
        wmi = w_mix_in[l].astype(BF16)
        s0 = ATTN_Q_W + ATTN_KV_W
        s1 = s0 + ATTN_KV_W
        s2 = s1 + 2 * GMLP_W
        s3 = s2 + 2 * LRU_W
        qkg = jnp.concatenate([jnp.tile(q_norm[l], ATTN_HEADS),
                               jnp.tile(k_norm[l], ATTN_KV_HEADS)]).reshape(1, QK_W)

        h = _ffn(h, row(ffn1_norm), ffn1_w_in[l].astype(BF16),
                 ffn1_w_out[l].astype(BF16), fin, final=False)

        q, k, vt, gm, lr, gates = _mix_in(
            h, row(mix_norm), wmi[:, :s0], wmi[:, s0:s1], wmi[:, s1:s2], wmi[:, s2:s3],
            wmi[:, s3:], row(b_gate), qkg, cos, sin)

        attn_o = _attention(q, k, vt)

        gm_bias = jnp.repeat(gmlp_bs[l].T, GMLP_GROUP_W, axis=1)
        gmlp_o = _gmlp(gm, row(gmlp_v_norm), gmlp_ws[l].astype(BF16), gm_bias)

        def blk(w):
            return jax.vmap(_block_diag)(
                w.reshape(n_blk, heads_per_blk, LRU_HEAD_W, LRU_HEAD_W))
        lru_w = jnp.concatenate(
            [blk(lru_wa[l, 0]), blk(lru_wi[l, 0]), blk(lru_wa[l, 1]), blk(lru_wi[l, 1])],
            axis=2).astype(BF16)
        lru_b = jnp.stack(
            [lru_ba[l, 0].reshape(n_blk, LANES), lru_bi[l, 0].reshape(n_blk, LANES),
             lru_ba[l, 1].reshape(n_blk, LANES), lru_bi[l, 1].reshape(n_blk, LANES)],
            axis=1).reshape(1, -1)
        lru_o = _lru(lr, lru_conv_w[l], row(lru_conv_b), lru_w, lru_b, lru_lambda[l])

        h = _merge(h, attn_o, gmlp_o, lru_o, gates, attn_up[l].astype(BF16),
                   gmlp_up[l].astype(BF16), lru_up[l].astype(BF16),
                   w_mix_out[l].astype(BF16))

        xkt, xv = _xkv(mem, row(mem_norm), xattn_wkv[l].astype(BF16))
        h = _xattn(h, row(xattn_norm), xattn_wq[l].astype(BF16), xkt, xv,
                   xattn_wo[l].astype(BF16))

        h = _ffn(h, row(ffn2_norm), ffn2_w_in[l].astype(BF16),
                 ffn2_w_out[l].astype(BF16), fin, final=(l == DEPTH - 1))
    return h.reshape(BATCH, SEQ, D_MODEL)
```

```python
import functools

import jax
import jax.numpy as jnp
from jax import lax
from jax.experimental import pallas as pl
from jax.experimental.pallas import tpu as pltpu

F32 = jnp.float32
BF16 = jnp.bfloat16

D_MODEL = 1024
BATCH = 4
SEQ = 4096
DEPTH = 2
TOKENS = BATCH * SEQ
MEM_LEN = 256
GRID_W = 64
EPS = 1e-6

ATTN_HEADS = 8
ATTN_KV_HEADS = 2
ATTN_GROUP = ATTN_HEADS // ATTN_KV_HEADS
HEAD_DIM = 64
ATTN_Q_W = ATTN_HEADS * HEAD_DIM
ATTN_KV_W = ATTN_KV_HEADS * HEAD_DIM
QK_W = ATTN_Q_W + ATTN_KV_W
GROUP_Q_W = ATTN_GROUP * HEAD_DIM
ROPE_THETA = 10000.0
LOG2_E = 1.4426950408889634

GMLP_W = 512
GMLP_GROUPS = 4
GMLP_GROUP_W = GMLP_W // GMLP_GROUPS
GMLP_CHUNK = 128

LRU_W = 512
LRU_HEADS = 8
LRU_HEAD_W = LRU_W // LRU_HEADS
CONV_W = 4
LRU_C = 8.0
N_DIR = 2
N_BRANCH = 3

XATTN_HEADS = 4
XATTN_HEAD_DIM = D_MODEL // XATTN_HEADS
D_FF = 2816

MIX_V_LO = ATTN_Q_W + ATTN_KV_W
MIX_GM_LO = MIX_V_LO + ATTN_KV_W
MIX_LR_LO = MIX_GM_LO + 2 * GMLP_W
MIX_GATE_LO = MIX_LR_LO + 2 * LRU_W

LANES = 128
SUBLANES = 8
VMEM_LIMIT_BYTES = 56 * 1024 * 1024

FFN_TM = 512
FFN_CHUNK = 256
MIX_TM = 512
ATTN_TQ = 512
ATTN_CHAIN_Q = 256
ATTN_SLOTS = 4
ATTN_PAIR = 2
GMLP_TM = 512
MERGE_TM = 512
LRU_ROWS = 512
LRU_SCAN_UNROLL = 4
LRU_SEG = 516
LRU_PAD_SEQ = SUBLANES * LRU_SEG
LRU_HALO = SUBLANES


def _resident(shape):
    nd = len(shape)
    return pl.BlockSpec(shape, lambda *_: (0,) * nd, pipeline_mode=pl.Buffered(1))


def _params(n_grid_axes):
    return pltpu.CompilerParams(
        dimension_semantics=("arbitrary",) * n_grid_axes,
        vmem_limit_bytes=VMEM_LIMIT_BYTES,
    )


def _rms(x, g):
    ms = jnp.mean(x * x, axis=-1, keepdims=True)
    return x * lax.rsqrt(ms + EPS) * g


def _dot(a, b):
    return jnp.dot(a, b, preferred_element_type=F32)


def _ffn_body(x_ref, g_ref, win_ref, wout_ref, fin_ref, o_ref, *, final):
    x = x_ref[...]
    xn = _rms(x, g_ref[...]).astype(BF16)
    acc = jnp.zeros(x.shape, F32)
    for c in range(D_FF // FFN_CHUNK):
        lo = c * FFN_CHUNK
        a = _dot(xn, win_ref[:, lo:lo + FFN_CHUNK])
        b = _dot(xn, win_ref[:, D_FF + lo:D_FF + lo + FFN_CHUNK])
        h = (a * jax.nn.sigmoid(a) * b).astype(BF16)
        acc = acc + _dot(h, wout_ref[lo:lo + FFN_CHUNK, :])
    y = x + 0.5 * acc
    if final:
        y = _rms(y, fin_ref[...])
    o_ref[...] = y


def _ffn(x, g, w_in, w_out, fin, *, final):
    tile = pl.BlockSpec((FFN_TM, D_MODEL), lambda i: (i, 0))
    return pl.pallas_call(
        functools.partial(_ffn_body, final=final),
        grid=(TOKENS // FFN_TM,),
        in_specs=[tile, _resident((1, D_MODEL)), _resident(w_in.shape),
                  _resident(w_out.shape), _resident((1, D_MODEL))],
        out_specs=tile,
        out_shape=jax.ShapeDtypeStruct((TOKENS, D_MODEL), F32),
        compiler_params=_params(1),
        name="ffn_final" if final else "ffn",
    )(x, g, w_in, w_out, fin)


def _lane_partner(x, d):
    lane = lax.broadcasted_iota(jnp.int32, x.shape, 1)
    up = pltpu.roll(x, LANES - d, 1)
    down = pltpu.roll(x, d, 1)
    return jnp.where((lane & d) == 0, up, down)


def _head_mean(x):
    d = HEAD_DIM // 2
    while d >= 1:
        x = x + _lane_partner(x, d)
        d //= 2
    return x * (1.0 / HEAD_DIM)


def _mix_in_body(x_ref, g_ref, w_ref, qkg_ref, cos_ref, sin_ref,
                 q_ref, k_ref, vt_ref, gm_ref, lr_ref):
    xn = _rms(x_ref[...], g_ref[...]).astype(BF16)

    def proj(lo, hi):
        return _dot(xn, w_ref[:, lo:hi])

    qk = proj(0, MIX_V_LO)
    cos = cos_ref[...]
    sin = sin_ref[...]
    cols = []
    for c in range(QK_W // LANES):
        lanes = slice(c * LANES, (c + 1) * LANES)
        t = qk[:, lanes]
        t = t * lax.rsqrt(_head_mean(t * t) + EPS) * qkg_ref[:, lanes]
        cols.append(t * cos + _lane_partner(t, HEAD_DIM // 4) * sin)
    q_ref[...] = (jnp.concatenate(cols[:ATTN_Q_W // LANES], axis=1)
                  * (HEAD_DIM ** -0.5 * LOG2_E)).astype(BF16)
    k_ref[...] = cols[-1].astype(BF16)

    vt_ref[0] = proj(MIX_V_LO, MIX_GM_LO).T.astype(BF16)
    gm_ref[...] = proj(MIX_GM_LO, MIX_LR_LO)
    lr_ref[...] = proj(MIX_LR_LO, MIX_GATE_LO)


def _mix_in(x, g, w, qkg, cos, sin):
    seq_tiles = SEQ // MIX_TM
    row = lambda w_: pl.BlockSpec((MIX_TM, w_), lambda i: (i, 0))
    tab = pl.BlockSpec((MIX_TM, LANES), lambda i: (i % seq_tiles, 0))
    return pl.pallas_call(
        _mix_in_body,
        grid=(TOKENS // MIX_TM,),
        in_specs=[row(D_MODEL), _resident((1, D_MODEL)),
                  _resident((D_MODEL, MIX_GATE_LO)),
                  _resident(qkg.shape), tab, tab],
        out_specs=[
            row(ATTN_Q_W), row(ATTN_KV_W),
            pl.BlockSpec((1, ATTN_KV_W, MIX_TM),
                         lambda i: (i // seq_tiles, 0, i % seq_tiles)),
            row(2 * GMLP_W), row(2 * LRU_W)],
        out_shape=[
            jax.ShapeDtypeStruct((TOKENS, ATTN_Q_W), BF16),
            jax.ShapeDtypeStruct((TOKENS, ATTN_KV_W), BF16),
            jax.ShapeDtypeStruct((BATCH, ATTN_KV_W, SEQ), BF16),
            jax.ShapeDtypeStruct((TOKENS, 2 * GMLP_W), F32),
            jax.ShapeDtypeStruct((TOKENS, 2 * LRU_W), F32)],
        compiler_params=_params(1),
        name="mix_in",
    )(x, g, w, qkg, cos, sin)


ATTN_SUM_ROWS = 16
ATTN_MAX_EXPONENT_SPAN = 64.0


def _attn_body(q_ref, k_ref, vt_ref, o_ref, vt_aug, k_aug, kmax_ref, st_buf, pt_buf):
    own_lanes = (lax.broadcasted_iota(jnp.int32, (1, ATTN_KV_W), 1) // HEAD_DIM
                 == pl.program_id(1))

    @pl.when(pl.program_id(2) == 0)
    def _():
        vt_aug[0:HEAD_DIM, :] = vt_ref[0]
        vt_aug[HEAD_DIM:, :] = jnp.ones((ATTN_SUM_ROWS, SEQ), BF16)
        kb = k_ref[...]
        k_aug[:, :ATTN_KV_W] = kb
        k_aug[:, ATTN_KV_W:] = jnp.ones((SEQ, ATTN_KV_W), BF16)
        kf = kb.astype(F32)
        norm2 = jnp.sum(jnp.where(own_lanes, kf * kf, 0.0), axis=1, keepdims=True)
        kmax = jnp.sqrt(jnp.max(norm2, axis=0, keepdims=True))
        kmax_ref[...] = jnp.broadcast_to(kmax, kmax_ref.shape)

    kv_head = lax.broadcasted_iota(jnp.int32, (ATTN_KV_W, 1), 0) // HEAD_DIM
    own_rows = kv_head == pl.program_id(1)
    qt = q_ref[...].astype(F32).T

    n_sub = ATTN_TQ // ATTN_CHAIN_Q
    chains = [(sub, h) for sub in range(n_sub) for h in range(ATTN_GROUP)]

    def q_head(c):
        sub, h = chains[c]
        return qt[h * HEAD_DIM:(h + 1) * HEAD_DIM,
                  sub * ATTN_CHAIN_Q:(sub + 1) * ATTN_CHAIN_Q]

    def q_weights(c):
        return jnp.where(own_rows,
                         jnp.concatenate([q_head(c)] * ATTN_KV_HEADS, axis=0), 0.0)

    def finish(outs):
        ot_all = jnp.concatenate([jnp.concatenate(o, axis=0) for o in outs], axis=1)
        o_ref[...] = ot_all.T.astype(BF16)

    kmax = kmax_ref[0:1, 0:ATTN_CHAIN_Q]
    bounds = [jnp.sqrt(jnp.sum(q_head(c) * q_head(c), axis=0, keepdims=True)) * kmax
              for c in range(len(chains))]
    worst = bounds[0]
    for b in bounds[1:]:
        worst = jnp.maximum(worst, b)
    shift_is_safe = jnp.max(worst) * 2.0 <= ATTN_MAX_EXPONENT_SPAN

    @pl.when(shift_is_safe)
    def _():
        is_shift_row = lax.broadcasted_iota(jnp.int32, (ATTN_KV_W, 1), 0) == 0
        outs = [[None] * ATTN_GROUP for _ in range(n_sub)]
        for first in range(0, len(chains), ATTN_PAIR):
            pair = range(first, first + ATTN_PAIR)
            for c in pair:
                tail = jnp.where(is_shift_row, -bounds[c], 0.0)
                w = jnp.concatenate([q_weights(c), tail], axis=0).astype(BF16)
                pt_buf[c % ATTN_SLOTS] = jnp.exp2(_dot(k_aug[...], w)).astype(BF16)
            for c in pair:
                sub, h = chains[c]
                ot = _dot(vt_aug[...], pt_buf[c % ATTN_SLOTS])
                outs[sub][h] = ot[:HEAD_DIM] / ot[HEAD_DIM:HEAD_DIM + 1]
        finish(outs)

    @pl.when(jnp.logical_not(shift_is_safe))
    def _():
        k = k_ref[...]
        col_max = {}

        def scores(c):
            st = _dot(k, q_weights(c).astype(BF16))
            col_max[c] = jnp.max(st, axis=0, keepdims=True)
            st_buf[c % ATTN_SLOTS] = st

        for c in range(min(ATTN_SLOTS, len(chains))):
            scores(c)
        outs = [[None] * ATTN_GROUP for _ in range(n_sub)]
        for c, (sub, h) in enumerate(chains):
            slot = c % ATTN_SLOTS
            pt_buf[slot] = jnp.exp2(st_buf[slot] - col_max[c]).astype(BF16)
            ot = _dot(vt_aug[...], pt_buf[slot])
            outs[sub][h] = ot[:HEAD_DIM] / ot[HEAD_DIM:HEAD_DIM + 1]
            if c + ATTN_SLOTS < len(chains):
                scores(c + ATTN_SLOTS)
        finish(outs)


def _attention(q, k, vt):
    q_tiles = SEQ // ATTN_TQ
    return pl.pallas_call(
        _attn_body,
        grid=(BATCH, ATTN_KV_HEADS, q_tiles),
        in_specs=[
            pl.BlockSpec((ATTN_TQ, GROUP_Q_W), lambda b, j, i: (b * q_tiles + i, j)),
            pl.BlockSpec((SEQ, ATTN_KV_W), lambda b, j, i: (b, 0)),
            pl.BlockSpec((1, HEAD_DIM, SEQ), lambda b, j, i: (b, j, 0)),
        ],
        out_specs=pl.BlockSpec((ATTN_TQ, GROUP_Q_W),
                               lambda b, j, i: (b * q_tiles + i, j)),
        out_shape=jax.ShapeDtypeStruct((TOKENS, ATTN_Q_W), BF16),
        scratch_shapes=[pltpu.VMEM((HEAD_DIM + ATTN_SUM_ROWS, SEQ), BF16),
                        pltpu.VMEM((SEQ, 2 * ATTN_KV_W), BF16),
                        pltpu.VMEM((SUBLANES, ATTN_CHAIN_Q), F32),
                        pltpu.VMEM((ATTN_SLOTS, SEQ, ATTN_CHAIN_Q), F32),
                        pltpu.VMEM((ATTN_SLOTS, SEQ, ATTN_CHAIN_Q), BF16)],
        compiler_params=_params(3),
        name="gqa_attention",
    )(q, k, vt)


def _gmlp_body(gm_ref, vn_ref, ws_ref, bias_ref, o_ref):
    u = jax.nn.gelu(gm_ref[:, :GMLP_W])
    v = _rms(jax.nn.gelu(gm_ref[:, GMLP_W:]), vn_ref[...]).astype(BF16)
    bias = bias_ref[...]
    for c in range(GMLP_TM // GMLP_CHUNK):
        rows = slice(c * GMLP_CHUNK, (c + 1) * GMLP_CHUNK)
        sv = jnp.concatenate(
            [_dot(ws_ref[g], v[rows, g * GMLP_GROUP_W:(g + 1) * GMLP_GROUP_W])
             for g in range(GMLP_GROUPS)], axis=1)
        o_ref[rows, :] = (u[rows] * (sv + bias)).astype(BF16)


def _gmlp(gm, vn, ws, bias):
    return pl.pallas_call(
        _gmlp_body,
        grid=(TOKENS // GMLP_TM,),
        in_specs=[pl.BlockSpec((GMLP_TM, 2 * GMLP_W), lambda i: (i, 0)),
                  _resident(vn.shape), _resident(ws.shape), _resident(bias.shape)],
        out_specs=pl.BlockSpec((GMLP_TM, GMLP_W), lambda i: (i, 0)),
        out_shape=jax.ShapeDtypeStruct((TOKENS, GMLP_W), BF16),
        compiler_params=_params(1),
        name="gmlp",
    )(gm, vn, ws, bias)


def _lru_body(lx_ref, ly_ref, cw_ref, cb_ref, w_ref, b_ref, lam_ref, o_ref,
              xpad, a_f, b_f, a_b, b_b, carry_f, carry_b):
    n_chunks = SEQ // LRU_ROWS
    halo_zeros = jnp.zeros((LRU_HALO, LANES), F32)
    xpad[0:LRU_HALO, :] = halo_zeros
    xpad[LRU_HALO + SEQ:LRU_HALO + SEQ + LRU_HALO, :] = halo_zeros
    xpad[LRU_HALO:LRU_HALO + SEQ, :] = lx_ref[...]

    neg_sp = -jax.nn.softplus(-lam_ref[...])
    cw = cw_ref[...]
    cb = cb_ref[...]
    bias = b_ref[...]
    scan_bufs = ((a_f, b_f), (a_b, b_b))

    def dense(c, carry):
        r0 = pl.multiple_of(c * LRU_ROWS, LRU_ROWS)
        xc = cb
        for j in range(CONV_W):
            off = LRU_HALO + j - CONV_W // 2
            xc = xc + cw[j:j + 1, :] * xpad[pl.ds(r0 + off, LRU_ROWS), :]
        pre = _dot(xc.astype(BF16), w_ref[0]) + bias
        for d in range(N_DIR):
            r = jax.nn.sigmoid(pre[:, (2 * d) * LANES:(2 * d + 1) * LANES])
            i = jax.nn.sigmoid(pre[:, (2 * d + 1) * LANES:(2 * d + 2) * LANES])
            a = jnp.exp(LRU_C * r * neg_sp[d:d + 1, :])
            g2 = 1.0 - a * a
            gain = jnp.where(g2 > 0.0, g2 * lax.rsqrt(g2), 0.0)
            a_buf, b_buf = scan_bufs[d]
            a_buf[pl.ds(r0, LRU_ROWS), :] = a
            b_buf[pl.ds(r0, LRU_ROWS), :] = gain * (i * xc)
        return carry

    lax.fori_loop(0, n_chunks, dense, 0)
    tail = LRU_PAD_SEQ - SEQ
    for a_buf, b_buf in scan_bufs:
        a_buf[SEQ:LRU_PAD_SEQ, :] = jnp.ones((tail, LANES), F32)
        b_buf[SEQ:LRU_PAD_SEQ, :] = jnp.zeros((tail, LANES), F32)

    def seg(t):
        return pl.ds(t, SUBLANES, stride=LRU_SEG)

    def scan_step(k, carry):
        hf, pf, hb, pb = carry
        tf = k
        af = a_f[seg(tf), :]
        hf = af * hf + b_f[seg(tf), :]
        pf = af * pf
        b_f[seg(tf), :] = hf
        a_f[seg(tf), :] = pf
        tb = LRU_SEG - 1 - k
        ab = a_b[seg(tb), :]
        hb = ab * hb + b_b[seg(tb), :]
        pb = ab * pb
        b_b[seg(tb), :] = hb
        a_b[seg(tb), :] = pb
        return hf, pf, hb, pb

    z = jnp.zeros((SUBLANES, LANES), F32)
    o = jnp.ones((SUBLANES, LANES), F32)
    lax.fori_loop(0, LRU_SEG, scan_step, (z, o, z, o), unroll=LRU_SCAN_UNROLL)

    row0 = jnp.zeros((1, LANES), F32)
    carry_f[0:1, :] = row0
    c = row0
    for s in range(1, SUBLANES):
        last = s * LRU_SEG - 1
        c = b_f[last:last + 1, :] + a_f[last:last + 1, :] * c
        carry_f[s:s + 1, :] = c
    carry_b[SUBLANES - 1:SUBLANES, :] = row0
    c = row0
    for s in range(SUBLANES - 2, -1, -1):
        first = (s + 1) * LRU_SEG
        c = b_b[first:first + 1, :] + a_b[first:first + 1, :] * c
        carry_b[s:s + 1, :] = c

    seg_shift = LRU_SEG - LRU_ROWS
    row_in_chunk = lax.broadcasted_iota(jnp.int32, (LRU_ROWS, LANES), 0)

    def emit(c, carry):
        r0 = pl.multiple_of(c * LRU_ROWS, LRU_ROWS)
        rows = pl.ds(r0, LRU_ROWS)
        prev = pl.ds(jnp.maximum(c - 1, 0), 1)
        in_prev = row_in_chunk < c * seg_shift
        cf = jnp.where(in_prev, carry_f[prev, :], carry_f[pl.ds(c, 1), :])
        cb_ = jnp.where(in_prev, carry_b[prev, :], carry_b[pl.ds(c, 1), :])
        h = (b_f[rows, :] + a_f[rows, :] * cf) + (b_b[rows, :] + a_b[rows, :] * cb_)
        o_ref[rows, :] = (h * jax.nn.gelu(ly_ref[rows, :])).astype(BF16)
        return carry

    lax.fori_loop(0, n_chunks, emit, 0)


def _lru(lr, cw, cb, w, b, lam):
    n_blk = LRU_W // LANES
    scan_buf = pltpu.VMEM((LRU_PAD_SEQ, LANES), F32)
    return pl.pallas_call(
        _lru_body,
        grid=(BATCH, n_blk),
        in_specs=[
            pl.BlockSpec((SEQ, LANES), lambda b_, c: (b_, c)),
            pl.BlockSpec((SEQ, LANES), lambda b_, c: (b_, n_blk + c)),
            pl.BlockSpec((CONV_W, LANES), lambda b_, c: (0, c)),
            pl.BlockSpec((1, LANES), lambda b_, c: (0, c)),
            pl.BlockSpec((1, LANES, 2 * N_DIR * LANES), lambda b_, c: (c, 0, 0)),
            pl.BlockSpec((1, 2 * N_DIR * LANES), lambda b_, c: (0, c)),
            pl.BlockSpec((N_DIR, LANES), lambda b_, c: (0, c)),
        ],
        out_specs=pl.BlockSpec((SEQ, LANES), lambda b_, c: (b_, c)),
        out_shape=jax.ShapeDtypeStruct((TOKENS, LRU_W), BF16),
        scratch_shapes=[
            pltpu.VMEM((SEQ + 2 * LRU_HALO, LANES), F32),
            scan_buf, scan_buf, scan_buf, scan_buf,
            pltpu.VMEM((SUBLANES, LANES), F32),
            pltpu.VMEM((SUBLANES, LANES), F32),
        ],
        compiler_params=_params(2),
        name="rg_lru",
    )(lr, lr, cw, cb, w, b, lam)


def _xkv_body(mem_ref, g_ref, wkv_ref, kt_ref, v_ref):
    mn = _rms(mem_ref[0], g_ref[...]).astype(BF16)
    kv = _dot(mn, wkv_ref[...])
    kt_ref[0] = (kv[:, :D_MODEL] * (XATTN_HEAD_DIM ** -0.5)).T.astype(BF16)
    v_ref[0] = kv[:, D_MODEL:].astype(BF16)


def _xkv(mem, g, wkv):
    return pl.pallas_call(
        _xkv_body,
        grid=(BATCH,),
        in_specs=[pl.BlockSpec((1, MEM_LEN, D_MODEL), lambda b: (b, 0, 0)),
                  _resident((1, D_MODEL)), _resident(wkv.shape)],
        out_specs=[pl.BlockSpec((1, D_MODEL, MEM_LEN), lambda b: (b, 0, 0)),
                   pl.BlockSpec((1, MEM_LEN, D_MODEL), lambda b: (b, 0, 0))],
        out_shape=[jax.ShapeDtypeStruct((BATCH, D_MODEL, MEM_LEN), BF16),
                   jax.ShapeDtypeStruct((BATCH, MEM_LEN, D_MODEL), BF16)],
        compiler_params=_params(1),
        name="xattn_kv",
    )(mem, g, wkv)


def _merge_xattn_body(x_ref, a_ref, g_ref, l_ref, mixg_ref, wmix_ref, bgate_ref,
                      wa_ref, wg_ref, wl_ref, wo_ref,
                      xg_ref, wq_ref, kt_ref, v_ref, xwo_ref, o_ref):
    x = x_ref[...]
    hn = _rms(x, mixg_ref[...]).astype(BF16)
    merged = jnp.zeros((MERGE_TM, D_MODEL), F32)
    for k, (br, w) in enumerate(((a_ref, wa_ref), (g_ref, wg_ref), (l_ref, wl_ref))):
        lo = MIX_GATE_LO + k * D_MODEL
        gate = jax.nn.sigmoid(_dot(hn, wmix_ref[:, lo:lo + D_MODEL])
                              + bgate_ref[:, k * D_MODEL:(k + 1) * D_MODEL])
        merged = merged + gate * _dot(br[...], w[...])
    x = x + _dot(merged.astype(BF16), wo_ref[...])

    q = _dot(_rms(x, xg_ref[...]).astype(BF16), wq_ref[...]).astype(BF16)
    heads = []
    for h in range(XATTN_HEADS):
        cols = slice(h * XATTN_HEAD_DIM, (h + 1) * XATTN_HEAD_DIM)
        s = _dot(q[:, cols], kt_ref[0, cols, :])
        p = jnp.exp(s - jnp.max(s, axis=-1, keepdims=True))
        l = jnp.sum(p, axis=-1, keepdims=True)
        heads.append((_dot(p.astype(BF16), v_ref[0, :, cols]) * (1.0 / l)).astype(BF16))
    o_ref[...] = x + _dot(jnp.concatenate(heads, axis=1), xwo_ref[...])


def _merge_xattn(x, attn_o, gmlp_o, lru_o, mixg, wmix, bgate, wa, wg, wl, wo,
                 xg, wq, kt, v, xwo):
    tiles = SEQ // MERGE_TM
    row = lambda w: pl.BlockSpec((MERGE_TM, w), lambda i: (i, 0))
    return pl.pallas_call(
        _merge_xattn_body,
        grid=(TOKENS // MERGE_TM,),
        in_specs=[row(D_MODEL), row(ATTN_Q_W), row(GMLP_W), row(LRU_W),
                  _resident((1, D_MODEL)), _resident(wmix.shape), _resident(bgate.shape),
                  _resident(wa.shape), _resident(wg.shape), _resident(wl.shape),
                  _resident(wo.shape),
                  _resident((1, D_MODEL)), _resident(wq.shape),
                  pl.BlockSpec((1, D_MODEL, MEM_LEN), lambda i: (i // tiles, 0, 0)),
                  pl.BlockSpec((1, MEM_LEN, D_MODEL), lambda i: (i // tiles, 0, 0)),
                  _resident(xwo.shape)],
        out_specs=row(D_MODEL),
        out_shape=jax.ShapeDtypeStruct((TOKENS, D_MODEL), F32),
        compiler_params=_params(1),
        name="merge_xattn",
    )(x, attn_o, gmlp_o, lru_o, mixg, wmix, bgate, wa, wg, wl, wo, xg, wq, kt, v, xwo)


def _rope_tables():
    rows = SEQ // GRID_W
    row = jnp.repeat(jnp.arange(rows), GRID_W).astype(F32)
    col = jnp.tile(jnp.arange(GRID_W), rows).astype(F32)
    n_freq = HEAD_DIM // 4
    inv_freq = ROPE_THETA ** (-jnp.arange(n_freq, dtype=F32) / n_freq)
    ang_r = row[:, None] * inv_freq[None, :]
    ang_c = col[:, None] * inv_freq[None, :]
    cos = jnp.concatenate([jnp.cos(ang_r)] * 2 + [jnp.cos(ang_c)] * 2, axis=1)
    sin = jnp.concatenate([-jnp.sin(ang_r), jnp.sin(ang_r),
                           -jnp.sin(ang_c), jnp.sin(ang_c)], axis=1)
    rep = LANES // HEAD_DIM
    return jnp.tile(cos, (1, rep)), jnp.tile(sin, (1, rep))


def _block_diag(w):
    n, a, b = w.shape
    eye = jnp.eye(n, dtype=w.dtype)
    return (eye[:, None, :, None] * w[:, :, None, :]).reshape(n * a, n * b)


def kernel(x, mem, ffn1_norm, ffn1_w_in, ffn1_w_out, mix_norm, w_mix_in, b_gate,
           q_norm, k_norm, attn_up, gmlp_v_norm, gmlp_ws, gmlp_bs, gmlp_up,
           lru_conv_w, lru_conv_b, lru_wa, lru_ba, lru_wi, lru_bi, lru_lambda, lru_up,
           w_mix_out, xattn_norm, mem_norm, xattn_wq, xattn_wkv, xattn_wo,
           ffn2_norm, ffn2_w_in, ffn2_w_out, final_norm):
    assert x.shape == (BATCH, SEQ, D_MODEL) and mem.shape == (BATCH, MEM_LEN, D_MODEL)
    cos, sin = _rope_tables()
    fin = final_norm.reshape(1, D_MODEL)
    n_blk = LRU_W // LANES
    heads_per_blk = LANES // LRU_HEAD_W

    h = x.reshape(TOKENS, D_MODEL)
    for l in range(DEPTH):
        row = lambda a: a[l].reshape(1, -1)
        wmi = w_mix_in[l].astype(BF16)
        qkg = jnp.concatenate([jnp.tile(q_norm[l], ATTN_HEADS),
                               jnp.tile(k_norm[l], ATTN_KV_HEADS)]).reshape(1, QK_W)

        h = _ffn(h, row(ffn1_norm), ffn1_w_in[l].astype(BF16),
                 ffn1_w_out[l].astype(BF16), fin, final=False)

        q, k, vt, gm, lr = _mix_in(h, row(mix_norm), wmi, qkg, cos, sin)

        attn_o = _attention(q, k, vt)

        gm_bias = jnp.repeat(gmlp_bs[l].T, GMLP_GROUP_W, axis=1)
        gmlp_o = _gmlp(gm, row(gmlp_v_norm), gmlp_ws[l].astype(BF16), gm_bias)

        def blk(w):
            return jax.vmap(_block_diag)(
                w.reshape(n_blk, heads_per_blk, LRU_HEAD_W, LRU_HEAD_W))
        lru_w = jnp.concatenate(
            [blk(lru_wa[l, 0]), blk(lru_wi[l, 0]), blk(lru_wa[l, 1]), blk(lru_wi[l, 1])],
            axis=2).astype(BF16)
        lru_b = jnp.stack(
            [lru_ba[l, 0].reshape(n_blk, LANES), lru_bi[l, 0].reshape(n_blk, LANES),
             lru_ba[l, 1].reshape(n_blk, LANES), lru_bi[l, 1].reshape(n_blk, LANES)],
            axis=1).reshape(1, -1)
        lru_o = _lru(lr, lru_conv_w[l], row(lru_conv_b), lru_w, lru_b, lru_lambda[l])

        xkt, xv = _xkv(mem, row(mem_norm), xattn_wkv[l].astype(BF16))
        h = _merge_xattn(h, attn_o, gmlp_o, lru_o, row(mix_norm), wmi, row(b_gate),
                         attn_up[l].astype(BF16), gmlp_up[l].astype(BF16),
                         lru_up[l].astype(BF16), w_mix_out[l].astype(BF16),
                         row(xattn_norm), xattn_wq[l].astype(BF16), xkt, xv,
                         xattn_wo[l].astype(BF16))

        h = _ffn(h, row(ffn2_norm), ffn2_w_in[l].astype(BF16),
                 ffn2_w_out[l].astype(BF16), fin, final=(l == DEPTH - 1))
    return h.reshape(BATCH, SEQ, D_MODEL)
```

```python
import functools

import jax
import jax.numpy as jnp
from jax import lax
from jax.experimental import pallas as pl
from jax.experimental.pallas import tpu as pltpu

F32 = jnp.float32
BF16 = jnp.bfloat16

D_MODEL = 1024
BATCH = 4
SEQ = 4096
DEPTH = 2
TOKENS = BATCH * SEQ
MEM_LEN = 256
GRID_W = 64
EPS = 1e-6

ATTN_HEADS = 8
ATTN_KV_HEADS = 2
ATTN_GROUP = ATTN_HEADS // ATTN_KV_HEADS
HEAD_DIM = 64
ATTN_Q_W = ATTN_HEADS * HEAD_DIM
ATTN_KV_W = ATTN_KV_HEADS * HEAD_DIM
QK_W = ATTN_Q_W + ATTN_KV_W
GROUP_Q_W = ATTN_GROUP * HEAD_DIM
ROPE_THETA = 10000.0
LOG2_E = 1.4426950408889634

GMLP_W = 512
GMLP_GROUPS = 4
GMLP_GROUP_W = GMLP_W // GMLP_GROUPS
GMLP_CHUNK = 128

LRU_W = 512
LRU_HEADS = 8
LRU_HEAD_W = LRU_W // LRU_HEADS
CONV_W = 4
LRU_C = 8.0
N_DIR = 2
N_BRANCH = 3

XATTN_HEADS = 4
XATTN_HEAD_DIM = D_MODEL // XATTN_HEADS
D_FF = 2816

MIX_V_LO = ATTN_Q_W + ATTN_KV_W
MIX_GM_LO = MIX_V_LO + ATTN_KV_W
MIX_LR_LO = MIX_GM_LO + 2 * GMLP_W
MIX_GATE_LO = MIX_LR_LO + 2 * LRU_W

LANES = 128
SUBLANES = 8
VMEM_LIMIT_BYTES = 56 * 1024 * 1024

CAST_ROWS = 256
FFN_TM = 512
FFN_CHUNK = 256
MIX_TM = 512
ATTN_TQ = 512
ATTN_CHAIN_Q = 256
ATTN_SLOTS = 4
ATTN_PAIR = 2
MERGE_TM = 512
LRU_ROWS = 512
LRU_SCAN_UNROLL = 4
LRU_SEG = 516
LRU_PAD_SEQ = SUBLANES * LRU_SEG
LRU_HALO = SUBLANES


def _resident(shape):
    nd = len(shape)
    return pl.BlockSpec(shape, lambda *_: (0,) * nd, pipeline_mode=pl.Buffered(1))


def _layer(arr, l, cols=None):
    shape = arr.shape[1:] if cols is None else arr.shape[1:-1] + (cols,)
    nd = len(shape)
    return pl.BlockSpec((None,) + shape, lambda *_: (l,) + (0,) * nd,
                        pipeline_mode=pl.Buffered(1))


def _params(n_grid_axes):
    return pltpu.CompilerParams(
        dimension_semantics=("arbitrary",) * n_grid_axes,
        vmem_limit_bytes=VMEM_LIMIT_BYTES,
    )


def _rms(x, g):
    ms = jnp.mean(x * x, axis=-1, keepdims=True)
    return x * lax.rsqrt(ms + EPS) * g


def _dot(a, b):
    return jnp.dot(a, b, preferred_element_type=F32)


def _cast_body(w_ref, o_ref):
    o_ref[...] = w_ref[...].astype(BF16)


def _to_bf16(w):
    depth, rows, cols = w.shape
    blk = pl.BlockSpec((1, CAST_ROWS, cols), lambda l, i: (l, i, 0))
    return pl.pallas_call(
        _cast_body,
        grid=(depth, rows // CAST_ROWS),
        in_specs=[blk],
        out_specs=blk,
        out_shape=jax.ShapeDtypeStruct(w.shape, BF16),
        compiler_params=_params(2),
        name="cast_bf16",
    )(w)


def _ffn_body(x_ref, g_ref, win_ref, wout_ref, fin_ref, o_ref, *, final):
    x = x_ref[...]
    xn = _rms(x, g_ref[...]).astype(BF16)
    acc = jnp.zeros(x.shape, F32)
    for c in range(D_FF // FFN_CHUNK):
        lo = c * FFN_CHUNK
        a = _dot(xn, win_ref[:, lo:lo + FFN_CHUNK])
        b = _dot(xn, win_ref[:, D_FF + lo:D_FF + lo + FFN_CHUNK])
        h = (a * jax.nn.sigmoid(a) * b).astype(BF16)
        acc = acc + _dot(h, wout_ref[lo:lo + FFN_CHUNK, :])
    y = x + 0.5 * acc
    if final:
        y = _rms(y, fin_ref[...])
    o_ref[...] = y


def _ffn(x, g, w_in, w_out, fin, l, *, final):
    tile = pl.BlockSpec((FFN_TM, D_MODEL), lambda i: (i, 0))
    return pl.pallas_call(
        functools.partial(_ffn_body, final=final),
        grid=(TOKENS // FFN_TM,),
        in_specs=[tile, _layer(g, l), _layer(w_in, l), _layer(w_out, l),
                  _resident((1, D_MODEL))],
        out_specs=tile,
        out_shape=jax.ShapeDtypeStruct((TOKENS, D_MODEL), F32),
        compiler_params=_params(1),
        name="ffn_final" if final else "ffn",
    )(x, g, w_in, w_out, fin)


def _lane_partner(x, d):
    lane = lax.broadcasted_iota(jnp.int32, x.shape, 1)
    up = pltpu.roll(x, LANES - d, 1)
    down = pltpu.roll(x, d, 1)
    return jnp.where((lane & d) == 0, up, down)


def _mix_in_body(x_ref, g_ref, w_ref, qkg_ref, gsum_ref, cos_ref, sin_ref,
                 vn_ref, ws_ref, gbias_ref,
                 q_ref, k_ref, vt_ref, gmlp_ref, lr_ref):
    xn = _rms(x_ref[...], g_ref[...]).astype(BF16)

    def proj(lo, hi):
        return _dot(xn, w_ref[:, lo:hi])

    qk = proj(0, MIX_V_LO)
    sq = qk * qk
    sq_hi = sq.astype(BF16)
    sq_lo = (sq - sq_hi.astype(F32)).astype(BF16)
    ms = _dot(sq_hi, gsum_ref[...]) + _dot(sq_lo, gsum_ref[...])
    qkn = qk * lax.rsqrt(ms + EPS) * qkg_ref[...]
    cos = cos_ref[...]
    sin = sin_ref[...]
    cols = []
    for c in range(QK_W // LANES):
        t = qkn[:, c * LANES:(c + 1) * LANES]
        cols.append(t * cos + _lane_partner(t, HEAD_DIM // 4) * sin)
    q_ref[...] = (jnp.concatenate(cols[:ATTN_Q_W // LANES], axis=1)
                  * (HEAD_DIM ** -0.5 * LOG2_E)).astype(BF16)
    k_ref[...] = cols[-1].astype(BF16)

    vt_ref[0] = proj(MIX_V_LO, MIX_GM_LO).T.astype(BF16)
    lr_ref[...] = proj(MIX_LR_LO, MIX_GATE_LO)

    u = jax.nn.gelu(proj(MIX_GM_LO, MIX_GM_LO + GMLP_W))
    v = _rms(jax.nn.gelu(proj(MIX_GM_LO + GMLP_W, MIX_LR_LO)), vn_ref[...]).astype(BF16)
    gbias = gbias_ref[...]
    for c in range(MIX_TM // GMLP_CHUNK):
        rows = slice(c * GMLP_CHUNK, (c + 1) * GMLP_CHUNK)
        sv = jnp.concatenate(
            [_dot(ws_ref[g], v[rows, g * GMLP_GROUP_W:(g + 1) * GMLP_GROUP_W])
             for g in range(GMLP_GROUPS)], axis=1)
        gmlp_ref[rows, :] = (u[rows] * (sv + gbias)).astype(BF16)


def _mix_in(x, g, w, qkg, gsum, cos, sin, vn, ws, gbias, l):
    seq_tiles = SEQ // MIX_TM
    row = lambda w_: pl.BlockSpec((MIX_TM, w_), lambda i: (i, 0))
    tab = pl.BlockSpec((MIX_TM, LANES), lambda i: (i % seq_tiles, 0))
    return pl.pallas_call(
        _mix_in_body,
        grid=(TOKENS // MIX_TM,),
        in_specs=[row(D_MODEL), _layer(g, l), _layer(w, l, cols=MIX_GATE_LO),
                  _layer(qkg, l), _resident(gsum.shape), tab, tab,
                  _layer(vn, l), _layer(ws, l), _layer(gbias, l)],
        out_specs=[
            row(ATTN_Q_W), row(ATTN_KV_W),
            pl.BlockSpec((1, ATTN_KV_W, MIX_TM),
                         lambda i: (i // seq_tiles, 0, i % seq_tiles)),
            row(GMLP_W), row(2 * LRU_W)],
        out_shape=[
            jax.ShapeDtypeStruct((TOKENS, ATTN_Q_W), BF16),
            jax.ShapeDtypeStruct((TOKENS, ATTN_KV_W), BF16),
            jax.ShapeDtypeStruct((BATCH, ATTN_KV_W, SEQ), BF16),
            jax.ShapeDtypeStruct((TOKENS, GMLP_W), BF16),
            jax.ShapeDtypeStruct((TOKENS, 2 * LRU_W), F32)],
        compiler_params=_params(1),
        name="mix_in",
    )(x, g, w, qkg, gsum, cos, sin, vn, ws, gbias)


ATTN_SUM_ROWS = 16
ATTN_MAX_EXPONENT_SPAN = 64.0


def _attn_body(q_ref, k_ref, vt_ref, o_ref, vt_aug, k_aug, kmax_ref, st_buf, pt_buf):
    own_lanes = (lax.broadcasted_iota(jnp.int32, (1, ATTN_KV_W), 1) // HEAD_DIM
                 == pl.program_id(1))

    @pl.when(pl.program_id(2) == 0)
    def _():
        vt_aug[0:HEAD_DIM, :] = vt_ref[0]
        vt_aug[HEAD_DIM:, :] = jnp.ones((ATTN_SUM_ROWS, SEQ), BF16)
        kb = k_ref[...]
        k_aug[:, :ATTN_KV_W] = kb
        k_aug[:, ATTN_KV_W:] = jnp.ones((SEQ, ATTN_KV_W), BF16)
        kf = kb.astype(F32)
        norm2 = jnp.sum(jnp.where(own_lanes, kf * kf, 0.0), axis=1, keepdims=True)
        kmax = jnp.sqrt(jnp.max(norm2, axis=0, keepdims=True))
        kmax_ref[...] = jnp.broadcast_to(kmax, kmax_ref.shape)

    kv_head = lax.broadcasted_iota(jnp.int32, (ATTN_KV_W, 1), 0) // HEAD_DIM
    own_rows = kv_head == pl.program_id(1)
    qt = q_ref[...].astype(F32).T

    n_sub = ATTN_TQ // ATTN_CHAIN_Q
    chains = [(sub, h) for sub in range(n_sub) for h in range(ATTN_GROUP)]

    def q_head(c):
        sub, h = chains[c]
        return qt[h * HEAD_DIM:(h + 1) * HEAD_DIM,
                  sub * ATTN_CHAIN_Q:(sub + 1) * ATTN_CHAIN_Q]

    def q_weights(c):
        return jnp.where(own_rows,
                         jnp.concatenate([q_head(c)] * ATTN_KV_HEADS, axis=0), 0.0)

    def finish(outs):
        ot_all = jnp.concatenate([jnp.concatenate(o, axis=0) for o in outs], axis=1)
        o_ref[...] = ot_all.T.astype(BF16)

    kmax = kmax_ref[0:1, 0:ATTN_CHAIN_Q]
    bounds = [jnp.sqrt(jnp.sum(q_head(c) * q_head(c), axis=0, keepdims=True)) * kmax
              for c in range(len(chains))]
    worst = bounds[0]
    for b in bounds[1:]:
        worst = jnp.maximum(worst, b)
    shift_is_safe = jnp.max(worst) * 2.0 <= ATTN_MAX_EXPONENT_SPAN

    @pl.when(shift_is_safe)
    def _():
        is_shift_row = lax.broadcasted_iota(jnp.int32, (ATTN_KV_W, 1), 0) == 0
        outs = [[None] * ATTN_GROUP for _ in range(n_sub)]
        for first in range(0, len(chains), ATTN_PAIR):
            pair = range(first, first + ATTN_PAIR)
            for c in pair:
                tail = jnp.where(is_shift_row, -bounds[c], 0.0)
                w = jnp.concatenate([q_weights(c), tail], axis=0).astype(BF16)
                pt_buf[c % ATTN_SLOTS] = jnp.exp2(_dot(k_aug[...], w)).astype(BF16)
            for c in pair:
                sub, h = chains[c]
                ot = _dot(vt_aug[...], pt_buf[c % ATTN_SLOTS])
                outs[sub][h] = ot[:HEAD_DIM] / ot[HEAD_DIM:HEAD_DIM + 1]
        finish(outs)

    @pl.when(jnp.logical_not(shift_is_safe))
    def _():
        k = k_ref[...]
        col_max = {}

        def scores(c):
            st = _dot(k, q_weights(c).astype(BF16))
            col_max[c] = jnp.max(st, axis=0, keepdims=True)
            st_buf[c % ATTN_SLOTS] = st

        for c in range(min(ATTN_SLOTS, len(chains))):
            scores(c)
        outs = [[None] * ATTN_GROUP for _ in range(n_sub)]
        for c, (sub, h) in enumerate(chains):
            slot = c % ATTN_SLOTS
            pt_buf[slot] = jnp.exp2(st_buf[slot] - col_max[c]).astype(BF16)
            ot = _dot(vt_aug[...], pt_buf[slot])
            outs[sub][h] = ot[:HEAD_DIM] / ot[HEAD_DIM:HEAD_DIM + 1]
            if c + ATTN_SLOTS < len(chains):
                scores(c + ATTN_SLOTS)
        finish(outs)


def _attention(q, k, vt):
    q_tiles = SEQ // ATTN_TQ
    return pl.pallas_call(
        _attn_body,
        grid=(BATCH, ATTN_KV_HEADS, q_tiles),
        in_specs=[
            pl.BlockSpec((ATTN_TQ, GROUP_Q_W), lambda b, j, i: (b * q_tiles + i, j)),
            pl.BlockSpec((SEQ, ATTN_KV_W), lambda b, j, i: (b, 0)),
            pl.BlockSpec((1, HEAD_DIM, SEQ), lambda b, j, i: (b, j, 0)),
        ],
        out_specs=pl.BlockSpec((ATTN_TQ, GROUP_Q_W),
                               lambda b, j, i: (b * q_tiles + i, j)),
        out_shape=jax.ShapeDtypeStruct((TOKENS, ATTN_Q_W), BF16),
        scratch_shapes=[pltpu.VMEM((HEAD_DIM + ATTN_SUM_ROWS, SEQ), BF16),
                        pltpu.VMEM((SEQ, 2 * ATTN_KV_W), BF16),
                        pltpu.VMEM((SUBLANES, ATTN_CHAIN_Q), F32),
                        pltpu.VMEM((ATTN_SLOTS, SEQ, ATTN_CHAIN_Q), F32),
                        pltpu.VMEM((ATTN_SLOTS, SEQ, ATTN_CHAIN_Q), BF16)],
        compiler_params=_params(3),
        name="gqa_attention",
    )(q, k, vt)


def _lru_body(lx_ref, ly_ref, cw_ref, cb_ref, w_ref, b_ref, lam_ref, o_ref,
              xpad, a_f, b_f, a_b, b_b, carry_f, carry_b):
    n_chunks = SEQ // LRU_ROWS
    halo_zeros = jnp.zeros((LRU_HALO, LANES), F32)
    xpad[0:LRU_HALO, :] = halo_zeros
    xpad[LRU_HALO + SEQ:LRU_HALO + SEQ + LRU_HALO, :] = halo_zeros
    xpad[LRU_HALO:LRU_HALO + SEQ, :] = lx_ref[...]

    neg_sp = -jax.nn.softplus(-lam_ref[...])
    cw = cw_ref[...]
    cb = cb_ref[...]
    bias = b_ref[...]
    scan_bufs = ((a_f, b_f), (a_b, b_b))

    def dense(c, carry):
        r0 = pl.multiple_of(c * LRU_ROWS, LRU_ROWS)
        xc = cb
        for j in range(CONV_W):
            off = LRU_HALO + j - CONV_W // 2
            xc = xc + cw[j:j + 1, :] * xpad[pl.ds(r0 + off, LRU_ROWS), :]
        pre = _dot(xc.astype(BF16), w_ref[...]) + bias
        for d in range(N_DIR):
            r = jax.nn.sigmoid(pre[:, (2 * d) * LANES:(2 * d + 1) * LANES])
            i = jax.nn.sigmoid(pre[:, (2 * d + 1) * LANES:(2 * d + 2) * LANES])
            a = jnp.exp(LRU_C * r * neg_sp[d:d + 1, :])
            g2 = 1.0 - a * a
            gain = jnp.where(g2 > 0.0, g2 * lax.rsqrt(g2), 0.0)
            a_buf, b_buf = scan_bufs[d]
            a_buf[pl.ds(r0, LRU_ROWS), :] = a
            b_buf[pl.ds(r0, LRU_ROWS), :] = gain * (i * xc)
        return carry

    lax.fori_loop(0, n_chunks, dense, 0)
    tail = LRU_PAD_SEQ - SEQ
    for a_buf, b_buf in scan_bufs:
        a_buf[SEQ:LRU_PAD_SEQ, :] = jnp.ones((tail, LANES), F32)
        b_buf[SEQ:LRU_PAD_SEQ, :] = jnp.zeros((tail, LANES), F32)

    def seg(t):
        return pl.ds(t, SUBLANES, stride=LRU_SEG)

    def scan_step(k, carry):
        hf, pf, hb, pb = carry
        tf = k
        af = a_f[seg(tf), :]
        hf = af * hf + b_f[seg(tf), :]
        pf = af * pf
        b_f[seg(tf), :] = hf
        a_f[seg(tf), :] = pf
        tb = LRU_SEG - 1 - k
        ab = a_b[seg(tb), :]
        hb = ab * hb + b_b[seg(tb), :]
        pb = ab * pb
        b_b[seg(tb), :] = hb
        a_b[seg(tb), :] = pb
        return hf, pf, hb, pb

    z = jnp.zeros((SUBLANES, LANES), F32)
    o = jnp.ones((SUBLANES, LANES), F32)
    lax.fori_loop(0, LRU_SEG, scan_step, (z, o, z, o), unroll=LRU_SCAN_UNROLL)

    row0 = jnp.zeros((1, LANES), F32)
    carry_f[0:1, :] = row0
    c = row0
    for s in range(1, SUBLANES):
        last = s * LRU_SEG - 1
        c = b_f[last:last + 1, :] + a_f[last:last + 1, :] * c
        carry_f[s:s + 1, :] = c
    carry_b[SUBLANES - 1:SUBLANES, :] = row0
    c = row0
    for s in range(SUBLANES - 2, -1, -1):
        first = (s + 1) * LRU_SEG
        c = b_b[first:first + 1, :] + a_b[first:first + 1, :] * c
        carry_b[s:s + 1, :] = c

    seg_shift = LRU_SEG - LRU_ROWS
    row_in_chunk = lax.broadcasted_iota(jnp.int32, (LRU_ROWS, LANES), 0)

    def emit(c, carry):
        r0 = pl.multiple_of(c * LRU_ROWS, LRU_ROWS)
        rows = pl.ds(r0, LRU_ROWS)
        prev = pl.ds(jnp.maximum(c - 1, 0), 1)
        in_prev = row_in_chunk < c * seg_shift
        cf = jnp.where(in_prev, carry_f[prev, :], carry_f[pl.ds(c, 1), :])
        cb_ = jnp.where(in_prev, carry_b[prev, :], carry_b[pl.ds(c, 1), :])
        h = (b_f[rows, :] + a_f[rows, :] * cf) + (b_b[rows, :] + a_b[rows, :] * cb_)
        o_ref[rows, :] = (h * jax.nn.gelu(ly_ref[rows, :])).astype(BF16)
        return carry

    lax.fori_loop(0, n_chunks, emit, 0)


def _lru(lr, cw, cb, w, b, lam, l):
    n_blk = LRU_W // LANES
    scan_buf = pltpu.VMEM((LRU_PAD_SEQ, LANES), F32)
    return pl.pallas_call(
        _lru_body,
        grid=(BATCH, n_blk),
        in_specs=[
            pl.BlockSpec((SEQ, LANES), lambda b_, c: (b_, c)),
            pl.BlockSpec((SEQ, LANES), lambda b_, c: (b_, n_blk + c)),
            pl.BlockSpec((None, CONV_W, LANES), lambda b_, c: (l, 0, c)),
            pl.BlockSpec((None, 1, LANES), lambda b_, c: (l, 0, c)),
            pl.BlockSpec((None, None, LANES, 2 * N_DIR * LANES),
                         lambda b_, c: (l, c, 0, 0)),
            pl.BlockSpec((None, 1, 2 * N_DIR * LANES), lambda b_, c: (l, 0, c)),
            pl.BlockSpec((None, N_DIR, LANES), lambda b_, c: (l, 0, c)),
        ],
        out_specs=pl.BlockSpec((SEQ, LANES), lambda b_, c: (b_, c)),
        out_shape=jax.ShapeDtypeStruct((TOKENS, LRU_W), BF16),
        scratch_shapes=[
            pltpu.VMEM((SEQ + 2 * LRU_HALO, LANES), F32),
            scan_buf, scan_buf, scan_buf, scan_buf,
            pltpu.VMEM((SUBLANES, LANES), F32),
            pltpu.VMEM((SUBLANES, LANES), F32),
        ],
        compiler_params=_params(2),
        name="rg_lru",
    )(lr, lr, cw, cb, w, b, lam)


def _xkv_body(mem_ref, g_ref, wkv_ref, kt_ref, v_ref):
    mn = _rms(mem_ref[0], g_ref[...]).astype(BF16)
    kv = _dot(mn, wkv_ref[...])
    kt_ref[0] = (kv[:, :D_MODEL] * (XATTN_HEAD_DIM ** -0.5)).T.astype(BF16)
    v_ref[0] = kv[:, D_MODEL:].astype(BF16)


def _xkv(mem, g, wkv, l):
    return pl.pallas_call(
        _xkv_body,
        grid=(BATCH,),
        in_specs=[pl.BlockSpec((1, MEM_LEN, D_MODEL), lambda b: (b, 0, 0)),
                  _layer(g, l), _layer(wkv, l)],
        out_specs=[pl.BlockSpec((1, D_MODEL, MEM_LEN), lambda b: (b, 0, 0)),
                   pl.BlockSpec((1, MEM_LEN, D_MODEL), lambda b: (b, 0, 0))],
        out_shape=[jax.ShapeDtypeStruct((BATCH, D_MODEL, MEM_LEN), BF16),
                   jax.ShapeDtypeStruct((BATCH, MEM_LEN, D_MODEL), BF16)],
        compiler_params=_params(1),
        name="xattn_kv",
    )(mem, g, wkv)


def _merge_xattn_body(x_ref, a_ref, g_ref, l_ref, mixg_ref, wmix_ref, bgate_ref,
                      wa_ref, wg_ref, wl_ref, wo_ref,
                      xg_ref, wq_ref, kt_ref, v_ref, xwo_ref, o_ref):
    x = x_ref[...]
    hn = _rms(x, mixg_ref[...]).astype(BF16)
    merged = jnp.zeros((MERGE_TM, D_MODEL), F32)
    for k, (br, w) in enumerate(((a_ref, wa_ref), (g_ref, wg_ref), (l_ref, wl_ref))):
        lo = MIX_GATE_LO + k * D_MODEL
        gate = jax.nn.sigmoid(_dot(hn, wmix_ref[:, lo:lo + D_MODEL])
                              + bgate_ref[:, k * D_MODEL:(k + 1) * D_MODEL])
        merged = merged + gate * _dot(br[...], w[...])
    x = x + _dot(merged.astype(BF16), wo_ref[...])

    q = _dot(_rms(x, xg_ref[...]).astype(BF16), wq_ref[...]).astype(BF16)
    heads = []
    for h in range(XATTN_HEADS):
        cols = slice(h * XATTN_HEAD_DIM, (h + 1) * XATTN_HEAD_DIM)
        s = _dot(q[:, cols], kt_ref[0, cols, :])
        p = jnp.exp(s - jnp.max(s, axis=-1, keepdims=True))
        l = jnp.sum(p, axis=-1, keepdims=True)
        heads.append((_dot(p.astype(BF16), v_ref[0, :, cols]) * (1.0 / l)).astype(BF16))
    o_ref[...] = x + _dot(jnp.concatenate(heads, axis=1), xwo_ref[...])


def _merge_xattn(x, attn_o, gmlp_o, lru_o, mixg, wmix, bgate, wa, wg, wl, wo,
                 xg, wq, kt, v, xwo, l):
    tiles = SEQ // MERGE_TM
    row = lambda w: pl.BlockSpec((MERGE_TM, w), lambda i: (i, 0))
    return pl.pallas_call(
        _merge_xattn_body,
        grid=(TOKENS // MERGE_TM,),
        in_specs=[row(D_MODEL), row(ATTN_Q_W), row(GMLP_W), row(LRU_W),
                  _layer(mixg, l), _layer(wmix, l), _layer(bgate, l),
                  _layer(wa, l), _layer(wg, l), _layer(wl, l), _layer(wo, l),
                  _layer(xg, l), _layer(wq, l),
                  pl.BlockSpec((1, D_MODEL, MEM_LEN), lambda i: (i // tiles, 0, 0)),
                  pl.BlockSpec((1, MEM_LEN, D_MODEL), lambda i: (i // tiles, 0, 0)),
                  _layer(xwo, l)],
        out_specs=row(D_MODEL),
        out_shape=jax.ShapeDtypeStruct((TOKENS, D_MODEL), F32),
        compiler_params=_params(1),
        name="merge_xattn",
    )(x, attn_o, gmlp_o, lru_o, mixg, wmix, bgate, wa, wg, wl, wo, xg, wq, kt, v, xwo)


def _rope_tables():
    rows = SEQ // GRID_W
    row = jnp.repeat(jnp.arange(rows), GRID_W).astype(F32)
    col = jnp.tile(jnp.arange(GRID_W), rows).astype(F32)
    n_freq = HEAD_DIM // 4
    inv_freq = ROPE_THETA ** (-jnp.arange(n_freq, dtype=F32) / n_freq)
    ang_r = row[:, None] * inv_freq[None, :]
    ang_c = col[:, None] * inv_freq[None, :]
    cos = jnp.concatenate([jnp.cos(ang_r)] * 2 + [jnp.cos(ang_c)] * 2, axis=1)
    sin = jnp.concatenate([-jnp.sin(ang_r), jnp.sin(ang_r),
                           -jnp.sin(ang_c), jnp.sin(ang_c)], axis=1)
    rep = LANES // HEAD_DIM
    return jnp.tile(cos, (1, rep)), jnp.tile(sin, (1, rep))


def _block_diag(w):
    n, a, b = w.shape[-3:]
    eye = jnp.eye(n, dtype=w.dtype)
    out = eye[:, None, :, None] * w[..., :, :, None, :]
    return out.reshape(w.shape[:-3] + (n * a, n * b))


def kernel(x, mem, ffn1_norm, ffn1_w_in, ffn1_w_out, mix_norm, w_mix_in, b_gate,
           q_norm, k_norm, attn_up, gmlp_v_norm, gmlp_ws, gmlp_bs, gmlp_up,
           lru_conv_w, lru_conv_b, lru_wa, lru_ba, lru_wi, lru_bi, lru_lambda, lru_up,
           w_mix_out, xattn_norm, mem_norm, xattn_wq, xattn_wkv, xattn_wo,
           ffn2_norm, ffn2_w_in, ffn2_w_out, final_norm):
    assert x.shape == (BATCH, SEQ, D_MODEL) and mem.shape == (BATCH, MEM_LEN, D_MODEL)
    vec = lambda a: a.reshape(DEPTH, 1, -1)
    cos, sin = _rope_tables()
    fin = final_norm.reshape(1, D_MODEL)
    gsum = _block_diag(jnp.full((QK_W // HEAD_DIM, HEAD_DIM, HEAD_DIM),
                                1.0 / HEAD_DIM, BF16))
    qkg = vec(jnp.concatenate([jnp.tile(q_norm, (1, ATTN_HEADS)),
                               jnp.tile(k_norm, (1, ATTN_KV_HEADS))], axis=1))
    gm_bias = jnp.repeat(jnp.swapaxes(gmlp_bs, 1, 2), GMLP_GROUP_W, axis=2)
    gm_ws = _to_bf16(gmlp_ws.reshape(DEPTH, GMLP_W, GMLP_CHUNK)).reshape(gmlp_ws.shape)

    n_blk = LRU_W // LANES
    heads_per_blk = LANES // LRU_HEAD_W

    def blk(w):
        return _block_diag(
            w.reshape(DEPTH, n_blk, heads_per_blk, LRU_HEAD_W, LRU_HEAD_W))
    lru_w = jnp.concatenate(
        [blk(lru_wa[:, 0]), blk(lru_wi[:, 0]), blk(lru_wa[:, 1]), blk(lru_wi[:, 1])],
        axis=-1).astype(BF16)
    lru_b = jnp.stack(
        [b.reshape(DEPTH, n_blk, LANES)
         for b in (lru_ba[:, 0], lru_bi[:, 0], lru_ba[:, 1], lru_bi[:, 1])],
        axis=2).reshape(DEPTH, 1, -1)

    w_ffn1_in, w_ffn1_out = _to_bf16(ffn1_w_in), _to_bf16(ffn1_w_out)
    w_ffn2_in, w_ffn2_out = _to_bf16(ffn2_w_in), _to_bf16(ffn2_w_out)
    w_mix = _to_bf16(w_mix_in)
    w_attn_up, w_gmlp_up, w_lru_up = _to_bf16(attn_up), _to_bf16(gmlp_up), _to_bf16(lru_up)
    w_out = _to_bf16(w_mix_out)
    w_xq, w_xkv, w_xo = _to_bf16(xattn_wq), _to_bf16(xattn_wkv), _to_bf16(xattn_wo)

    h = x.reshape(TOKENS, D_MODEL)
    for l in range(DEPTH):
        h = _ffn(h, vec(ffn1_norm), w_ffn1_in, w_ffn1_out, fin, l, final=False)
        q, k, vt, gmlp_o, lr = _mix_in(
            h, vec(mix_norm), w_mix, qkg, gsum, cos, sin,
            vec(gmlp_v_norm), gm_ws, gm_bias, l)
        attn_o = _attention(q, k, vt)
        lru_o = _lru(lr, lru_conv_w, vec(lru_conv_b), lru_w, lru_b, lru_lambda, l)
        xkt, xv = _xkv(mem, vec(mem_norm), w_xkv, l)
        h = _merge_xattn(h, attn_o, gmlp_o, lru_o, vec(mix_norm), w_mix, vec(b_gate),
                         w_attn_up, w_gmlp_up, w_lru_up, w_out,
                         vec(xattn_norm), w_xq, xkt, xv, w_xo, l)
        h = _ffn(h, vec(ffn2_norm), w_ffn2_in, w_ffn2_out, fin, l,
                 final=(l == DEPTH - 1))
    return h.reshape(BATCH, SEQ, D_MODEL)
```

```python
import functools

import jax
import jax.numpy as jnp
from jax import lax
from jax.experimental import pallas as pl
from jax.experimental.pallas import tpu as pltpu

F32 = jnp.float32
BF16 = jnp.bfloat16

D_MODEL = 1024
BATCH = 4
SEQ = 4096
DEPTH = 2
TOKENS = BATCH * SEQ
MEM_LEN = 256
GRID_W = 64
EPS = 1e-6

ATTN_HEADS = 8
ATTN_KV_HEADS = 2
ATTN_GROUP = ATTN_HEADS // ATTN_KV_HEADS
HEAD_DIM = 64
ATTN_Q_W = ATTN_HEADS * HEAD_DIM
ATTN_KV_W = ATTN_KV_HEADS * HEAD_DIM
QK_W = ATTN_Q_W + ATTN_KV_W
GROUP_Q_W = ATTN_GROUP * HEAD_DIM
ROPE_THETA = 10000.0
LOG2_E = 1.4426950408889634

GMLP_W = 512
GMLP_GROUPS = 4
GMLP_GROUP_W = GMLP_W // GMLP_GROUPS
GMLP_CHUNK = 128

LRU_W = 512
LRU_HEADS = 8
LRU_HEAD_W = LRU_W // LRU_HEADS
CONV_W = 4
LRU_C = 8.0
N_DIR = 2
N_BRANCH = 3

XATTN_HEADS = 4
XATTN_HEAD_DIM = D_MODEL // XATTN_HEADS
D_FF = 2816

MIX_V_LO = ATTN_Q_W + ATTN_KV_W
MIX_GM_LO = MIX_V_LO + ATTN_KV_W
MIX_LR_LO = MIX_GM_LO + 2 * GMLP_W
MIX_GATE_LO = MIX_LR_LO + 2 * LRU_W

LANES = 128
SUBLANES = 8
VMEM_LIMIT_BYTES = 56 * 1024 * 1024

CAST_ROWS = 256
FFN_TM = 1024
FFN_CHUNK = 256
MIX_TM = 512
ATTN_TQ = 512
ATTN_CHAIN_Q = 256
ATTN_SLOTS = 4
ATTN_PAIR = 2
MERGE_TM = 512
LRU_ROWS = 512
LRU_SCAN_UNROLL = 4
LRU_SEG = 516
LRU_PAD_SEQ = SUBLANES * LRU_SEG
LRU_HALO = SUBLANES


def _resident(shape):
    nd = len(shape)
    return pl.BlockSpec(shape, lambda *_: (0,) * nd, pipeline_mode=pl.Buffered(1))


def _layer(arr, l, cols=None):
    shape = arr.shape[1:] if cols is None else arr.shape[1:-1] + (cols,)
    nd = len(shape)
    return pl.BlockSpec((None,) + shape, lambda *_: (l,) + (0,) * nd,
                        pipeline_mode=pl.Buffered(1))


def _params(n_grid_axes):
    return pltpu.CompilerParams(
        dimension_semantics=("arbitrary",) * n_grid_axes,
        vmem_limit_bytes=VMEM_LIMIT_BYTES,
    )


def _rms(x, g):
    ms = jnp.mean(x * x, axis=-1, keepdims=True)
    return x * lax.rsqrt(ms + EPS) * g


def _dot(a, b):
    return jnp.dot(a, b, preferred_element_type=F32)


def _cast_body(w_ref, o_ref):
    o_ref[...] = w_ref[...].astype(BF16)


def _to_bf16(w):
    depth, rows, cols = w.shape
    blk = pl.BlockSpec((1, CAST_ROWS, cols), lambda l, i: (l, i, 0))
    return pl.pallas_call(
        _cast_body,
        grid=(depth, rows // CAST_ROWS),
        in_specs=[blk],
        out_specs=blk,
        out_shape=jax.ShapeDtypeStruct(w.shape, BF16),
        compiler_params=_params(2),
        name="cast_bf16",
    )(w)


def _ffn_body(x_ref, g_ref, win_ref, wout_ref, fin_ref, o_ref, *, final):
    x = x_ref[...]
    xn = _rms(x, g_ref[...]).astype(BF16)
    acc = jnp.zeros(x.shape, F32)
    for c in range(D_FF // FFN_CHUNK):
        lo = c * FFN_CHUNK
        a = _dot(xn, win_ref[:, lo:lo + FFN_CHUNK])
        b = _dot(xn, win_ref[:, D_FF + lo:D_FF + lo + FFN_CHUNK])
        h = (a * jax.nn.sigmoid(a) * b).astype(BF16)
        acc = acc + _dot(h, wout_ref[lo:lo + FFN_CHUNK, :])
    y = x + 0.5 * acc
    if final:
        y = _rms(y, fin_ref[...])
    o_ref[...] = y


def _ffn(x, g, w_in, w_out, fin, l, *, final):
    tile = pl.BlockSpec((FFN_TM, D_MODEL), lambda i: (i, 0))
    return pl.pallas_call(
        functools.partial(_ffn_body, final=final),
        grid=(TOKENS // FFN_TM,),
        in_specs=[tile, _layer(g, l), _layer(w_in, l), _layer(w_out, l),
                  _resident((1, D_MODEL))],
        out_specs=tile,
        out_shape=jax.ShapeDtypeStruct((TOKENS, D_MODEL), F32),
        compiler_params=_params(1),
        name="ffn_final" if final else "ffn",
    )(x, g, w_in, w_out, fin)


def _lane_partner(x, d):
    lane = lax.broadcasted_iota(jnp.int32, x.shape, 1)
    up = pltpu.roll(x, LANES - d, 1)
    down = pltpu.roll(x, d, 1)
    return jnp.where((lane & d) == 0, up, down)


def _mix_in_body(x_ref, g_ref, w_ref, qkg_ref, gsum_ref, cos_ref, sin_ref,
                 vn_ref, ws_ref, gbias_ref,
                 qt_ref, k_ref, vt_ref, gmlp_ref, lr_ref):
    xn = _rms(x_ref[...], g_ref[...]).astype(BF16)

    def proj(lo, hi):
        return _dot(xn, w_ref[:, lo:hi])

    qk = proj(0, MIX_V_LO)
    sq = qk * qk
    sq_hi = sq.astype(BF16)
    sq_lo = (sq - sq_hi.astype(F32)).astype(BF16)
    ms = _dot(sq_hi, gsum_ref[...]) + _dot(sq_lo, gsum_ref[...])

    u = jax.nn.gelu(proj(MIX_GM_LO, MIX_GM_LO + GMLP_W))
    v = _rms(jax.nn.gelu(proj(MIX_GM_LO + GMLP_W, MIX_LR_LO)), vn_ref[...]).astype(BF16)
    vt_ref[0] = proj(MIX_V_LO, MIX_GM_LO).T.astype(BF16)
    lr_ref[...] = proj(MIX_LR_LO, MIX_GATE_LO)

    gbias = gbias_ref[...]
    for c in range(MIX_TM // GMLP_CHUNK):
        rows = slice(c * GMLP_CHUNK, (c + 1) * GMLP_CHUNK)
        sv = jnp.concatenate(
            [_dot(ws_ref[g], v[rows, g * GMLP_GROUP_W:(g + 1) * GMLP_GROUP_W])
             for g in range(GMLP_GROUPS)], axis=1)
        gmlp_ref[rows, :] = (u[rows] * (sv + gbias)).astype(BF16)

    qkn = qk * lax.rsqrt(ms + EPS) * qkg_ref[...]
    cos = cos_ref[...]
    sin = sin_ref[...]
    cols = []
    for c in range(QK_W // LANES):
        t = qkn[:, c * LANES:(c + 1) * LANES]
        cols.append(t * cos + _lane_partner(t, HEAD_DIM // 4) * sin)
    q = jnp.concatenate(cols[:ATTN_Q_W // LANES], axis=1) * (HEAD_DIM ** -0.5 * LOG2_E)
    qt_ref[0] = q.T.astype(BF16)
    k_ref[...] = cols[-1].astype(BF16)


def _mix_in(x, g, w, qkg, gsum, cos, sin, vn, ws, gbias, l):
    seq_tiles = SEQ // MIX_TM
    row = lambda w_: pl.BlockSpec((MIX_TM, w_), lambda i: (i, 0))
    tab = pl.BlockSpec((MIX_TM, LANES), lambda i: (i % seq_tiles, 0))
    return pl.pallas_call(
        _mix_in_body,
        grid=(TOKENS // MIX_TM,),
        in_specs=[row(D_MODEL), _layer(g, l), _layer(w, l, cols=MIX_GATE_LO),
                  _layer(qkg, l), _resident(gsum.shape), tab, tab,
                  _layer(vn, l), _layer(ws, l), _layer(gbias, l)],
        out_specs=[
            pl.BlockSpec((1, ATTN_Q_W, MIX_TM),
                         lambda i: (i // seq_tiles, 0, i % seq_tiles)),
            row(ATTN_KV_W),
            pl.BlockSpec((1, ATTN_KV_W, MIX_TM),
                         lambda i: (i // seq_tiles, 0, i % seq_tiles)),
            row(GMLP_W), row(2 * LRU_W)],
        out_shape=[
            jax.ShapeDtypeStruct((BATCH, ATTN_Q_W, SEQ), BF16),
            jax.ShapeDtypeStruct((TOKENS, ATTN_KV_W), BF16),
            jax.ShapeDtypeStruct((BATCH, ATTN_KV_W, SEQ), BF16),
            jax.ShapeDtypeStruct((TOKENS, GMLP_W), BF16),
            jax.ShapeDtypeStruct((TOKENS, 2 * LRU_W), F32)],
        compiler_params=_params(1),
        name="mix_in",
    )(x, g, w, qkg, gsum, cos, sin, vn, ws, gbias)


ATTN_SUM_ROWS = 16
ATTN_MAX_EXPONENT_SPAN = 64.0
ATTN_ROUNDING_SLACK = 1.0 + 2.0 ** -8


def _attn_body(qg_ref, qt_ref, k_ref, vt_ref, o_ref,
               vt_aug, k_aug, shift_ref, st_buf, pt_buf):
    own_lanes = (lax.broadcasted_iota(jnp.int32, (1, ATTN_KV_W), 1) // HEAD_DIM
                 == pl.program_id(1))

    @pl.when(pl.program_id(2) == 0)
    def _():
        vt_aug[0:HEAD_DIM, :] = vt_ref[0]
        vt_aug[HEAD_DIM:, :] = jnp.ones((ATTN_SUM_ROWS, SEQ), BF16)
        kb = k_ref[...]
        k_aug[:, :ATTN_KV_W] = kb
        k_aug[:, ATTN_KV_W:] = jnp.ones((SEQ, ATTN_KV_W), BF16)
        kf = kb.astype(F32)
        norm2 = jnp.sum(jnp.where(own_lanes, kf * kf, 0.0), axis=1, keepdims=True)
        kmax = jnp.sqrt(jnp.max(norm2, axis=0, keepdims=True))
        qmax = jnp.max(jnp.abs(qg_ref[:, :ATTN_Q_W]), axis=1, keepdims=True)
        shift_ref[0] = jnp.max(qmax * kmax) * (LOG2_E * ATTN_ROUNDING_SLACK)

    kv_head = lax.broadcasted_iota(jnp.int32, (ATTN_KV_W, 1), 0) // HEAD_DIM
    own_rows = kv_head == pl.program_id(1)
    qt = qt_ref[0]

    n_sub = ATTN_TQ // ATTN_CHAIN_Q
    chains = [(sub, h) for sub in range(n_sub) for h in range(ATTN_GROUP)]

    def q_weights(c):
        sub, h = chains[c]
        qh = qt[h * HEAD_DIM:(h + 1) * HEAD_DIM,
                sub * ATTN_CHAIN_Q:(sub + 1) * ATTN_CHAIN_Q]
        stacked = jnp.concatenate([qh] * ATTN_KV_HEADS, axis=0)
        return jnp.where(own_rows, stacked, jnp.zeros_like(stacked))

    def finish(outs):
        ot_all = jnp.concatenate([jnp.concatenate(o, axis=0) for o in outs], axis=1)
        o_ref[...] = ot_all.T.astype(BF16)

    shift = shift_ref[0]
    shift_is_safe = shift * 2.0 <= ATTN_MAX_EXPONENT_SPAN

    @pl.when(shift_is_safe)
    def _():
        shift_row = lax.broadcasted_iota(jnp.int32, (ATTN_KV_W, ATTN_CHAIN_Q), 0) == 0
        tail = jnp.where(shift_row, -shift, 0.0).astype(BF16)
        outs = [[None] * ATTN_GROUP for _ in range(n_sub)]
        for first in range(0, len(chains), ATTN_PAIR):
            pair = range(first, first + ATTN_PAIR)
            for c in pair:
                w = jnp.concatenate([q_weights(c), tail], axis=0)
                pt_buf[c % ATTN_SLOTS] = jnp.exp2(_dot(k_aug[...], w)).astype(BF16)
            for c in pair:
                sub, h = chains[c]
                ot = _dot(vt_aug[...], pt_buf[c % ATTN_SLOTS])
                outs[sub][h] = ot[:HEAD_DIM] / ot[HEAD_DIM:HEAD_DIM + 1]
        finish(outs)

    @pl.when(jnp.logical_not(shift_is_safe))
    def _():
        k = k_ref[...]
        col_max = {}

        def scores(c):
            st = _dot(k, q_weights(c))
            col_max[c] = jnp.max(st, axis=0, keepdims=True)
            st_buf[c % ATTN_SLOTS] = st

        for c in range(min(ATTN_SLOTS, len(chains))):
            scores(c)
        outs = [[None] * ATTN_GROUP for _ in range(n_sub)]
        for c, (sub, h) in enumerate(chains):
            slot = c % ATTN_SLOTS
            pt_buf[slot] = jnp.exp2(st_buf[slot] - col_max[c]).astype(BF16)
            ot = _dot(vt_aug[...], pt_buf[slot])
            outs[sub][h] = ot[:HEAD_DIM] / ot[HEAD_DIM:HEAD_DIM + 1]
            if c + ATTN_SLOTS < len(chains):
                scores(c + ATTN_SLOTS)
        finish(outs)


def _attention(qkg, qt, k, vt, l):
    q_tiles = SEQ // ATTN_TQ
    return pl.pallas_call(
        _attn_body,
        grid=(BATCH, ATTN_KV_HEADS, q_tiles),
        in_specs=[
            _layer(qkg, l),
            pl.BlockSpec((1, GROUP_Q_W, ATTN_TQ), lambda b, j, i: (b, j, i)),
            pl.BlockSpec((SEQ, ATTN_KV_W), lambda b, j, i: (b, 0)),
            pl.BlockSpec((1, HEAD_DIM, SEQ), lambda b, j, i: (b, j, 0)),
        ],
        out_specs=pl.BlockSpec((ATTN_TQ, GROUP_Q_W),
                               lambda b, j, i: (b * q_tiles + i, j)),
        out_shape=jax.ShapeDtypeStruct((TOKENS, ATTN_Q_W), BF16),
        scratch_shapes=[pltpu.VMEM((HEAD_DIM + ATTN_SUM_ROWS, SEQ), BF16),
                        pltpu.VMEM((SEQ, 2 * ATTN_KV_W), BF16),
                        pltpu.SMEM((1,), F32),
                        pltpu.VMEM((ATTN_SLOTS, SEQ, ATTN_CHAIN_Q), F32),
                        pltpu.VMEM((ATTN_SLOTS, SEQ, ATTN_CHAIN_Q), BF16)],
        compiler_params=_params(3),
        name="gqa_attention",
    )(qkg, qt, k, vt)


def _lru_body(lx_ref, ly_ref, cw_ref, cb_ref, w_ref, b_ref, lam_ref, o_ref,
              xpad, a_f, b_f, a_b, b_b, carry_f, carry_b):
    n_chunks = SEQ // LRU_ROWS
    halo_zeros = jnp.zeros((LRU_HALO, LANES), F32)
    xpad[0:LRU_HALO, :] = halo_zeros
    xpad[LRU_HALO + SEQ:LRU_HALO + SEQ + LRU_HALO, :] = halo_zeros
    xpad[LRU_HALO:LRU_HALO + SEQ, :] = lx_ref[...]

    neg_sp = -jax.nn.softplus(-lam_ref[...])
    cw = cw_ref[...]
    cb = cb_ref[...]
    bias = b_ref[...]
    scan_bufs = ((a_f, b_f), (a_b, b_b))

    def dense(c, carry):
        r0 = pl.multiple_of(c * LRU_ROWS, LRU_ROWS)
        xc = cb
        for j in range(CONV_W):
            off = LRU_HALO + j - CONV_W // 2
            xc = xc + cw[j:j + 1, :] * xpad[pl.ds(r0 + off, LRU_ROWS), :]
        pre = _dot(xc.astype(BF16), w_ref[...]) + bias
        for d in range(N_DIR):
            r = jax.nn.sigmoid(pre[:, (2 * d) * LANES:(2 * d + 1) * LANES])
            i = jax.nn.sigmoid(pre[:, (2 * d + 1) * LANES:(2 * d + 2) * LANES])
            a = jnp.exp(LRU_C * r * neg_sp[d:d + 1, :])
            g2 = 1.0 - a * a
            gain = jnp.where(g2 > 0.0, g2 * lax.rsqrt(g2), 0.0)
            a_buf, b_buf = scan_bufs[d]
            a_buf[pl.ds(r0, LRU_ROWS), :] = a
            b_buf[pl.ds(r0, LRU_ROWS), :] = gain * (i * xc)
        return carry

    lax.fori_loop(0, n_chunks, dense, 0)
    tail = LRU_PAD_SEQ - SEQ
    for a_buf, b_buf in scan_bufs:
        a_buf[SEQ:LRU_PAD_SEQ, :] = jnp.ones((tail, LANES), F32)
        b_buf[SEQ:LRU_PAD_SEQ, :] = jnp.zeros((tail, LANES), F32)

    def seg(t):
        return pl.ds(t, SUBLANES, stride=LRU_SEG)

    def scan_step(k, carry):
        hf, pf, hb, pb = carry
        tf = k
        af = a_f[seg(tf), :]
        hf = af * hf + b_f[seg(tf), :]
        pf = af * pf
        b_f[seg(tf), :] = hf
        a_f[seg(tf), :] = pf
        tb = LRU_SEG - 1 - k
        ab = a_b[seg(tb), :]
        hb = ab * hb + b_b[seg(tb), :]
        pb = ab * pb
        b_b[seg(tb), :] = hb
        a_b[seg(tb), :] = pb
        return hf, pf, hb, pb

    z = jnp.zeros((SUBLANES, LANES), F32)
    o = jnp.ones((SUBLANES, LANES), F32)
    lax.fori_loop(0, LRU_SEG, scan_step, (z, o, z, o), unroll=LRU_SCAN_UNROLL)

    row0 = jnp.zeros((1, LANES), F32)
    carry_f[0:1, :] = row0
    c = row0
    for s in range(1, SUBLANES):
        last = s * LRU_SEG - 1
        c = b_f[last:last + 1, :] + a_f[last:last + 1, :] * c
        carry_f[s:s + 1, :] = c
    carry_b[SUBLANES - 1:SUBLANES, :] = row0
    c = row0
    for s in range(SUBLANES - 2, -1, -1):
        first = (s + 1) * LRU_SEG
        c = b_b[first:first + 1, :] + a_b[first:first + 1, :] * c
        carry_b[s:s + 1, :] = c

    seg_shift = LRU_SEG - LRU_ROWS
    row_in_chunk = lax.broadcasted_iota(jnp.int32, (LRU_ROWS, LANES), 0)

    def emit(c, carry):
        r0 = pl.multiple_of(c * LRU_ROWS, LRU_ROWS)
        rows = pl.ds(r0, LRU_ROWS)
        prev = pl.ds(jnp.maximum(c - 1, 0), 1)
        in_prev = row_in_chunk < c * seg_shift
        cf = jnp.where(in_prev, carry_f[prev, :], carry_f[pl.ds(c, 1), :])
        cb_ = jnp.where(in_prev, carry_b[prev, :], carry_b[pl.ds(c, 1), :])
        h = (b_f[rows, :] + a_f[rows, :] * cf) + (b_b[rows, :] + a_b[rows, :] * cb_)
        o_ref[rows, :] = (h * jax.nn.gelu(ly_ref[rows, :])).astype(BF16)
        return carry

    lax.fori_loop(0, n_chunks, emit, 0)


def _lru(lr, cw, cb, w, b, lam, l):
    n_blk = LRU_W // LANES
    scan_buf = pltpu.VMEM((LRU_PAD_SEQ, LANES), F32)
    return pl.pallas_call(
        _lru_body,
        grid=(BATCH, n_blk),
        in_specs=[
            pl.BlockSpec((SEQ, LANES), lambda b_, c: (b_, c)),
            pl.BlockSpec((SEQ, LANES), lambda b_, c: (b_, n_blk + c)),
            pl.BlockSpec((None, CONV_W, LANES), lambda b_, c: (l, 0, c)),
            pl.BlockSpec((None, 1, LANES), lambda b_, c: (l, 0, c)),
            pl.BlockSpec((None, None, LANES, 2 * N_DIR * LANES),
                         lambda b_, c: (l, c, 0, 0)),
            pl.BlockSpec((None, 1, 2 * N_DIR * LANES), lambda b_, c: (l, 0, c)),
            pl.BlockSpec((None, N_DIR, LANES), lambda b_, c: (l, 0, c)),
        ],
        out_specs=pl.BlockSpec((SEQ, LANES), lambda b_, c: (b_, c)),
        out_shape=jax.ShapeDtypeStruct((TOKENS, LRU_W), BF16),
        scratch_shapes=[
            pltpu.VMEM((SEQ + 2 * LRU_HALO, LANES), F32),
            scan_buf, scan_buf, scan_buf, scan_buf,
            pltpu.VMEM((SUBLANES, LANES), F32),
            pltpu.VMEM((SUBLANES, LANES), F32),
        ],
        compiler_params=_params(2),
        name="rg_lru",
    )(lr, lr, cw, cb, w, b, lam)


def _xkv_body(mem_ref, g_ref, wkv_ref, kt_ref, v_ref):
    mn = _rms(mem_ref[0], g_ref[...]).astype(BF16)
    kv = _dot(mn, wkv_ref[...])
    kt_ref[0] = (kv[:, :D_MODEL] * (XATTN_HEAD_DIM ** -0.5)).T.astype(BF16)
    v_ref[0] = kv[:, D_MODEL:].astype(BF16)


def _xkv(mem, g, wkv, l):
    return pl.pallas_call(
        _xkv_body,
        grid=(BATCH,),
        in_specs=[pl.BlockSpec((1, MEM_LEN, D_MODEL), lambda b: (b, 0, 0)),
                  _layer(g, l), _layer(wkv, l)],
        out_specs=[pl.BlockSpec((1, D_MODEL, MEM_LEN), lambda b: (b, 0, 0)),
                   pl.BlockSpec((1, MEM_LEN, D_MODEL), lambda b: (b, 0, 0))],
        out_shape=[jax.ShapeDtypeStruct((BATCH, D_MODEL, MEM_LEN), BF16),
                   jax.ShapeDtypeStruct((BATCH, MEM_LEN, D_MODEL), BF16)],
        compiler_params=_params(1),
        name="xattn_kv",
    )(mem, g, wkv)


def _merge_xattn_body(x_ref, a_ref, g_ref, l_ref, mixg_ref, wmix_ref, bgate_ref,
                      wa_ref, wg_ref, wl_ref, wo_ref,
                      xg_ref, wq_ref, kt_ref, v_ref, xwo_ref, o_ref):
    x = x_ref[...]
    hn = _rms(x, mixg_ref[...]).astype(BF16)
    merged = jnp.zeros((MERGE_TM, D_MODEL), F32)
    for k, (br, w) in enumerate(((a_ref, wa_ref), (g_ref, wg_ref), (l_ref, wl_ref))):
        lo = MIX_GATE_LO + k * D_MODEL
        gate = jax.nn.sigmoid(_dot(hn, wmix_ref[:, lo:lo + D_MODEL])
                              + bgate_ref[:, k * D_MODEL:(k + 1) * D_MODEL])
        merged = merged + gate * _dot(br[...], w[...])
    x = x + _dot(merged.astype(BF16), wo_ref[...])

    q = _dot(_rms(x, xg_ref[...]).astype(BF16), wq_ref[...]).astype(BF16)
    heads = []
    for h in range(XATTN_HEADS):
        cols = slice(h * XATTN_HEAD_DIM, (h + 1) * XATTN_HEAD_DIM)
        s = _dot(q[:, cols], kt_ref[0, cols, :])
        p = jnp.exp(s - jnp.max(s, axis=-1, keepdims=True))
        l = jnp.sum(p, axis=-1, keepdims=True)
        heads.append((_dot(p.astype(BF16), v_ref[0, :, cols]) * (1.0 / l)).astype(BF16))
    o_ref[...] = x + _dot(jnp.concatenate(heads, axis=1), xwo_ref[...])


def _merge_xattn(x, attn_o, gmlp_o, lru_o, mixg, wmix, bgate, wa, wg, wl, wo,
                 xg, wq, kt, v, xwo, l):
    tiles = SEQ // MERGE_TM
    row = lambda w: pl.BlockSpec((MERGE_TM, w), lambda i: (i, 0))
    return pl.pallas_call(
        _merge_xattn_body,
        grid=(TOKENS // MERGE_TM,),
        in_specs=[row(D_MODEL), row(ATTN_Q_W), row(GMLP_W), row(LRU_W),
                  _layer(mixg, l), _layer(wmix, l), _layer(bgate, l),
                  _layer(wa, l), _layer(wg, l), _layer(wl, l), _layer(wo, l),
                  _layer(xg, l), _layer(wq, l),
                  pl.BlockSpec((1, D_MODEL, MEM_LEN), lambda i: (i // tiles, 0, 0)),
                  pl.BlockSpec((1, MEM_LEN, D_MODEL), lambda i: (i // tiles, 0, 0)),
                  _layer(xwo, l)],
        out_specs=row(D_MODEL),
        out_shape=jax.ShapeDtypeStruct((TOKENS, D_MODEL), F32),
        compiler_params=_params(1),
        name="merge_xattn",
    )(x, attn_o, gmlp_o, lru_o, mixg, wmix, bgate, wa, wg, wl, wo, xg, wq, kt, v, xwo)


def _rope_tables():
    rows = SEQ // GRID_W
    row = jnp.repeat(jnp.arange(rows), GRID_W).astype(F32)
    col = jnp.tile(jnp.arange(GRID_W), rows).astype(F32)
    n_freq = HEAD_DIM // 4
    inv_freq = ROPE_THETA ** (-jnp.arange(n_freq, dtype=F32) / n_freq)
    ang_r = row[:, None] * inv_freq[None, :]
    ang_c = col[:, None] * inv_freq[None, :]
    cos = jnp.concatenate([jnp.cos(ang_r)] * 2 + [jnp.cos(ang_c)] * 2, axis=1)
    sin = jnp.concatenate([-jnp.sin(ang_r), jnp.sin(ang_r),
                           -jnp.sin(ang_c), jnp.sin(ang_c)], axis=1)
    rep = LANES // HEAD_DIM
    return jnp.tile(cos, (1, rep)), jnp.tile(sin, (1, rep))


def _block_diag(w):
    n, a, b = w.shape[-3:]
    eye = jnp.eye(n, dtype=w.dtype)
    out = eye[:, None, :, None] * w[..., :, :, None, :]
    return out.reshape(w.shape[:-3] + (n * a, n * b))


def kernel(x, mem, ffn1_norm, ffn1_w_in, ffn1_w_out, mix_norm, w_mix_in, b_gate,
           q_norm, k_norm, attn_up, gmlp_v_norm, gmlp_ws, gmlp_bs, gmlp_up,
           lru_conv_w, lru_conv_b, lru_wa, lru_ba, lru_wi, lru_bi, lru_lambda, lru_up,
           w_mix_out, xattn_norm, mem_norm, xattn_wq, xattn_wkv, xattn_wo,
           ffn2_norm, ffn2_w_in, ffn2_w_out, final_norm):
    assert x.shape == (BATCH, SEQ, D_MODEL) and mem.shape == (BATCH, MEM_LEN, D_MODEL)
    vec = lambda a: a.reshape(DEPTH, 1, -1)
    cos, sin = _rope_tables()
    fin = final_norm.reshape(1, D_MODEL)
    gsum = _block_diag(jnp.full((QK_W // HEAD_DIM, HEAD_DIM, HEAD_DIM),
                                1.0 / HEAD_DIM, BF16))
    qkg = vec(jnp.concatenate([jnp.tile(q_norm, (1, ATTN_HEADS)),
                               jnp.tile(k_norm, (1, ATTN_KV_HEADS))], axis=1))
    gm_bias = jnp.repeat(jnp.swapaxes(gmlp_bs, 1, 2), GMLP_GROUP_W, axis=2)
    gm_ws = _to_bf16(gmlp_ws.reshape(DEPTH, GMLP_W, GMLP_CHUNK)).reshape(gmlp_ws.shape)

    n_blk = LRU_W // LANES
    heads_per_blk = LANES // LRU_HEAD_W

    def blk(w):
        return _block_diag(
            w.reshape(DEPTH, n_blk, heads_per_blk, LRU_HEAD_W, LRU_HEAD_W))
    lru_w = jnp.concatenate(
        [blk(lru_wa[:, 0]), blk(lru_wi[:, 0]), blk(lru_wa[:, 1]), blk(lru_wi[:, 1])],
        axis=-1).astype(BF16)
    lru_b = jnp.stack(
        [b.reshape(DEPTH, n_blk, LANES)
         for b in (lru_ba[:, 0], lru_bi[:, 0], lru_ba[:, 1], lru_bi[:, 1])],
        axis=2).reshape(DEPTH, 1, -1)

    w_ffn1_in, w_ffn1_out = _to_bf16(ffn1_w_in), _to_bf16(ffn1_w_out)
    w_ffn2_in, w_ffn2_out = _to_bf16(ffn2_w_in), _to_bf16(ffn2_w_out)
    w_mix = _to_bf16(w_mix_in)
    w_attn_up, w_gmlp_up, w_lru_up = _to_bf16(attn_up), _to_bf16(gmlp_up), _to_bf16(lru_up)
    w_out = _to_bf16(w_mix_out)
    w_xq, w_xkv, w_xo = _to_bf16(xattn_wq), _to_bf16(xattn_wkv), _to_bf16(xattn_wo)

    h = x.reshape(TOKENS, D_MODEL)
    for l in range(DEPTH):
        h = _ffn(h, vec(ffn1_norm), w_ffn1_in, w_ffn1_out, fin, l, final=False)
        qt, k, vt, gmlp_o, lr = _mix_in(
            h, vec(mix_norm), w_mix, qkg, gsum, cos, sin,
            vec(gmlp_v_norm), gm_ws, gm_bias, l)
        attn_o = _attention(qkg, qt, k, vt, l)
        lru_o = _lru(lr, lru_conv_w, vec(lru_conv_b), lru_w, lru_b, lru_lambda, l)
        xkt, xv = _xkv(mem, vec(mem_norm), w_xkv, l)
        h = _merge_xattn(h, attn_o, gmlp_o, lru_o, vec(mix_norm), w_mix, vec(b_gate),
                         w_attn_up, w_gmlp_up, w_lru_up, w_out,
                         vec(xattn_norm), w_xq, xkt, xv, w_xo, l)
        h = _ffn(h, vec(ffn2_norm), w_ffn2_in, w_ffn2_out, fin, l,
                 final=(l == DEPTH - 1))
    return h.reshape(BATCH, SEQ, D_MODEL)
```

```python
import functools

import jax
import jax.numpy as jnp
from jax import lax
from jax.experimental import pallas as pl
from jax.experimental.pallas import tpu as pltpu

F32 = jnp.float32
BF16 = jnp.bfloat16

D_MODEL = 1024
BATCH = 4
SEQ = 4096
DEPTH = 2
TOKENS = BATCH * SEQ
MEM_LEN = 256
GRID_W = 64
EPS = 1e-6

ATTN_HEADS = 8
ATTN_KV_HEADS = 2
ATTN_GROUP = ATTN_HEADS // ATTN_KV_HEADS
HEAD_DIM = 64
ATTN_Q_W = ATTN_HEADS * HEAD_DIM
ATTN_KV_W = ATTN_KV_HEADS * HEAD_DIM
QK_W = ATTN_Q_W + ATTN_KV_W
GROUP_Q_W = ATTN_GROUP * HEAD_DIM
ROPE_THETA = 10000.0
LOG2_E = 1.4426950408889634

GMLP_W = 512
GMLP_GROUPS = 4
GMLP_GROUP_W = GMLP_W // GMLP_GROUPS
GMLP_CHUNK = 128

LRU_W = 512
LRU_HEADS = 8
LRU_HEAD_W = LRU_W // LRU_HEADS
CONV_W = 4
LRU_C = 8.0
N_DIR = 2
N_BRANCH = 3

XATTN_HEADS = 4
XATTN_HEAD_DIM = D_MODEL // XATTN_HEADS
D_FF = 2816

MIX_V_LO = ATTN_Q_W + ATTN_KV_W
MIX_GM_LO = MIX_V_LO + ATTN_KV_W
MIX_LR_LO = MIX_GM_LO + 2 * GMLP_W
MIX_GATE_LO = MIX_LR_LO + 2 * LRU_W

LANES = 128
SUBLANES = 8
VMEM_LIMIT_BYTES = 56 * 1024 * 1024

CAST_ROWS = 256
FFN_TM = 1024
FFN_CHUNK = 256
MIX_TM = 512
ATTN_TQ = 512
ATTN_CHAIN_Q = 256
ATTN_SLOTS = 4
ATTN_PAIR = 2
MERGE_TM = 512
LRU_BLOCK = 256
LRU_SLABS = LRU_BLOCK // LANES
LRU_ROWS = 512
LRU_SCAN_UNROLL = 4
LRU_SEG = 516
LRU_PAD_SEQ = SUBLANES * LRU_SEG
LRU_HALO = SUBLANES


def _resident(shape):
    nd = len(shape)
    return pl.BlockSpec(shape, lambda *_: (0,) * nd, pipeline_mode=pl.Buffered(1))


def _layer(arr, l, cols=None):
    shape = arr.shape[1:] if cols is None else arr.shape[1:-1] + (cols,)
    nd = len(shape)
    return pl.BlockSpec((None,) + shape, lambda *_: (l,) + (0,) * nd,
                        pipeline_mode=pl.Buffered(1))


def _params(n_grid_axes):
    return pltpu.CompilerParams(
        dimension_semantics=("arbitrary",) * n_grid_axes,
        vmem_limit_bytes=VMEM_LIMIT_BYTES,
    )


def _rms(x, g):
    ms = jnp.mean(x * x, axis=-1, keepdims=True)
    return x * lax.rsqrt(ms + EPS) * g


def _dot(a, b):
    return jnp.dot(a, b, preferred_element_type=F32)


def _cast_body(w_ref, o_ref):
    o_ref[...] = w_ref[...].astype(BF16)


def _to_bf16(w):
    depth, rows, cols = w.shape
    blk = pl.BlockSpec((1, CAST_ROWS, cols), lambda l, i: (l, i, 0))
    return pl.pallas_call(
        _cast_body,
        grid=(depth, rows // CAST_ROWS),
        in_specs=[blk],
        out_specs=blk,
        out_shape=jax.ShapeDtypeStruct(w.shape, BF16),
        compiler_params=_params(2),
        name="cast_bf16",
    )(w)


def _ffn_body(x_ref, g_ref, win_ref, wout_ref, fin_ref, o_ref, *, final):
    x = x_ref[...]
    xn = _rms(x, g_ref[...]).astype(BF16)
    acc = jnp.zeros(x.shape, F32)
    for c in range(D_FF // FFN_CHUNK):
        lo = c * FFN_CHUNK
        a = _dot(xn, win_ref[:, lo:lo + FFN_CHUNK])
        b = _dot(xn, win_ref[:, D_FF + lo:D_FF + lo + FFN_CHUNK])
        h = (a * jax.nn.sigmoid(a) * b).astype(BF16)
        acc = acc + _dot(h, wout_ref[lo:lo + FFN_CHUNK, :])
    y = x + 0.5 * acc
    if final:
        y = _rms(y, fin_ref[...])
    o_ref[...] = y


def _ffn(x, g, w_in, w_out, fin, l, *, final):
    tile = pl.BlockSpec((FFN_TM, D_MODEL), lambda i: (i, 0))
    return pl.pallas_call(
        functools.partial(_ffn_body, final=final),
        grid=(TOKENS // FFN_TM,),
        in_specs=[tile, _layer(g, l), _layer(w_in, l), _layer(w_out, l),
                  _resident((1, D_MODEL))],
        out_specs=tile,
        out_shape=jax.ShapeDtypeStruct((TOKENS, D_MODEL), F32),
        compiler_params=_params(1),
        name="ffn_final" if final else "ffn",
    )(x, g, w_in, w_out, fin)


def _lane_partner(x, d):
    lane = lax.broadcasted_iota(jnp.int32, x.shape, 1)
    up = pltpu.roll(x, LANES - d, 1)
    down = pltpu.roll(x, d, 1)
    return jnp.where((lane & d) == 0, up, down)


def _mix_in_body(x_ref, g_ref, w_ref, qkg_ref, gsum_ref, cos_ref, sin_ref,
                 vn_ref, ws_ref, gbias_ref,
                 qt_ref, k_ref, vt_ref, gmlp_ref, lr_ref):
    xn = _rms(x_ref[...], g_ref[...]).astype(BF16)

    def proj(lo, hi):
        return _dot(xn, w_ref[:, lo:hi])

    qk = proj(0, MIX_V_LO)
    sq = qk * qk
    sq_hi = sq.astype(BF16)
    sq_lo = (sq - sq_hi.astype(F32)).astype(BF16)
    ms = _dot(sq_hi, gsum_ref[...]) + _dot(sq_lo, gsum_ref[...])

    u = jax.nn.gelu(proj(MIX_GM_LO, MIX_GM_LO + GMLP_W))
    v = _rms(jax.nn.gelu(proj(MIX_GM_LO + GMLP_W, MIX_LR_LO)), vn_ref[...]).astype(BF16)
    vt_ref[0] = proj(MIX_V_LO, MIX_GM_LO).T.astype(BF16)
    lr_ref[...] = proj(MIX_LR_LO, MIX_GATE_LO)

    gbias = gbias_ref[...]
    for c in range(MIX_TM // GMLP_CHUNK):
        rows = slice(c * GMLP_CHUNK, (c + 1) * GMLP_CHUNK)
        sv = jnp.concatenate(
            [_dot(ws_ref[g], v[rows, g * GMLP_GROUP_W:(g + 1) * GMLP_GROUP_W])
             for g in range(GMLP_GROUPS)], axis=1)
        gmlp_ref[rows, :] = (u[rows] * (sv + gbias)).astype(BF16)

    qkn = qk * lax.rsqrt(ms + EPS) * qkg_ref[...]
    cos = cos_ref[...]
    sin = sin_ref[...]
    cols = []
    for c in range(QK_W // LANES):
        t = qkn[:, c * LANES:(c + 1) * LANES]
        cols.append(t * cos + _lane_partner(t, HEAD_DIM // 4) * sin)
    q = jnp.concatenate(cols[:ATTN_Q_W // LANES], axis=1) * (HEAD_DIM ** -0.5 * LOG2_E)
    qt_ref[0] = q.T.astype(BF16)
    k_ref[...] = cols[-1].astype(BF16)


def _mix_in(x, g, w, qkg, gsum, cos, sin, vn, ws, gbias, l):
    seq_tiles = SEQ // MIX_TM
    row = lambda w_: pl.BlockSpec((MIX_TM, w_), lambda i: (i, 0))
    tab = pl.BlockSpec((MIX_TM, LANES), lambda i: (i % seq_tiles, 0))
    return pl.pallas_call(
        _mix_in_body,
        grid=(TOKENS // MIX_TM,),
        in_specs=[row(D_MODEL), _layer(g, l), _layer(w, l, cols=MIX_GATE_LO),
                  _layer(qkg, l), _resident(gsum.shape), tab, tab,
                  _layer(vn, l), _layer(ws, l), _layer(gbias, l)],
        out_specs=[
            pl.BlockSpec((1, ATTN_Q_W, MIX_TM),
                         lambda i: (i // seq_tiles, 0, i % seq_tiles)),
            row(ATTN_KV_W),
            pl.BlockSpec((1, ATTN_KV_W, MIX_TM),
                         lambda i: (i // seq_tiles, 0, i % seq_tiles)),
            row(GMLP_W), row(2 * LRU_W)],
        out_shape=[
            jax.ShapeDtypeStruct((BATCH, ATTN_Q_W, SEQ), BF16),
            jax.ShapeDtypeStruct((TOKENS, ATTN_KV_W), BF16),
            jax.ShapeDtypeStruct((BATCH, ATTN_KV_W, SEQ), BF16),
            jax.ShapeDtypeStruct((TOKENS, GMLP_W), BF16),
            jax.ShapeDtypeStruct((TOKENS, 2 * LRU_W), F32)],
        compiler_params=_params(1),
        name="mix_in",
    )(x, g, w, qkg, gsum, cos, sin, vn, ws, gbias)


ATTN_SUM_ROWS = 16
ATTN_MAX_EXPONENT_SPAN = 64.0
ATTN_ROUNDING_SLACK = 1.0 + 2.0 ** -8


def _attn_body(qg_ref, qt_ref, k_ref, vt_ref, o_ref,
               vt_aug, k_aug, shift_ref, st_buf, pt_buf):
    own_lanes = (lax.broadcasted_iota(jnp.int32, (1, ATTN_KV_W), 1) // HEAD_DIM
                 == pl.program_id(1))

    @pl.when(pl.program_id(2) == 0)
    def _():
        vt_aug[0:HEAD_DIM, :] = vt_ref[0]
        vt_aug[HEAD_DIM:, :] = jnp.ones((ATTN_SUM_ROWS, SEQ), BF16)
        kb = k_ref[...]
        k_aug[:, :ATTN_KV_W] = kb
        k_aug[:, ATTN_KV_W:] = jnp.ones((SEQ, ATTN_KV_W), BF16)
        kf = kb.astype(F32)
        norm2 = jnp.sum(jnp.where(own_lanes, kf * kf, 0.0), axis=1, keepdims=True)
        kmax = jnp.sqrt(jnp.max(norm2, axis=0, keepdims=True))
        qmax = jnp.max(jnp.abs(qg_ref[:, :ATTN_Q_W]), axis=1, keepdims=True)
        shift_ref[0] = jnp.max(qmax * kmax) * (LOG2_E * ATTN_ROUNDING_SLACK)

    kv_head = lax.broadcasted_iota(jnp.int32, (ATTN_KV_W, 1), 0) // HEAD_DIM
    own_rows = kv_head == pl.program_id(1)
    qt = qt_ref[0]

    n_sub = ATTN_TQ // ATTN_CHAIN_Q
    chains = [(sub, h) for sub in range(n_sub) for h in range(ATTN_GROUP)]

    def q_weights(c):
        sub, h = chains[c]
        qh = qt[h * HEAD_DIM:(h + 1) * HEAD_DIM,
                sub * ATTN_CHAIN_Q:(sub + 1) * ATTN_CHAIN_Q]
        stacked = jnp.concatenate([qh] * ATTN_KV_HEADS, axis=0)
        return jnp.where(own_rows, stacked, jnp.zeros_like(stacked))

    def finish(outs):
        ot_all = jnp.concatenate([jnp.concatenate(o, axis=0) for o in outs], axis=1)
        o_ref[...] = ot_all.T.astype(BF16)

    shift = shift_ref[0]
    shift_is_safe = shift * 2.0 <= ATTN_MAX_EXPONENT_SPAN

    @pl.when(shift_is_safe)
    def _():
        shift_row = lax.broadcasted_iota(jnp.int32, (ATTN_KV_W, ATTN_CHAIN_Q), 0) == 0
        tail = jnp.where(shift_row, -shift, 0.0).astype(BF16)
        outs = [[None] * ATTN_GROUP for _ in range(n_sub)]
        for first in range(0, len(chains), ATTN_PAIR):
            pair = range(first, first + ATTN_PAIR)
            for c in pair:
                w = jnp.concatenate([q_weights(c), tail], axis=0)
                pt_buf[c % ATTN_SLOTS] = jnp.exp2(_dot(k_aug[...], w)).astype(BF16)
            for c in pair:
                sub, h = chains[c]
                ot = _dot(vt_aug[...], pt_buf[c % ATTN_SLOTS])
                outs[sub][h] = ot[:HEAD_DIM] / ot[HEAD_DIM:HEAD_DIM + 1]
        finish(outs)

    @pl.when(jnp.logical_not(shift_is_safe))
    def _():
        k = k_ref[...]
        col_max = {}

        def scores(c):
            st = _dot(k, q_weights(c))
            col_max[c] = jnp.max(st, axis=0, keepdims=True)
            st_buf[c % ATTN_SLOTS] = st

        for c in range(min(ATTN_SLOTS, len(chains))):
            scores(c)
        outs = [[None] * ATTN_GROUP for _ in range(n_sub)]
        for c, (sub, h) in enumerate(chains):
            slot = c % ATTN_SLOTS
            pt_buf[slot] = jnp.exp2(st_buf[slot] - col_max[c]).astype(BF16)
            ot = _dot(vt_aug[...], pt_buf[slot])
            outs[sub][h] = ot[:HEAD_DIM] / ot[HEAD_DIM:HEAD_DIM + 1]
            if c + ATTN_SLOTS < len(chains):
                scores(c + ATTN_SLOTS)
        finish(outs)


def _attention(qkg, qt, k, vt, l):
    q_tiles = SEQ // ATTN_TQ
    return pl.pallas_call(
        _attn_body,
        grid=(BATCH, ATTN_KV_HEADS, q_tiles),
        in_specs=[
            _layer(qkg, l),
            pl.BlockSpec((1, GROUP_Q_W, ATTN_TQ), lambda b, j, i: (b, j, i)),
            pl.BlockSpec((SEQ, ATTN_KV_W), lambda b, j, i: (b, 0)),
            pl.BlockSpec((1, HEAD_DIM, SEQ), lambda b, j, i: (b, j, 0)),
        ],
        out_specs=pl.BlockSpec((ATTN_TQ, GROUP_Q_W),
                               lambda b, j, i: (b * q_tiles + i, j)),
        out_shape=jax.ShapeDtypeStruct((TOKENS, ATTN_Q_W), BF16),
        scratch_shapes=[pltpu.VMEM((HEAD_DIM + ATTN_SUM_ROWS, SEQ), BF16),
                        pltpu.VMEM((SEQ, 2 * ATTN_KV_W), BF16),
                        pltpu.SMEM((1,), F32),
                        pltpu.VMEM((ATTN_SLOTS, SEQ, ATTN_CHAIN_Q), F32),
                        pltpu.VMEM((ATTN_SLOTS, SEQ, ATTN_CHAIN_Q), BF16)],
        compiler_params=_params(3),
        name="gqa_attention",
    )(qkg, qt, k, vt)


def _sigmoid(x):
    return 0.5 * jnp.tanh(0.5 * x) + 0.5


def _lru_body(lx_ref, ly_ref, cw_ref, cb_ref, w_ref, b_ref, lam_ref, o_ref,
              xpad, a_f, b_f, a_b, b_b, carry_f, carry_b):
    n_chunks = SEQ // LRU_ROWS
    halo_zeros = jnp.zeros((LRU_SLABS, LRU_HALO, LANES), F32)
    xpad[:, 0:LRU_HALO, :] = halo_zeros
    xpad[:, LRU_HALO + SEQ:LRU_HALO + SEQ + LRU_HALO, :] = halo_zeros
    for s in range(LRU_SLABS):
        xpad[s, LRU_HALO:LRU_HALO + SEQ, :] = lx_ref[:, s * LANES:(s + 1) * LANES]

    rate = -(LRU_C * LOG2_E) * jax.nn.softplus(-lam_ref[...])
    cw = cw_ref[...]
    cb = cb_ref[...]
    bias = b_ref[...]
    scan_bufs = ((a_f, b_f), (a_b, b_b))

    def dense(c, carry):
        r0 = pl.multiple_of(c * LRU_ROWS, LRU_ROWS)
        slabs = []
        for s in range(LRU_SLABS):
            lanes = slice(s * LANES, (s + 1) * LANES)
            acc = cb[:, lanes]
            for j in range(CONV_W):
                off = LRU_HALO + j - CONV_W // 2
                acc = acc + cw[j:j + 1, lanes] * xpad[s, pl.ds(r0 + off, LRU_ROWS), :]
            slabs.append(acc)
        xc = jnp.concatenate(slabs, axis=1)
        pre = _dot(xc.astype(BF16), w_ref[...]) + bias
        for d in range(N_DIR):
            r = _sigmoid(pre[:, (2 * d) * LRU_BLOCK:(2 * d + 1) * LRU_BLOCK])
            i = _sigmoid(pre[:, (2 * d + 1) * LRU_BLOCK:(2 * d + 2) * LRU_BLOCK])
            a = jnp.exp2(r * rate[d:d + 1, :])
            g2 = 1.0 - a * a
            gain = jnp.where(g2 > 0.0, g2 * lax.rsqrt(g2), 0.0)
            b = gain * (i * xc)
            a_buf, b_buf = scan_bufs[d]
            for s in range(LRU_SLABS):
                lanes = slice(s * LANES, (s + 1) * LANES)
                a_buf[s, pl.ds(r0, LRU_ROWS), :] = a[:, lanes]
                b_buf[s, pl.ds(r0, LRU_ROWS), :] = b[:, lanes]
        return carry

    lax.fori_loop(0, n_chunks, dense, 0)
    tail = LRU_PAD_SEQ - SEQ
    for a_buf, b_buf in scan_bufs:
        a_buf[:, SEQ:LRU_PAD_SEQ, :] = jnp.ones((LRU_SLABS, tail, LANES), F32)
        b_buf[:, SEQ:LRU_PAD_SEQ, :] = jnp.zeros((LRU_SLABS, tail, LANES), F32)

    def seg(t):
        return pl.ds(t, SUBLANES, stride=LRU_SEG)

    def scan_step(k, carry):
        out = []
        for (a_buf, b_buf), t, chains in ((scan_bufs[0], k, carry[0]),
                                          (scan_bufs[1], LRU_SEG - 1 - k, carry[1])):
            new = []
            for s, (h, p) in enumerate(chains):
                a = a_buf[s, seg(t), :]
                h = a * h + b_buf[s, seg(t), :]
                p = a * p
                b_buf[s, seg(t), :] = h
                a_buf[s, seg(t), :] = p
                new.append((h, p))
            out.append(tuple(new))
        return tuple(out)

    z = jnp.zeros((SUBLANES, LANES), F32)
    o = jnp.ones((SUBLANES, LANES), F32)
    init = tuple(tuple((z, o) for _ in range(LRU_SLABS)) for _ in range(N_DIR))
    lax.fori_loop(0, LRU_SEG, scan_step, init, unroll=LRU_SCAN_UNROLL)

    row0 = jnp.zeros((1, LANES), F32)
    for s in range(LRU_SLABS):
        carry_f[s, 0:1, :] = row0
        c = row0
        for g in range(1, SUBLANES):
            last = g * LRU_SEG - 1
            c = b_f[s, last:last + 1, :] + a_f[s, last:last + 1, :] * c
            carry_f[s, g:g + 1, :] = c
        carry_b[s, SUBLANES - 1:SUBLANES, :] = row0
        c = row0
        for g in range(SUBLANES - 2, -1, -1):
            first = (g + 1) * LRU_SEG
            c = b_b[s, first:first + 1, :] + a_b[s, first:first + 1, :] * c
            carry_b[s, g:g + 1, :] = c

    seg_shift = LRU_SEG - LRU_ROWS
    row_in_chunk = lax.broadcasted_iota(jnp.int32, (LRU_ROWS, LANES), 0)

    def emit(c, carry):
        r0 = pl.multiple_of(c * LRU_ROWS, LRU_ROWS)
        rows = pl.ds(r0, LRU_ROWS)
        prev = pl.ds(jnp.maximum(c - 1, 0), 1)
        in_prev = row_in_chunk < c * seg_shift
        hs = []
        for s in range(LRU_SLABS):
            cf = jnp.where(in_prev, carry_f[s, prev, :], carry_f[s, pl.ds(c, 1), :])
            cb_ = jnp.where(in_prev, carry_b[s, prev, :], carry_b[s, pl.ds(c, 1), :])
            hs.append((b_f[s, rows, :] + a_f[s, rows, :] * cf)
                      + (b_b[s, rows, :] + a_b[s, rows, :] * cb_))
        h = jnp.concatenate(hs, axis=1)
        o_ref[rows, :] = (h * jax.nn.gelu(ly_ref[rows, :])).astype(BF16)
        return carry

    lax.fori_loop(0, n_chunks, emit, 0)


def _lru(lr, cw, cb, w, b, lam, l):
    n_blk = LRU_W // LRU_BLOCK
    scan_buf = pltpu.VMEM((LRU_SLABS, LRU_PAD_SEQ, LANES), F32)
    carry_buf = pltpu.VMEM((LRU_SLABS, SUBLANES, LANES), F32)
    return pl.pallas_call(
        _lru_body,
        grid=(BATCH, n_blk),
        in_specs=[
            pl.BlockSpec((SEQ, LRU_BLOCK), lambda b_, c: (b_, c)),
            pl.BlockSpec((SEQ, LRU_BLOCK), lambda b_, c: (b_, n_blk + c)),
            pl.BlockSpec((None, CONV_W, LRU_BLOCK), lambda b_, c: (l, 0, c)),
            pl.BlockSpec((None, 1, LRU_BLOCK), lambda b_, c: (l, 0, c)),
            pl.BlockSpec((None, None, LRU_BLOCK, 2 * N_DIR * LRU_BLOCK),
                         lambda b_, c: (l, c, 0, 0)),
            pl.BlockSpec((None, 1, 2 * N_DIR * LRU_BLOCK), lambda b_, c: (l, 0, c)),
            pl.BlockSpec((None, N_DIR, LRU_BLOCK), lambda b_, c: (l, 0, c)),
        ],
        out_specs=pl.BlockSpec((SEQ, LRU_BLOCK), lambda b_, c: (b_, c)),
        out_shape=jax.ShapeDtypeStruct((TOKENS, LRU_W), BF16),
        scratch_shapes=[
            pltpu.VMEM((LRU_SLABS, SEQ + 2 * LRU_HALO, LANES), F32),
            scan_buf, scan_buf, scan_buf, scan_buf,
            carry_buf, carry_buf,
        ],
        compiler_params=_params(2),
        name="rg_lru",
    )(lr, lr, cw, cb, w, b, lam)


def _xkv_body(mem_ref, g_ref, wkv_ref, kt_ref, v_ref):
    mn = _rms(mem_ref[0], g_ref[...]).astype(BF16)
    kv = _dot(mn, wkv_ref[...])
    kt_ref[0] = (kv[:, :D_MODEL] * (XATTN_HEAD_DIM ** -0.5)).T.astype(BF16)
    v_ref[0] = kv[:, D_MODEL:].astype(BF16)


def _xkv(mem, g, wkv, l):
    return pl.pallas_call(
        _xkv_body,
        grid=(BATCH,),
        in_specs=[pl.BlockSpec((1, MEM_LEN, D_MODEL), lambda b: (b, 0, 0)),
                  _layer(g, l), _layer(wkv, l)],
        out_specs=[pl.BlockSpec((1, D_MODEL, MEM_LEN), lambda b: (b, 0, 0)),
                   pl.BlockSpec((1, MEM_LEN, D_MODEL), lambda b: (b, 0, 0))],
        out_shape=[jax.ShapeDtypeStruct((BATCH, D_MODEL, MEM_LEN), BF16),
                   jax.ShapeDtypeStruct((BATCH, MEM_LEN, D_MODEL), BF16)],
        compiler_params=_params(1),
        name="xattn_kv",
    )(mem, g, wkv)


def _merge_xattn_body(x_ref, a_ref, g_ref, l_ref, mixg_ref, wmix_ref, bgate_ref,
                      wa_ref, wg_ref, wl_ref, wo_ref,
                      xg_ref, wq_ref, kt_ref, v_ref, xwo_ref, o_ref):
    x = x_ref[...]
    hn = _rms(x, mixg_ref[...]).astype(BF16)
    merged = jnp.zeros((MERGE_TM, D_MODEL), F32)
    for k, (br, w) in enumerate(((a_ref, wa_ref), (g_ref, wg_ref), (l_ref, wl_ref))):
        lo = MIX_GATE_LO + k * D_MODEL
        gate = jax.nn.sigmoid(_dot(hn, wmix_ref[:, lo:lo + D_MODEL])
                              + bgate_ref[:, k * D_MODEL:(k + 1) * D_MODEL])
        merged = merged + gate * _dot(br[...], w[...])
    x = x + _dot(merged.astype(BF16), wo_ref[...])

    q = _dot(_rms(x, xg_ref[...]).astype(BF16), wq_ref[...]).astype(BF16)
    heads = []
    for h in range(XATTN_HEADS):
        cols = slice(h * XATTN_HEAD_DIM, (h + 1) * XATTN_HEAD_DIM)
        s = _dot(q[:, cols], kt_ref[0, cols, :])
        p = jnp.exp(s - jnp.max(s, axis=-1, keepdims=True))
        l = jnp.sum(p, axis=-1, keepdims=True)
        heads.append((_dot(p.astype(BF16), v_ref[0, :, cols]) * (1.0 / l)).astype(BF16))
    o_ref[...] = x + _dot(jnp.concatenate(heads, axis=1), xwo_ref[...])


def _merge_xattn(x, attn_o, gmlp_o, lru_o, mixg, wmix, bgate, wa, wg, wl, wo,
                 xg, wq, kt, v, xwo, l):
    tiles = SEQ // MERGE_TM
    row = lambda w: pl.BlockSpec((MERGE_TM, w), lambda i: (i, 0))
    return pl.pallas_call(
        _merge_xattn_body,
        grid=(TOKENS // MERGE_TM,),
        in_specs=[row(D_MODEL), row(ATTN_Q_W), row(GMLP_W), row(LRU_W),
                  _layer(mixg, l), _layer(wmix, l), _layer(bgate, l),
                  _layer(wa, l), _layer(wg, l), _layer(wl, l), _layer(wo, l),
                  _layer(xg, l), _layer(wq, l),
                  pl.BlockSpec((1, D_MODEL, MEM_LEN), lambda i: (i // tiles, 0, 0)),
                  pl.BlockSpec((1, MEM_LEN, D_MODEL), lambda i: (i // tiles, 0, 0)),
                  _layer(xwo, l)],
        out_specs=row(D_MODEL),
        out_shape=jax.ShapeDtypeStruct((TOKENS, D_MODEL), F32),
        compiler_params=_params(1),
        name="merge_xattn",
    )(x, attn_o, gmlp_o, lru_o, mixg, wmix, bgate, wa, wg, wl, wo, xg, wq, kt, v, xwo)


def _rope_tables():
    rows = SEQ // GRID_W
    row = jnp.repeat(jnp.arange(rows), GRID_W).astype(F32)
    col = jnp.tile(jnp.arange(GRID_W), rows).astype(F32)
    n_freq = HEAD_DIM // 4
    inv_freq = ROPE_THETA ** (-jnp.arange(n_freq, dtype=F32) / n_freq)
    ang_r = row[:, None] * inv_freq[None, :]
    ang_c = col[:, None] * inv_freq[None, :]
    cos = jnp.concatenate([jnp.cos(ang_r)] * 2 + [jnp.cos(ang_c)] * 2, axis=1)
    sin = jnp.concatenate([-jnp.sin(ang_r), jnp.sin(ang_r),
                           -jnp.sin(ang_c), jnp.sin(ang_c)], axis=1)
    rep = LANES // HEAD_DIM
    return jnp.tile(cos, (1, rep)), jnp.tile(sin, (1, rep))


def _block_diag(w):
    n, a, b = w.shape[-3:]
    eye = jnp.eye(n, dtype=w.dtype)
    out = eye[:, None, :, None] * w[..., :, :, None, :]
    return out.reshape(w.shape[:-3] + (n * a, n * b))


def kernel(x, mem, ffn1_norm, ffn1_w_in, ffn1_w_out, mix_norm, w_mix_in, b_gate,
           q_norm, k_norm, attn_up, gmlp_v_norm, gmlp_ws, gmlp_bs, gmlp_up,
           lru_conv_w, lru_conv_b, lru_wa, lru_ba, lru_wi, lru_bi, lru_lambda, lru_up,
           w_mix_out, xattn_norm, mem_norm, xattn_wq, xattn_wkv, xattn_wo,
           ffn2_norm, ffn2_w_in, ffn2_w_out, final_norm):
    assert x.shape == (BATCH, SEQ, D_MODEL) and mem.shape == (BATCH, MEM_LEN, D_MODEL)
    vec = lambda a: a.reshape(DEPTH, 1, -1)
    cos, sin = _rope_tables()
    fin = final_norm.reshape(1, D_MODEL)
    gsum = _block_diag(jnp.full((QK_W // HEAD_DIM, HEAD_DIM, HEAD_DIM),
                                1.0 / HEAD_DIM, BF16))
    qkg = vec(jnp.concatenate([jnp.tile(q_norm, (1, ATTN_HEADS)),
                               jnp.tile(k_norm, (1, ATTN_KV_HEADS))], axis=1))
    gm_bias = jnp.repeat(jnp.swapaxes(gmlp_bs, 1, 2), GMLP_GROUP_W, axis=2)
    gm_ws = _to_bf16(gmlp_ws.reshape(DEPTH, GMLP_W, GMLP_CHUNK)).reshape(gmlp_ws.shape)

    n_blk = LRU_W // LRU_BLOCK
    heads_per_blk = LRU_BLOCK // LRU_HEAD_W

    def blk(w):
        return _block_diag(
            w.reshape(DEPTH, n_blk, heads_per_blk, LRU_HEAD_W, LRU_HEAD_W))
    lru_w = jnp.concatenate(
        [blk(lru_wa[:, 0]), blk(lru_wi[:, 0]), blk(lru_wa[:, 1]), blk(lru_wi[:, 1])],
        axis=-1).astype(BF16)
    lru_b = jnp.stack(
        [b.reshape(DEPTH, n_blk, LRU_BLOCK)
         for b in (lru_ba[:, 0], lru_bi[:, 0], lru_ba[:, 1], lru_bi[:, 1])],
        axis=2).reshape(DEPTH, 1, -1)

    w_ffn1_in, w_ffn1_out = _to_bf16(ffn1_w_in), _to_bf16(ffn1_w_out)
    w_ffn2_in, w_ffn2_out = _to_bf16(ffn2_w_in), _to_bf16(ffn2_w_out)
    w_mix = _to_bf16(w_mix_in)
    w_attn_up, w_gmlp_up, w_lru_up = _to_bf16(attn_up), _to_bf16(gmlp_up), _to_bf16(lru_up)
    w_out = _to_bf16(w_mix_out)
    w_xq, w_xkv, w_xo = _to_bf16(xattn_wq), _to_bf16(xattn_wkv), _to_bf16(xattn_wo)

    h = x.reshape(TOKENS, D_MODEL)
    for l in range(DEPTH):
        h = _ffn(h, vec(ffn1_norm), w_ffn1_in, w_ffn1_out, fin, l, final=False)
        qt, k, vt, gmlp_o, lr = _mix_in(
            h, vec(mix_norm), w_mix, qkg, gsum, cos, sin,
            vec(gmlp_v_norm), gm_ws, gm_bias, l)
        attn_o = _attention(qkg, qt, k, vt, l)
        lru_o = _lru(lr, lru_conv_w, vec(lru_conv_b), lru_w, lru_b, lru_lambda, l)
        xkt, xv = _xkv(mem, vec(mem_norm), w_xkv, l)
        h = _merge_xattn(h, attn_o, gmlp_o, lru_o, vec(mix_norm), w_mix, vec(b_gate),
                         w_attn_up, w_gmlp_up, w_lru_up, w_out,
                         vec(xattn_norm), w_xq, xkt, xv, w_xo, l)
        h = _ffn(h, vec(ffn2_norm), w_ffn2_in, w_ffn2_out, fin, l,
                 final=(l == DEPTH - 1))
    return h.reshape(BATCH, SEQ, D_MODEL)
```

```python
import functools

import jax
import jax.numpy as jnp
from jax import lax
from jax.experimental import pallas as pl
from jax.experimental.pallas import tpu as pltpu

F32 = jnp.float32
BF16 = jnp.bfloat16

D_MODEL = 1024
BATCH = 4
SEQ = 4096
DEPTH = 2
TOKENS = BATCH * SEQ
MEM_LEN = 256
GRID_W = 64
EPS = 1e-6

ATTN_HEADS = 8
ATTN_KV_HEADS = 2
ATTN_GROUP = ATTN_HEADS // ATTN_KV_HEADS
HEAD_DIM = 64
ATTN_Q_W = ATTN_HEADS * HEAD_DIM
ATTN_KV_W = ATTN_KV_HEADS * HEAD_DIM
QK_W = ATTN_Q_W + ATTN_KV_W
GROUP_Q_W = ATTN_GROUP * HEAD_DIM
ROPE_THETA = 10000.0
LOG2_E = 1.4426950408889634

GMLP_W = 512
GMLP_GROUPS = 4
GMLP_GROUP_W = GMLP_W // GMLP_GROUPS
GMLP_CHUNK = 128

LRU_W = 512
LRU_HEADS = 8
LRU_HEAD_W = LRU_W // LRU_HEADS
CONV_W = 4
LRU_C = 8.0
N_DIR = 2
N_BRANCH = 3

XATTN_HEADS = 4
XATTN_HEAD_DIM = D_MODEL // XATTN_HEADS
D_FF = 2816

MIX_V_LO = ATTN_Q_W + ATTN_KV_W
MIX_GM_LO = MIX_V_LO + ATTN_KV_W
MIX_LR_LO = MIX_GM_LO + 2 * GMLP_W
MIX_GATE_LO = MIX_LR_LO + 2 * LRU_W

LANES = 128
SUBLANES = 8
BF16_ROWS = 16
VMEM_LIMIT_BYTES = 56 * 1024 * 1024

CAST_ROWS = 256
FFN_TM = 512
FFN_CHUNK = 256
MIX_TM = 512
ATTN_TQ = 512
ATTN_CHAIN_Q = 256
ATTN_SLOTS = 4
ATTN_PAIR = 2
MERGE_TM = 512
LRU_BLOCK = 256
LRU_SLABS = LRU_BLOCK // LANES
LRU_ROWS = 512
LRU_SCAN_UNROLL = 4
LRU_SEG = 516
LRU_PAD_SEQ = SUBLANES * LRU_SEG
LRU_HALO = SUBLANES


def _resident(shape):
    nd = len(shape)
    return pl.BlockSpec(shape, lambda *_: (0,) * nd, pipeline_mode=pl.Buffered(1))


def _layer(arr, l, cols=None):
    shape = arr.shape[1:] if cols is None else arr.shape[1:-1] + (cols,)
    nd = len(shape)
    return pl.BlockSpec((None,) + shape, lambda *_: (l,) + (0,) * nd,
                        pipeline_mode=pl.Buffered(1))


def _params(n_grid_axes):
    return pltpu.CompilerParams(
        dimension_semantics=("arbitrary",) * n_grid_axes,
        vmem_limit_bytes=VMEM_LIMIT_BYTES,
    )


def _rms(x, g):
    ms = jnp.mean(x * x, axis=-1, keepdims=True)
    return x * lax.rsqrt(ms + EPS) * g


def _dot(a, b):
    return jnp.dot(a, b, preferred_element_type=F32)


def _cast_body(w_ref, o_ref):
    o_ref[...] = w_ref[...].astype(BF16)


def _to_bf16(w, l):
    _, rows, cols = w.shape
    return pl.pallas_call(
        _cast_body,
        grid=(rows // CAST_ROWS,),
        in_specs=[pl.BlockSpec((None, CAST_ROWS, cols), lambda i: (l, i, 0))],
        out_specs=pl.BlockSpec((CAST_ROWS, cols), lambda i: (i, 0)),
        out_shape=jax.ShapeDtypeStruct((rows, cols), BF16),
        compiler_params=_params(1),
        name="cast_bf16",
    )(w)


def _cast_block_rows(rows, steps):
    for n in range(min(steps, rows // BF16_ROWS), 0, -1):
        if rows % n == 0 and (rows // n) % BF16_ROWS == 0:
            return rows // n
    raise ValueError((rows, steps))


def _call_with_casts(body, *, steps, in_specs, out_specs, out_shape, args, casts,
                     name, scratch_shapes=()):
    n_in, n_out, n_cast = len(in_specs), len(out_specs), len(casts)
    c_in, c_out, c_shape = [], [], []
    for w, l in casts:
        _, rows, cols = w.shape
        blk = _cast_block_rows(rows, steps)
        last = rows // blk - 1
        c_in.append(pl.BlockSpec((None, blk, cols),
                                 lambda i, l=l, last=last: (l, jnp.minimum(i, last), 0)))
        c_out.append(pl.BlockSpec((blk, cols),
                                  lambda i, last=last: (jnp.minimum(i, last), 0)))
        c_shape.append(jax.ShapeDtypeStruct((rows, cols), BF16))

    def with_casts(*refs):
        ins, srcs = refs[:n_in], refs[n_in:n_in + n_cast]
        outs = refs[n_in + n_cast:n_in + n_cast + n_out]
        dsts = refs[n_in + n_cast + n_out:n_in + 2 * n_cast + n_out]
        for src, dst in zip(srcs, dsts):
            dst[...] = src[...].astype(BF16)
        body(*ins, *outs, *refs[n_in + 2 * n_cast + n_out:])

    res = pl.pallas_call(
        with_casts,
        grid=(steps,),
        in_specs=list(in_specs) + c_in,
        out_specs=list(out_specs) + c_out,
        out_shape=list(out_shape) + c_shape,
        scratch_shapes=list(scratch_shapes),
        compiler_params=_params(1),
        name=name,
    )(*args, *[w for w, _ in casts])
    return res[:n_out], res[n_out:]


def _ffn_body(x_ref, g_ref, win_ref, wout_ref, fin_ref, o_ref, *, final):
    x = x_ref[...]
    xn = _rms(x, g_ref[...]).astype(BF16)
    acc = jnp.zeros(x.shape, F32)
    for c in range(D_FF // FFN_CHUNK):
        lo = c * FFN_CHUNK
        a = _dot(xn, win_ref[:, lo:lo + FFN_CHUNK])
        b = _dot(xn, win_ref[:, D_FF + lo:D_FF + lo + FFN_CHUNK])
        h = (a * jax.nn.sigmoid(a) * b).astype(BF16)
        acc = acc + _dot(h, wout_ref[lo:lo + FFN_CHUNK, :])
    y = x + 0.5 * acc
    if final:
        y = _rms(y, fin_ref[...])
    o_ref[...] = y


def _ffn(x, g, w_in, w_out, fin, l, casts, *, final):
    tile = pl.BlockSpec((FFN_TM, D_MODEL), lambda i: (i, 0))
    (y,), cast = _call_with_casts(
        functools.partial(_ffn_body, final=final),
        steps=TOKENS // FFN_TM,
        in_specs=[tile, _layer(g, l), _resident(w_in.shape), _resident(w_out.shape),
                  _resident((1, D_MODEL))],
        out_specs=[tile],
        out_shape=[jax.ShapeDtypeStruct((TOKENS, D_MODEL), F32)],
        args=(x, g, w_in, w_out, fin),
        casts=casts,
        name="ffn_final" if final else "ffn",
    )
    return y, cast


def _lane_partner(x, d):
    lane = lax.broadcasted_iota(jnp.int32, x.shape, 1)
    up = pltpu.roll(x, LANES - d, 1)
    down = pltpu.roll(x, d, 1)
    return jnp.where((lane & d) == 0, up, down)


def _mix_in_body(x_ref, g_ref, w_ref, qkg_ref, gsum_ref, cos_ref, sin_ref,
                 vn_ref, ws_ref, gbias_ref,
                 qt_ref, k_ref, vt_ref, gmlp_ref, lr_ref):
    xn = _rms(x_ref[...], g_ref[...]).astype(BF16)

    def proj(lo, hi):
        return _dot(xn, w_ref[:, lo:hi])

    qk = proj(0, MIX_V_LO)
    sq = qk * qk
    sq_hi = sq.astype(BF16)
    sq_lo = (sq - sq_hi.astype(F32)).astype(BF16)
    ms = _dot(sq_hi, gsum_ref[...]) + _dot(sq_lo, gsum_ref[...])

    u = jax.nn.gelu(proj(MIX_GM_LO, MIX_GM_LO + GMLP_W))
    v = _rms(jax.nn.gelu(proj(MIX_GM_LO + GMLP_W, MIX_LR_LO)), vn_ref[...]).astype(BF16)
    vt_ref[0] = proj(MIX_V_LO, MIX_GM_LO).T.astype(BF16)
    lr_ref[...] = proj(MIX_LR_LO, MIX_GATE_LO)

    gbias = gbias_ref[...]
    for c in range(MIX_TM // GMLP_CHUNK):
        rows = slice(c * GMLP_CHUNK, (c + 1) * GMLP_CHUNK)
        sv = jnp.concatenate(
            [_dot(ws_ref[g], v[rows, g * GMLP_GROUP_W:(g + 1) * GMLP_GROUP_W])
             for g in range(GMLP_GROUPS)], axis=1)
        gmlp_ref[rows, :] = (u[rows] * (sv + gbias)).astype(BF16)

    qkn = qk * lax.rsqrt(ms + EPS) * qkg_ref[...]
    cos = cos_ref[...]
    sin = sin_ref[...]
    cols = []
    for c in range(QK_W // LANES):
        t = qkn[:, c * LANES:(c + 1) * LANES]
        cols.append(t * cos + _lane_partner(t, HEAD_DIM // 4) * sin)
    q = jnp.concatenate(cols[:ATTN_Q_W // LANES], axis=1) * (HEAD_DIM ** -0.5 * LOG2_E)
    qt_ref[0] = q.T.astype(BF16)
    k_ref[...] = cols[-1].astype(BF16)


def _mix_in(x, g, w, qkg, gsum, cos, sin, vn, ws, gbias, l, casts):
    seq_tiles = SEQ // MIX_TM
    row = lambda w_: pl.BlockSpec((MIX_TM, w_), lambda i: (i, 0))
    tab = pl.BlockSpec((MIX_TM, LANES), lambda i: (i % seq_tiles, 0))
    return _call_with_casts(
        _mix_in_body,
        steps=TOKENS // MIX_TM,
        in_specs=[row(D_MODEL), _layer(g, l),
                  _resident((D_MODEL, MIX_GATE_LO)),
                  _layer(qkg, l), _resident(gsum.shape), tab, tab,
                  _layer(vn, l), _resident(ws.shape), _layer(gbias, l)],
        out_specs=[
            pl.BlockSpec((1, ATTN_Q_W, MIX_TM),
                         lambda i: (i // seq_tiles, 0, i % seq_tiles)),
            row(ATTN_KV_W),
            pl.BlockSpec((1, ATTN_KV_W, MIX_TM),
                         lambda i: (i // seq_tiles, 0, i % seq_tiles)),
            row(GMLP_W), row(2 * LRU_W)],
        out_shape=[
            jax.ShapeDtypeStruct((BATCH, ATTN_Q_W, SEQ), BF16),
            jax.ShapeDtypeStruct((TOKENS, ATTN_KV_W), BF16),
            jax.ShapeDtypeStruct((BATCH, ATTN_KV_W, SEQ), BF16),
            jax.ShapeDtypeStruct((TOKENS, GMLP_W), BF16),
            jax.ShapeDtypeStruct((TOKENS, 2 * LRU_W), F32)],
        args=(x, g, w, qkg, gsum, cos, sin, vn, ws, gbias),
        casts=casts,
        name="mix_in",
    )


ATTN_SUM_ROWS = 16
ATTN_MAX_EXPONENT_SPAN = 64.0
ATTN_ROUNDING_SLACK = 1.0 + 2.0 ** -8


def _attn_body(qg_ref, qt_ref, k_ref, vt_ref, o_ref,
               vt_aug, k_aug, shift_ref, st_buf, pt_buf):
    own_lanes = (lax.broadcasted_iota(jnp.int32, (1, ATTN_KV_W), 1) // HEAD_DIM
                 == pl.program_id(1))

    @pl.when(pl.program_id(2) == 0)
    def _():
        vt_aug[0:HEAD_DIM, :] = vt_ref[0]
        vt_aug[HEAD_DIM:, :] = jnp.ones((ATTN_SUM_ROWS, SEQ), BF16)
        kb = k_ref[...]
        k_aug[:, :ATTN_KV_W] = kb
        k_aug[:, ATTN_KV_W:] = jnp.ones((SEQ, ATTN_KV_W), BF16)
        kf = kb.astype(F32)
        norm2 = jnp.sum(jnp.where(own_lanes, kf * kf, 0.0), axis=1, keepdims=True)
        kmax = jnp.sqrt(jnp.max(norm2, axis=0, keepdims=True))
        qmax = jnp.max(jnp.abs(qg_ref[:, :ATTN_Q_W]), axis=1, keepdims=True)
        shift_ref[0] = jnp.max(qmax * kmax) * (LOG2_E * ATTN_ROUNDING_SLACK)

    kv_head = lax.broadcasted_iota(jnp.int32, (ATTN_KV_W, 1), 0) // HEAD_DIM
    own_rows = kv_head == pl.program_id(1)
    qt = qt_ref[0]

    n_sub = ATTN_TQ // ATTN_CHAIN_Q
    chains = [(sub, h) for sub in range(n_sub) for h in range(ATTN_GROUP)]

    def q_weights(c):
        sub, h = chains[c]
        qh = qt[h * HEAD_DIM:(h + 1) * HEAD_DIM,
                sub * ATTN_CHAIN_Q:(sub + 1) * ATTN_CHAIN_Q]
        stacked = jnp.concatenate([qh] * ATTN_KV_HEADS, axis=0)
        return jnp.where(own_rows, stacked, jnp.zeros_like(stacked))

    def finish(outs):
        ot_all = jnp.concatenate([jnp.concatenate(o, axis=0) for o in outs], axis=1)
        o_ref[...] = ot_all.T.astype(BF16)

    shift = shift_ref[0]
    shift_is_safe = shift * 2.0 <= ATTN_MAX_EXPONENT_SPAN

    @pl.when(shift_is_safe)
    def _():
        shift_row = lax.broadcasted_iota(jnp.int32, (ATTN_KV_W, ATTN_CHAIN_Q), 0) == 0
        tail = jnp.where(shift_row, -shift, 0.0).astype(BF16)
        outs = [[None] * ATTN_GROUP for _ in range(n_sub)]
        for first in range(0, len(chains), ATTN_PAIR):
            pair = range(first, first + ATTN_PAIR)
            for c in pair:
                w = jnp.concatenate([q_weights(c), tail], axis=0)
                pt_buf[c % ATTN_SLOTS] = jnp.exp2(_dot(k_aug[...], w)).astype(BF16)
            for c in pair:
                sub, h = chains[c]
                ot = _dot(vt_aug[...], pt_buf[c % ATTN_SLOTS])
                outs[sub][h] = ot[:HEAD_DIM] / ot[HEAD_DIM:HEAD_DIM + 1]
        finish(outs)

    @pl.when(jnp.logical_not(shift_is_safe))
    def _():
        k = k_ref[...]
        col_max = {}

        def scores(c):
            st = _dot(k, q_weights(c))
            col_max[c] = jnp.max(st, axis=0, keepdims=True)
            st_buf[c % ATTN_SLOTS] = st

        for c in range(min(ATTN_SLOTS, len(chains))):
            scores(c)
        outs = [[None] * ATTN_GROUP for _ in range(n_sub)]
        for c, (sub, h) in enumerate(chains):
            slot = c % ATTN_SLOTS
            pt_buf[slot] = jnp.exp2(st_buf[slot] - col_max[c]).astype(BF16)
            ot = _dot(vt_aug[...], pt_buf[slot])
            outs[sub][h] = ot[:HEAD_DIM] / ot[HEAD_DIM:HEAD_DIM + 1]
            if c + ATTN_SLOTS < len(chains):
                scores(c + ATTN_SLOTS)
        finish(outs)


def _attention(qkg, qt, k, vt, l):
    q_tiles = SEQ // ATTN_TQ
    return pl.pallas_call(
        _attn_body,
        grid=(BATCH, ATTN_KV_HEADS, q_tiles),
        in_specs=[
            _layer(qkg, l),
            pl.BlockSpec((1, GROUP_Q_W, ATTN_TQ), lambda b, j, i: (b, j, i)),
            pl.BlockSpec((SEQ, ATTN_KV_W), lambda b, j, i: (b, 0)),
            pl.BlockSpec((1, HEAD_DIM, SEQ), lambda b, j, i: (b, j, 0)),
        ],
        out_specs=pl.BlockSpec((ATTN_TQ, GROUP_Q_W),
                               lambda b, j, i: (b * q_tiles + i, j)),
        out_shape=jax.ShapeDtypeStruct((TOKENS, ATTN_Q_W), BF16),
        scratch_shapes=[pltpu.VMEM((HEAD_DIM + ATTN_SUM_ROWS, SEQ), BF16),
                        pltpu.VMEM((SEQ, 2 * ATTN_KV_W), BF16),
                        pltpu.SMEM((1,), F32),
                        pltpu.VMEM((ATTN_SLOTS, SEQ, ATTN_CHAIN_Q), F32),
                        pltpu.VMEM((ATTN_SLOTS, SEQ, ATTN_CHAIN_Q), BF16)],
        compiler_params=_params(3),
        name="gqa_attention",
    )(qkg, qt, k, vt)


def _sigmoid(x):
    return 0.5 * jnp.tanh(0.5 * x) + 0.5


def _lru_body(lx_ref, ly_ref, cw_ref, cb_ref, w_ref, b_ref, lam_ref, o_ref,
              xpad, a_f, b_f, a_b, b_b, carry_f, carry_b):
    n_chunks = SEQ // LRU_ROWS
    halo_zeros = jnp.zeros((LRU_SLABS, LRU_HALO, LANES), F32)
    xpad[:, 0:LRU_HALO, :] = halo_zeros
    xpad[:, LRU_HALO + SEQ:LRU_HALO + SEQ + LRU_HALO, :] = halo_zeros
    for s in range(LRU_SLABS):
        xpad[s, LRU_HALO:LRU_HALO + SEQ, :] = lx_ref[:, s * LANES:(s + 1) * LANES]

    rate = -(LRU_C * LOG2_E) * jax.nn.softplus(-lam_ref[...])
    cw = cw_ref[...]
    cb = cb_ref[...]
    bias = b_ref[...]
    scan_bufs = ((a_f, b_f), (a_b, b_b))

    def dense(c, carry):
        r0 = pl.multiple_of(c * LRU_ROWS, LRU_ROWS)
        slabs = []
        for s in range(LRU_SLABS):
            lanes = slice(s * LANES, (s + 1) * LANES)
            acc = cb[:, lanes]
            for j in range(CONV_W):
                off = LRU_HALO + j - CONV_W // 2
                acc = acc + cw[j:j + 1, lanes] * xpad[s, pl.ds(r0 + off, LRU_ROWS), :]
            slabs.append(acc)
        xc = jnp.concatenate(slabs, axis=1)
        pre = _dot(xc.astype(BF16), w_ref[...]) + bias
        for d in range(N_DIR):
            r = _sigmoid(pre[:, (2 * d) * LRU_BLOCK:(2 * d + 1) * LRU_BLOCK])
            i = _sigmoid(pre[:, (2 * d + 1) * LRU_BLOCK:(2 * d + 2) * LRU_BLOCK])
            a = jnp.exp2(r * rate[d:d + 1, :])
            g2 = 1.0 - a * a
            gain = jnp.where(g2 > 0.0, g2 * lax.rsqrt(g2), 0.0)
            b = gain * (i * xc)
            a_buf, b_buf = scan_bufs[d]
            for s in range(LRU_SLABS):
                lanes = slice(s * LANES, (s + 1) * LANES)
                a_buf[s, pl.ds(r0, LRU_ROWS), :] = a[:, lanes]
                b_buf[s, pl.ds(r0, LRU_ROWS), :] = b[:, lanes]
        return carry

    lax.fori_loop(0, n_chunks, dense, 0)
    tail = LRU_PAD_SEQ - SEQ
    for a_buf, b_buf in scan_bufs:
        a_buf[:, SEQ:LRU_PAD_SEQ, :] = jnp.ones((LRU_SLABS, tail, LANES), F32)
        b_buf[:, SEQ:LRU_PAD_SEQ, :] = jnp.zeros((LRU_SLABS, tail, LANES), F32)

    def seg(t):
        return pl.ds(t, SUBLANES, stride=LRU_SEG)

    def scan_step(k, carry):
        out = []
        for (a_buf, b_buf), t, chains in ((scan_bufs[0], k, carry[0]),
                                          (scan_bufs[1], LRU_SEG - 1 - k, carry[1])):
            new = []
            for s, (h, p) in enumerate(chains):
                a = a_buf[s, seg(t), :]
                h = a * h + b_buf[s, seg(t), :]
                p = a * p
                b_buf[s, seg(t), :] = h
                a_buf[s, seg(t), :] = p
                new.append((h, p))
            out.append(tuple(new))
        return tuple(out)

    z = jnp.zeros((SUBLANES, LANES), F32)
    o = jnp.ones((SUBLANES, LANES), F32)
    init = tuple(tuple((z, o) for _ in range(LRU_SLABS)) for _ in range(N_DIR))
    lax.fori_loop(0, LRU_SEG, scan_step, init, unroll=LRU_SCAN_UNROLL)

    row0 = jnp.zeros((1, LANES), F32)
    for s in range(LRU_SLABS):
        carry_f[s, 0:1, :] = row0
        c = row0
        for g in range(1, SUBLANES):
            last = g * LRU_SEG - 1
            c = b_f[s, last:last + 1, :] + a_f[s, last:last + 1, :] * c
            carry_f[s, g:g + 1, :] = c
        carry_b[s, SUBLANES - 1:SUBLANES, :] = row0
        c = row0
        for g in range(SUBLANES - 2, -1, -1):
            first = (g + 1) * LRU_SEG
            c = b_b[s, first:first + 1, :] + a_b[s, first:first + 1, :] * c
            carry_b[s, g:g + 1, :] = c

    seg_shift = LRU_SEG - LRU_ROWS
    row_in_chunk = lax.broadcasted_iota(jnp.int32, (LRU_ROWS, LANES), 0)

    def emit(c, carry):
        r0 = pl.multiple_of(c * LRU_ROWS, LRU_ROWS)
        rows = pl.ds(r0, LRU_ROWS)
        prev = pl.ds(jnp.maximum(c - 1, 0), 1)
        in_prev = row_in_chunk < c * seg_shift
        hs = []
        for s in range(LRU_SLABS):
            cf = jnp.where(in_prev, carry_f[s, prev, :], carry_f[s, pl.ds(c, 1), :])
            cb_ = jnp.where(in_prev, carry_b[s, prev, :], carry_b[s, pl.ds(c, 1), :])
            hs.append((b_f[s, rows, :] + a_f[s, rows, :] * cf)
                      + (b_b[s, rows, :] + a_b[s, rows, :] * cb_))
        h = jnp.concatenate(hs, axis=1)
        o_ref[rows, :] = (h * jax.nn.gelu(ly_ref[rows, :])).astype(BF16)
        return carry

    lax.fori_loop(0, n_chunks, emit, 0)


def _lru(lr, cw, cb, w, b, lam, l):
    n_blk = LRU_W // LRU_BLOCK
    scan_buf = pltpu.VMEM((LRU_SLABS, LRU_PAD_SEQ, LANES), F32)
    carry_buf = pltpu.VMEM((LRU_SLABS, SUBLANES, LANES), F32)
    return pl.pallas_call(
        _lru_body,
        grid=(BATCH, n_blk),
        in_specs=[
            pl.BlockSpec((SEQ, LRU_BLOCK), lambda b_, c: (b_, c)),
            pl.BlockSpec((SEQ, LRU_BLOCK), lambda b_, c: (b_, n_blk + c)),
            pl.BlockSpec((None, CONV_W, LRU_BLOCK), lambda b_, c: (l, 0, c)),
            pl.BlockSpec((None, 1, LRU_BLOCK), lambda b_, c: (l, 0, c)),
            pl.BlockSpec((None, None, LRU_BLOCK, 2 * N_DIR * LRU_BLOCK),
                         lambda b_, c: (l, c, 0, 0)),
            pl.BlockSpec((None, 1, 2 * N_DIR * LRU_BLOCK), lambda b_, c: (l, 0, c)),
            pl.BlockSpec((None, N_DIR, LRU_BLOCK), lambda b_, c: (l, 0, c)),
        ],
        out_specs=pl.BlockSpec((SEQ, LRU_BLOCK), lambda b_, c: (b_, c)),
        out_shape=jax.ShapeDtypeStruct((TOKENS, LRU_W), BF16),
        scratch_shapes=[
            pltpu.VMEM((LRU_SLABS, SEQ + 2 * LRU_HALO, LANES), F32),
            scan_buf, scan_buf, scan_buf, scan_buf,
            carry_buf, carry_buf,
        ],
        compiler_params=_params(2),
        name="rg_lru",
    )(lr, lr, cw, cb, w, b, lam)


def _xkv_body(mem_ref, g_ref, wkv_ref, kt_ref, v_ref):
    mn = _rms(mem_ref[0], g_ref[...]).astype(BF16)
    kv = _dot(mn, wkv_ref[...])
    kt_ref[0] = (kv[:, :D_MODEL] * (XATTN_HEAD_DIM ** -0.5)).T.astype(BF16)
    v_ref[0] = kv[:, D_MODEL:].astype(BF16)


def _xkv(mem, g, wkv, l):
    return pl.pallas_call(
        _xkv_body,
        grid=(BATCH,),
        in_specs=[pl.BlockSpec((1, MEM_LEN, D_MODEL), lambda b: (b, 0, 0)),
                  _layer(g, l), _resident(wkv.shape)],
        out_specs=[pl.BlockSpec((1, D_MODEL, MEM_LEN), lambda b: (b, 0, 0)),
                   pl.BlockSpec((1, MEM_LEN, D_MODEL), lambda b: (b, 0, 0))],
        out_shape=[jax.ShapeDtypeStruct((BATCH, D_MODEL, MEM_LEN), BF16),
                   jax.ShapeDtypeStruct((BATCH, MEM_LEN, D_MODEL), BF16)],
        compiler_params=_params(1),
        name="xattn_kv",
    )(mem, g, wkv)


def _merge_xattn_body(x_ref, a_ref, g_ref, l_ref, mixg_ref, wmix_ref, bgate_ref,
                      wa_ref, wg_ref, wl_ref, wo_ref,
                      xg_ref, wq_ref, kt_ref, v_ref, xwo_ref, o_ref):
    x = x_ref[...]
    hn = _rms(x, mixg_ref[...]).astype(BF16)
    merged = jnp.zeros((MERGE_TM, D_MODEL), F32)
    for k, (br, w) in enumerate(((a_ref, wa_ref), (g_ref, wg_ref), (l_ref, wl_ref))):
        lo = MIX_GATE_LO + k * D_MODEL
        gate = jax.nn.sigmoid(_dot(hn, wmix_ref[:, lo:lo + D_MODEL])
                              + bgate_ref[:, k * D_MODEL:(k + 1) * D_MODEL])
        merged = merged + gate * _dot(br[...], w[...])
    x = x + _dot(merged.astype(BF16), wo_ref[...])

    q = _dot(_rms(x, xg_ref[...]).astype(BF16), wq_ref[...]).astype(BF16)
    heads = []
    for h in range(XATTN_HEADS):
        cols = slice(h * XATTN_HEAD_DIM, (h + 1) * XATTN_HEAD_DIM)
        s = _dot(q[:, cols], kt_ref[0, cols, :])
        p = jnp.exp(s - jnp.max(s, axis=-1, keepdims=True))
        l = jnp.sum(p, axis=-1, keepdims=True)
        heads.append((_dot(p.astype(BF16), v_ref[0, :, cols]) * (1.0 / l)).astype(BF16))
    o_ref[...] = x + _dot(jnp.concatenate(heads, axis=1), xwo_ref[...])


def _merge_xattn(x, attn_o, gmlp_o, lru_o, mixg, wmix, bgate, wa, wg, wl, wo,
                 xg, wq, kt, v, xwo, l, casts):
    tiles = SEQ // MERGE_TM
    row = lambda w: pl.BlockSpec((MERGE_TM, w), lambda i: (i, 0))
    (y,), cast = _call_with_casts(
        _merge_xattn_body,
        steps=TOKENS // MERGE_TM,
        in_specs=[row(D_MODEL), row(ATTN_Q_W), row(GMLP_W), row(LRU_W),
                  _layer(mixg, l), _resident(wmix.shape), _layer(bgate, l),
                  _resident(wa.shape), _resident(wg.shape), _resident(wl.shape),
                  _resident(wo.shape),
                  _layer(xg, l), _resident(wq.shape),
                  pl.BlockSpec((1, D_MODEL, MEM_LEN), lambda i: (i // tiles, 0, 0)),
                  pl.BlockSpec((1, MEM_LEN, D_MODEL), lambda i: (i // tiles, 0, 0)),
                  _resident(xwo.shape)],
        out_specs=[row(D_MODEL)],
        out_shape=[jax.ShapeDtypeStruct((TOKENS, D_MODEL), F32)],
        args=(x, attn_o, gmlp_o, lru_o, mixg, wmix, bgate, wa, wg, wl, wo,
              xg, wq, kt, v, xwo),
        casts=casts,
        name="merge_xattn",
    )
    return y, cast


def _rope_tables():
    rows = SEQ // GRID_W
    row = jnp.repeat(jnp.arange(rows), GRID_W).astype(F32)
    col = jnp.tile(jnp.arange(GRID_W), rows).astype(F32)
    n_freq = HEAD_DIM // 4
    inv_freq = ROPE_THETA ** (-jnp.arange(n_freq, dtype=F32) / n_freq)
    ang_r = row[:, None] * inv_freq[None, :]
    ang_c = col[:, None] * inv_freq[None, :]
    cos = jnp.concatenate([jnp.cos(ang_r)] * 2 + [jnp.cos(ang_c)] * 2, axis=1)
    sin = jnp.concatenate([-jnp.sin(ang_r), jnp.sin(ang_r),
                           -jnp.sin(ang_c), jnp.sin(ang_c)], axis=1)
    rep = LANES // HEAD_DIM
    return jnp.tile(cos, (1, rep)), jnp.tile(sin, (1, rep))


def _block_diag(w):
    n, a, b = w.shape[-3:]
    eye = jnp.eye(n, dtype=w.dtype)
    out = eye[:, None, :, None] * w[..., :, :, None, :]
    return out.reshape(w.shape[:-3] + (n * a, n * b))


def kernel(x, mem, ffn1_norm, ffn1_w_in, ffn1_w_out, mix_norm, w_mix_in, b_gate,
           q_norm, k_norm, attn_up, gmlp_v_norm, gmlp_ws, gmlp_bs, gmlp_up,
           lru_conv_w, lru_conv_b, lru_wa, lru_ba, lru_wi, lru_bi, lru_lambda, lru_up,
           w_mix_out, xattn_norm, mem_norm, xattn_wq, xattn_wkv, xattn_wo,
           ffn2_norm, ffn2_w_in, ffn2_w_out, final_norm):
    assert x.shape == (BATCH, SEQ, D_MODEL) and mem.shape == (BATCH, MEM_LEN, D_MODEL)
    vec = lambda a: a.reshape(DEPTH, 1, -1)
    cos, sin = _rope_tables()
    fin = final_norm.reshape(1, D_MODEL)
    gsum = _block_diag(jnp.full((QK_W // HEAD_DIM, HEAD_DIM, HEAD_DIM),
                                1.0 / HEAD_DIM, BF16))
    qkg = vec(jnp.concatenate([jnp.tile(q_norm, (1, ATTN_HEADS)),
                               jnp.tile(k_norm, (1, ATTN_KV_HEADS))], axis=1))
    gm_bias = jnp.repeat(jnp.swapaxes(gmlp_bs, 1, 2), GMLP_GROUP_W, axis=2)
    gm_ws_f32 = gmlp_ws.reshape(DEPTH, GMLP_W, GMLP_CHUNK)

    n_blk = LRU_W // LRU_BLOCK
    heads_per_blk = LRU_BLOCK // LRU_HEAD_W

    def blk(w):
        return _block_diag(
            w.reshape(DEPTH, n_blk, heads_per_blk, LRU_HEAD_W, LRU_HEAD_W))
    lru_w = jnp.concatenate(
        [blk(lru_wa[:, 0]), blk(lru_wi[:, 0]), blk(lru_wa[:, 1]), blk(lru_wi[:, 1])],
        axis=-1).astype(BF16)
    lru_b = jnp.stack(
        [b.reshape(DEPTH, n_blk, LRU_BLOCK)
         for b in (lru_ba[:, 0], lru_bi[:, 0], lru_ba[:, 1], lru_bi[:, 1])],
        axis=2).reshape(DEPTH, 1, -1)

    mixer_f32 = (w_mix_in, xattn_wkv, attn_up, gmlp_up, lru_up, w_mix_out,
                 xattn_wq, xattn_wo, gm_ws_f32)
    ffn1_w = [_to_bf16(ffn1_w_in, 0), _to_bf16(ffn1_w_out, 0)]
    mixer_w = None

    h = x.reshape(TOKENS, D_MODEL)
    for l in range(DEPTH):
        nxt = l + 1 if l + 1 < DEPTH else None
        casts = [(ffn2_w_in, l), (ffn2_w_out, l)]
        if mixer_w is None:
            casts += [(w, l) for w in mixer_f32]
        h, cast = _ffn(h, vec(ffn1_norm), *ffn1_w, fin, l, casts, final=False)
        ffn2_w = cast[:2]
        if mixer_w is None:
            mixer_w = cast[2:]
        w_mix, w_xkv, w_attn_up, w_gmlp_up, w_lru_up, w_out, w_xq, w_xo, gm_ws = mixer_w

        (qt, k, vt, gmlp_o, lr), _ = _mix_in(
            h, vec(mix_norm), w_mix, qkg, gsum, cos, sin,
            vec(gmlp_v_norm), gm_ws.reshape(gmlp_ws.shape[1:]), gm_bias, l, [])
        attn_o = _attention(qkg, qt, k, vt, l)
        lru_o = _lru(lr, lru_conv_w, vec(lru_conv_b), lru_w, lru_b, lru_lambda, l)
        xkt, xv = _xkv(mem, vec(mem_norm), w_xkv, l)
        h, ffn1_w = _merge_xattn(
            h, attn_o, gmlp_o, lru_o, vec(mix_norm), w_mix, vec(b_gate),
            w_attn_up, w_gmlp_up, w_lru_up, w_out, vec(xattn_norm), w_xq, xkt, xv, w_xo,
            l, [] if nxt is None else [(ffn1_w_in, nxt), (ffn1_w_out, nxt)])
        h, mixer_w = _ffn(h, vec(ffn2_norm), *ffn2_w, fin, l,
                          [] if nxt is None else [(w, nxt) for w in mixer_f32],
                          final=nxt is None)
    return h.reshape(BATCH, SEQ, D_MODEL)
```

```python
import functools

import jax
import jax.numpy as jnp
from jax import lax
from jax.experimental import pallas as pl
from jax.experimental.pallas import tpu as pltpu

F32 = jnp.float32
BF16 = jnp.bfloat16

D_MODEL = 1024
BATCH = 4
SEQ = 4096
DEPTH = 2
TOKENS = BATCH * SEQ
MEM_LEN = 256
GRID_W = 64
EPS = 1e-6

ATTN_HEADS = 8
ATTN_KV_HEADS = 2
ATTN_GROUP = ATTN_HEADS // ATTN_KV_HEADS
HEAD_DIM = 64
ATTN_Q_W = ATTN_HEADS * HEAD_DIM
ATTN_KV_W = ATTN_KV_HEADS * HEAD_DIM
QK_W = ATTN_Q_W + ATTN_KV_W
GROUP_Q_W = ATTN_GROUP * HEAD_DIM
ROPE_THETA = 10000.0
LOG2_E = 1.4426950408889634

GMLP_W = 512
GMLP_GROUPS = 4
GMLP_GROUP_W = GMLP_W // GMLP_GROUPS
GMLP_CHUNK = 128

LRU_W = 512
LRU_HEADS = 8
LRU_HEAD_W = LRU_W // LRU_HEADS
CONV_W = 4
LRU_C = 8.0
N_DIR = 2
N_BRANCH = 3

XATTN_HEADS = 4
XATTN_HEAD_DIM = D_MODEL // XATTN_HEADS
D_FF = 2816

MIX_V_LO = ATTN_Q_W + ATTN_KV_W
MIX_GM_LO = MIX_V_LO + ATTN_KV_W
MIX_LR_LO = MIX_GM_LO + 2 * GMLP_W
MIX_GATE_LO = MIX_LR_LO + 2 * LRU_W

LANES = 128
SUBLANES = 8
BF16_ROWS = 16
VMEM_LIMIT_BYTES = 56 * 1024 * 1024

CAST_ROWS = 256
FFN_TM = 512
FFN_CHUNK = 256
MIX_TM = 512
ATTN_TQ = 1024
ATTN_CHAIN_Q = 256
ATTN_SLOTS = 4
ATTN_PAIR = 2
MERGE_TM = 512
LRU_BLOCK = 256
LRU_SLABS = LRU_BLOCK // LANES
LRU_ROWS = 512
LRU_SCAN_UNROLL = 4
LRU_SEG = 516
LRU_PAD_SEQ = SUBLANES * LRU_SEG
LRU_HALO = SUBLANES


def _resident(shape):
    nd = len(shape)
    return pl.BlockSpec(shape, lambda *_: (0,) * nd, pipeline_mode=pl.Buffered(1))


def _layer(arr, l, cols=None):
    shape = arr.shape[1:] if cols is None else arr.shape[1:-1] + (cols,)
    nd = len(shape)
    return pl.BlockSpec((None,) + shape, lambda *_: (l,) + (0,) * nd,
                        pipeline_mode=pl.Buffered(1))


def _params(n_grid_axes):
    return pltpu.CompilerParams(
        dimension_semantics=("arbitrary",) * n_grid_axes,
        vmem_limit_bytes=VMEM_LIMIT_BYTES,
    )


def _rms(x, g):
    ms = jnp.mean(x * x, axis=-1, keepdims=True)
    return x * lax.rsqrt(ms + EPS) * g


def _dot(a, b):
    return jnp.dot(a, b, preferred_element_type=F32)


def _cast_body(w_ref, o_ref):
    o_ref[...] = w_ref[...].astype(BF16)


def _to_bf16(w, l):
    _, rows, cols = w.shape
    return pl.pallas_call(
        _cast_body,
        grid=(rows // CAST_ROWS,),
        in_specs=[pl.BlockSpec((None, CAST_ROWS, cols), lambda i: (l, i, 0))],
        out_specs=pl.BlockSpec((CAST_ROWS, cols), lambda i: (i, 0)),
        out_shape=jax.ShapeDtypeStruct((rows, cols), BF16),
        compiler_params=_params(1),
        name="cast_bf16",
    )(w)


def _cast_block_rows(rows, steps):
    for n in range(min(steps, rows // BF16_ROWS), 0, -1):
        if rows % n == 0 and (rows // n) % BF16_ROWS == 0:
            return rows // n
    raise ValueError((rows, steps))


def _call_with_casts(body, *, steps, in_specs, out_specs, out_shape, args, casts,
                     name, scratch_shapes=()):
    n_in, n_out, n_cast = len(in_specs), len(out_specs), len(casts)
    c_in, c_out, c_shape = [], [], []
    for w, l in casts:
        _, rows, cols = w.shape
        blk = _cast_block_rows(rows, steps)
        last = rows // blk - 1
        c_in.append(pl.BlockSpec((None, blk, cols),
                                 lambda i, l=l, last=last: (l, jnp.minimum(i, last), 0)))
        c_out.append(pl.BlockSpec((blk, cols),
                                  lambda i, last=last: (jnp.minimum(i, last), 0)))
        c_shape.append(jax.ShapeDtypeStruct((rows, cols), BF16))

    def with_casts(*refs):
        ins, srcs = refs[:n_in], refs[n_in:n_in + n_cast]
        outs = refs[n_in + n_cast:n_in + n_cast + n_out]
        dsts = refs[n_in + n_cast + n_out:n_in + 2 * n_cast + n_out]
        for src, dst in zip(srcs, dsts):
            dst[...] = src[...].astype(BF16)
        body(*ins, *outs, *refs[n_in + 2 * n_cast + n_out:])

    res = pl.pallas_call(
        with_casts,
        grid=(steps,),
        in_specs=list(in_specs) + c_in,
        out_specs=list(out_specs) + c_out,
        out_shape=list(out_shape) + c_shape,
        scratch_shapes=list(scratch_shapes),
        compiler_params=_params(1),
        name=name,
    )(*args, *[w for w, _ in casts])
    return res[:n_out], res[n_out:]


def _ffn_body(x_ref, g_ref, win_ref, wout_ref, fin_ref, o_ref, *, final):
    x = x_ref[...]
    xn = _rms(x, g_ref[...]).astype(BF16)
    acc = jnp.zeros(x.shape, F32)
    for c in range(D_FF // FFN_CHUNK):
        lo = c * FFN_CHUNK
        a = _dot(xn, win_ref[:, lo:lo + FFN_CHUNK])
        b = _dot(xn, win_ref[:, D_FF + lo:D_FF + lo + FFN_CHUNK])
        h = (a * jax.nn.sigmoid(a) * b).astype(BF16)
        acc = acc + _dot(h, wout_ref[lo:lo + FFN_CHUNK, :])
    y = x + 0.5 * acc
    if final:
        y = _rms(y, fin_ref[...])
    o_ref[...] = y


def _ffn(x, g, w_in, w_out, fin, l, casts, *, final):
    tile = pl.BlockSpec((FFN_TM, D_MODEL), lambda i: (i, 0))
    (y,), cast = _call_with_casts(
        functools.partial(_ffn_body, final=final),
        steps=TOKENS // FFN_TM,
        in_specs=[tile, _layer(g, l), _resident(w_in.shape), _resident(w_out.shape),
                  _resident((1, D_MODEL))],
        out_specs=[tile],
        out_shape=[jax.ShapeDtypeStruct((TOKENS, D_MODEL), F32)],
        args=(x, g, w_in, w_out, fin),
        casts=casts,
        name="ffn_final" if final else "ffn",
    )
    return y, cast


def _lane_partner(x, d):
    lane = lax.broadcasted_iota(jnp.int32, x.shape, 1)
    up = pltpu.roll(x, LANES - d, 1)
    down = pltpu.roll(x, d, 1)
    return jnp.where((lane & d) == 0, up, down)


def _mix_in_body(x_ref, g_ref, w_ref, qkg_ref, gsum_ref, cos_ref, sin_ref,
                 vn_ref, ws_ref, gbias_ref,
                 qt_ref, k_ref, vt_ref, gmlp_ref, lr_ref):
    xn = _rms(x_ref[...], g_ref[...]).astype(BF16)

    def proj(lo, hi):
        return _dot(xn, w_ref[:, lo:hi])

    qk = proj(0, MIX_V_LO)
    sq = qk * qk
    sq_hi = sq.astype(BF16)
    sq_lo = (sq - sq_hi.astype(F32)).astype(BF16)
    ms = _dot(sq_hi, gsum_ref[...]) + _dot(sq_lo, gsum_ref[...])

    u = jax.nn.gelu(proj(MIX_GM_LO, MIX_GM_LO + GMLP_W))
    v = _rms(jax.nn.gelu(proj(MIX_GM_LO + GMLP_W, MIX_LR_LO)), vn_ref[...]).astype(BF16)
    vt_ref[0] = proj(MIX_V_LO, MIX_GM_LO).T.astype(BF16)
    lr_ref[...] = proj(MIX_LR_LO, MIX_GATE_LO)

    gbias = gbias_ref[...]
    for c in range(MIX_TM // GMLP_CHUNK):
        rows = slice(c * GMLP_CHUNK, (c + 1) * GMLP_CHUNK)
        sv = jnp.concatenate(
            [_dot(ws_ref[g], v[rows, g * GMLP_GROUP_W:(g + 1) * GMLP_GROUP_W])
             for g in range(GMLP_GROUPS)], axis=1)
        gmlp_ref[rows, :] = (u[rows] * (sv + gbias)).astype(BF16)

    qkn = qk * lax.rsqrt(ms + EPS) * qkg_ref[...]
    cos = cos_ref[...]
    sin = sin_ref[...]
    cols = []
    for c in range(QK_W // LANES):
        t = qkn[:, c * LANES:(c + 1) * LANES]
        cols.append(t * cos + _lane_partner(t, HEAD_DIM // 4) * sin)
    q = jnp.concatenate(cols[:ATTN_Q_W // LANES], axis=1) * (HEAD_DIM ** -0.5 * LOG2_E)
    qt_ref[0] = q.T.astype(BF16)
    k_ref[...] = cols[-1].astype(BF16)


def _mix_in(x, g, w, qkg, gsum, cos, sin, vn, ws, gbias, l, casts):
    seq_tiles = SEQ // MIX_TM
    row = lambda w_: pl.BlockSpec((MIX_TM, w_), lambda i: (i, 0))
    tab = pl.BlockSpec((MIX_TM, LANES), lambda i: (i % seq_tiles, 0))
    return _call_with_casts(
        _mix_in_body,
        steps=TOKENS // MIX_TM,
        in_specs=[row(D_MODEL), _layer(g, l),
                  _resident((D_MODEL, MIX_GATE_LO)),
                  _layer(qkg, l), _resident(gsum.shape), tab, tab,
                  _layer(vn, l), _resident(ws.shape), _layer(gbias, l)],
        out_specs=[
            pl.BlockSpec((1, ATTN_Q_W, MIX_TM),
                         lambda i: (i // seq_tiles, 0, i % seq_tiles)),
            row(ATTN_KV_W),
            pl.BlockSpec((1, ATTN_KV_W, MIX_TM),
                         lambda i: (i // seq_tiles, 0, i % seq_tiles)),
            row(GMLP_W), row(2 * LRU_W)],
        out_shape=[
            jax.ShapeDtypeStruct((BATCH, ATTN_Q_W, SEQ), BF16),
            jax.ShapeDtypeStruct((TOKENS, ATTN_KV_W), BF16),
            jax.ShapeDtypeStruct((BATCH, ATTN_KV_W, SEQ), BF16),
            jax.ShapeDtypeStruct((TOKENS, GMLP_W), BF16),
            jax.ShapeDtypeStruct((TOKENS, 2 * LRU_W), F32)],
        args=(x, g, w, qkg, gsum, cos, sin, vn, ws, gbias),
        casts=casts,
        name="mix_in",
    )


ATTN_SUM_ROWS = 16
ATTN_MAX_EXPONENT_SPAN = 64.0
ATTN_ROUNDING_SLACK = 1.0 + 2.0 ** -8


def _attn_body(qg_ref, qt_ref, k_ref, vt_ref, o_ref,
               vt_aug, k_aug, shift_ref, st_buf, pt_buf):
    own_lanes = (lax.broadcasted_iota(jnp.int32, (1, ATTN_KV_W), 1) // HEAD_DIM
                 == pl.program_id(1))

    @pl.when(pl.program_id(2) == 0)
    def _():
        vt_aug[0:HEAD_DIM, :] = vt_ref[0]
        vt_aug[HEAD_DIM:, :] = jnp.ones((ATTN_SUM_ROWS, SEQ), BF16)
        kb = k_ref[...]
        k_aug[:, :ATTN_KV_W] = kb
        k_aug[:, ATTN_KV_W:] = jnp.ones((SEQ, ATTN_KV_W), BF16)
        kf = kb.astype(F32)
        norm2 = jnp.sum(jnp.where(own_lanes, kf * kf, 0.0), axis=1, keepdims=True)
        kmax = jnp.sqrt(jnp.max(norm2, axis=0, keepdims=True))
        qmax = jnp.max(jnp.abs(qg_ref[:, :ATTN_Q_W]), axis=1, keepdims=True)
        shift_ref[0] = jnp.max(qmax * kmax) * (LOG2_E * ATTN_ROUNDING_SLACK)

    kv_head = lax.broadcasted_iota(jnp.int32, (ATTN_KV_W, 1), 0) // HEAD_DIM
    own_rows = kv_head == pl.program_id(1)
    qt = qt_ref[0]

    n_sub = ATTN_TQ // ATTN_CHAIN_Q
    chains = [(sub, h) for sub in range(n_sub) for h in range(ATTN_GROUP)]

    def q_weights(c):
        sub, h = chains[c]
        qh = qt[h * HEAD_DIM:(h + 1) * HEAD_DIM,
                sub * ATTN_CHAIN_Q:(sub + 1) * ATTN_CHAIN_Q]
        stacked = jnp.concatenate([qh] * ATTN_KV_HEADS, axis=0)
        return jnp.where(own_rows, stacked, jnp.zeros_like(stacked))

    def finish(outs):
        ot_all = jnp.concatenate([jnp.concatenate(o, axis=0) for o in outs], axis=1)
        o_ref[...] = ot_all.T.astype(BF16)

    shift = shift_ref[0]
    shift_is_safe = shift * 2.0 <= ATTN_MAX_EXPONENT_SPAN

    @pl.when(shift_is_safe)
    def _():
        shift_row = lax.broadcasted_iota(jnp.int32, (ATTN_KV_W, ATTN_CHAIN_Q), 0) == 0
        tail = jnp.where(shift_row, -shift, 0.0).astype(BF16)
        outs = [[None] * ATTN_GROUP for _ in range(n_sub)]
        for first in range(0, len(chains), ATTN_PAIR):
            pair = range(first, first + ATTN_PAIR)
            for c in pair:
                w = jnp.concatenate([q_weights(c), tail], axis=0)
                pt_buf[c % ATTN_SLOTS] = jnp.exp2(_dot(k_aug[...], w)).astype(BF16)
            for c in pair:
                sub, h = chains[c]
                ot = _dot(vt_aug[...], pt_buf[c % ATTN_SLOTS])
                outs[sub][h] = ot[:HEAD_DIM] / ot[HEAD_DIM:HEAD_DIM + 1]
        finish(outs)

    @pl.when(jnp.logical_not(shift_is_safe))
    def _():
        k = k_ref[...]
        col_max = {}

        def scores(c):
            st = _dot(k, q_weights(c))
            col_max[c] = jnp.max(st, axis=0, keepdims=True)
            st_buf[c % ATTN_SLOTS] = st

        for c in range(min(ATTN_SLOTS, len(chains))):
            scores(c)
        outs = [[None] * ATTN_GROUP for _ in range(n_sub)]
        for c, (sub, h) in enumerate(chains):
            slot = c % ATTN_SLOTS
            pt_buf[slot] = jnp.exp2(st_buf[slot] - col_max[c]).astype(BF16)
            ot = _dot(vt_aug[...], pt_buf[slot])
            outs[sub][h] = ot[:HEAD_DIM] / ot[HEAD_DIM:HEAD_DIM + 1]
            if c + ATTN_SLOTS < len(chains):
                scores(c + ATTN_SLOTS)
        finish(outs)


def _attention(qkg, qt, k, vt, l):
    q_tiles = SEQ // ATTN_TQ
    return pl.pallas_call(
        _attn_body,
        grid=(BATCH, ATTN_KV_HEADS, q_tiles),
        in_specs=[
            _layer(qkg, l),
            pl.BlockSpec((1, GROUP_Q_W, ATTN_TQ), lambda b, j, i: (b, j, i)),
            pl.BlockSpec((SEQ, ATTN_KV_W), lambda b, j, i: (b, 0)),
            pl.BlockSpec((1, HEAD_DIM, SEQ), lambda b, j, i: (b, j, 0)),
        ],
        out_specs=pl.BlockSpec((ATTN_TQ, GROUP_Q_W),
                               lambda b, j, i: (b * q_tiles + i, j)),
        out_shape=jax.ShapeDtypeStruct((TOKENS, ATTN_Q_W), BF16),
        scratch_shapes=[pltpu.VMEM((HEAD_DIM + ATTN_SUM_ROWS, SEQ), BF16),
                        pltpu.VMEM((SEQ, 2 * ATTN_KV_W), BF16),
                        pltpu.SMEM((1,), F32),
                        pltpu.VMEM((ATTN_SLOTS, SEQ, ATTN_CHAIN_Q), F32),
                        pltpu.VMEM((ATTN_SLOTS, SEQ, ATTN_CHAIN_Q), BF16)],
        compiler_params=_params(3),
        name="gqa_attention",
    )(qkg, qt, k, vt)


def _sigmoid(x):
    return 0.5 * jnp.tanh(0.5 * x) + 0.5


def _lru_body(lx_ref, ly_ref, cw_ref, cb_ref, w_ref, b_ref, lam_ref, o_ref,
              xpad, a_f, b_f, a_b, b_b, carry_f, carry_b):
    n_chunks = SEQ // LRU_ROWS
    halo_zeros = jnp.zeros((LRU_SLABS, LRU_HALO, LANES), F32)
    xpad[:, 0:LRU_HALO, :] = halo_zeros
    xpad[:, LRU_HALO + SEQ:LRU_HALO + SEQ + LRU_HALO, :] = halo_zeros
    for s in range(LRU_SLABS):
        xpad[s, LRU_HALO:LRU_HALO + SEQ, :] = lx_ref[:, s * LANES:(s + 1) * LANES]

    rate = -(LRU_C * LOG2_E) * jax.nn.softplus(-lam_ref[...])
    cw = cw_ref[...]
    cb = cb_ref[...]
    bias = b_ref[...]
    scan_bufs = ((a_f, b_f), (a_b, b_b))

    def dense(c, carry):
        r0 = pl.multiple_of(c * LRU_ROWS, LRU_ROWS)
        slabs = []
        for s in range(LRU_SLABS):
            lanes = slice(s * LANES, (s + 1) * LANES)
            acc = cb[:, lanes]
            for j in range(CONV_W):
                off = LRU_HALO + j - CONV_W // 2
                acc = acc + cw[j:j + 1, lanes] * xpad[s, pl.ds(r0 + off, LRU_ROWS), :]
            slabs.append(acc)
        xc = jnp.concatenate(slabs, axis=1)
        pre = _dot(xc.astype(BF16), w_ref[...]) + bias
        for d in range(N_DIR):
            r = _sigmoid(pre[:, (2 * d) * LRU_BLOCK:(2 * d + 1) * LRU_BLOCK])
            i = _sigmoid(pre[:, (2 * d + 1) * LRU_BLOCK:(2 * d + 2) * LRU_BLOCK])
            a = jnp.exp2(r * rate[d:d + 1, :])
            g2 = 1.0 - a * a
            gain = jnp.where(g2 > 0.0, g2 * lax.rsqrt(g2), 0.0)
            b = gain * (i * xc)
            a_buf, b_buf = scan_bufs[d]
            for s in range(LRU_SLABS):
                lanes = slice(s * LANES, (s + 1) * LANES)
                a_buf[s, pl.ds(r0, LRU_ROWS), :] = a[:, lanes]
                b_buf[s, pl.ds(r0, LRU_ROWS), :] = b[:, lanes]
        return carry

    lax.fori_loop(0, n_chunks, dense, 0)
    tail = LRU_PAD_SEQ - SEQ
    for a_buf, b_buf in scan_bufs:
        a_buf[:, SEQ:LRU_PAD_SEQ, :] = jnp.ones((LRU_SLABS, tail, LANES), F32)
        b_buf[:, SEQ:LRU_PAD_SEQ, :] = jnp.zeros((LRU_SLABS, tail, LANES), F32)

    def seg(t):
        return pl.ds(t, SUBLANES, stride=LRU_SEG)

    def scan_step(k, carry):
        out = []
        for (a_buf, b_buf), t, chains in ((scan_bufs[0], k, carry[0]),
                                          (scan_bufs[1], LRU_SEG - 1 - k, carry[1])):
            new = []
            for s, (h, p) in enumerate(chains):
                a = a_buf[s, seg(t), :]
                h = a * h + b_buf[s, seg(t), :]
                p = a * p
                b_buf[s, seg(t), :] = h
                a_buf[s, seg(t), :] = p
                new.append((h, p))
            out.append(tuple(new))
        return tuple(out)

    z = jnp.zeros((SUBLANES, LANES), F32)
    o = jnp.ones((SUBLANES, LANES), F32)
    init = tuple(tuple((z, o) for _ in range(LRU_SLABS)) for _ in range(N_DIR))
    lax.fori_loop(0, LRU_SEG, scan_step, init, unroll=LRU_SCAN_UNROLL)

    row0 = jnp.zeros((1, LANES), F32)
    for s in range(LRU_SLABS):
        carry_f[s, 0:1, :] = row0
        c = row0
        for g in range(1, SUBLANES):
            last = g * LRU_SEG - 1
            c = b_f[s, last:last + 1, :] + a_f[s, last:last + 1, :] * c
            carry_f[s, g:g + 1, :] = c
        carry_b[s, SUBLANES - 1:SUBLANES, :] = row0
        c = row0
        for g in range(SUBLANES - 2, -1, -1):
            first = (g + 1) * LRU_SEG
            c = b_b[s, first:first + 1, :] + a_b[s, first:first + 1, :] * c
            carry_b[s, g:g + 1, :] = c

    seg_shift = LRU_SEG - LRU_ROWS
    row_in_chunk = lax.broadcasted_iota(jnp.int32, (LRU_ROWS, LANES), 0)

    def emit(c, carry):
        r0 = pl.multiple_of(c * LRU_ROWS, LRU_ROWS)
        rows = pl.ds(r0, LRU_ROWS)
        prev = pl.ds(jnp.maximum(c - 1, 0), 1)
        in_prev = row_in_chunk < c * seg_shift
        hs = []
        for s in range(LRU_SLABS):
            cf = jnp.where(in_prev, carry_f[s, prev, :], carry_f[s, pl.ds(c, 1), :])
            cb_ = jnp.where(in_prev, carry_b[s, prev, :], carry_b[s, pl.ds(c, 1), :])
            hs.append((b_f[s, rows, :] + a_f[s, rows, :] * cf)
                      + (b_b[s, rows, :] + a_b[s, rows, :] * cb_))
        h = jnp.concatenate(hs, axis=1)
        o_ref[rows, :] = (h * jax.nn.gelu(ly_ref[rows, :])).astype(BF16)
        return carry

    lax.fori_loop(0, n_chunks, emit, 0)


def _lru(lr, cw, cb, w, b, lam, l):
    n_blk = LRU_W // LRU_BLOCK
    scan_buf = pltpu.VMEM((LRU_SLABS, LRU_PAD_SEQ, LANES), F32)
    carry_buf = pltpu.VMEM((LRU_SLABS, SUBLANES, LANES), F32)
    return pl.pallas_call(
        _lru_body,
        grid=(BATCH, n_blk),
        in_specs=[
            pl.BlockSpec((SEQ, LRU_BLOCK), lambda b_, c: (b_, c)),
            pl.BlockSpec((SEQ, LRU_BLOCK), lambda b_, c: (b_, n_blk + c)),
            pl.BlockSpec((None, CONV_W, LRU_BLOCK), lambda b_, c: (l, 0, c)),
            pl.BlockSpec((None, 1, LRU_BLOCK), lambda b_, c: (l, 0, c)),
            pl.BlockSpec((None, None, LRU_BLOCK, 2 * N_DIR * LRU_BLOCK),
                         lambda b_, c: (l, c, 0, 0)),
            pl.BlockSpec((None, 1, 2 * N_DIR * LRU_BLOCK), lambda b_, c: (l, 0, c)),
            pl.BlockSpec((None, N_DIR, LRU_BLOCK), lambda b_, c: (l, 0, c)),
        ],
        out_specs=pl.BlockSpec((SEQ, LRU_BLOCK), lambda b_, c: (b_, c)),
        out_shape=jax.ShapeDtypeStruct((TOKENS, LRU_W), BF16),
        scratch_shapes=[
            pltpu.VMEM((LRU_SLABS, SEQ + 2 * LRU_HALO, LANES), F32),
            scan_buf, scan_buf, scan_buf, scan_buf,
            carry_buf, carry_buf,
        ],
        compiler_params=_params(2),
        name="rg_lru",
    )(lr, lr, cw, cb, w, b, lam)


def _xkv_body(mem_ref, g_ref, wkv_ref, kt_ref, v_ref):
    mn = _rms(mem_ref[0], g_ref[...]).astype(BF16)
    kv = _dot(mn, wkv_ref[...])
    kt_ref[0] = (kv[:, :D_MODEL] * (XATTN_HEAD_DIM ** -0.5)).T.astype(BF16)
    v_ref[0] = kv[:, D_MODEL:].astype(BF16)


def _xkv(mem, g, wkv, l):
    return pl.pallas_call(
        _xkv_body,
        grid=(BATCH,),
        in_specs=[pl.BlockSpec((1, MEM_LEN, D_MODEL), lambda b: (b, 0, 0)),
                  _layer(g, l), _resident(wkv.shape)],
        out_specs=[pl.BlockSpec((1, D_MODEL, MEM_LEN), lambda b: (b, 0, 0)),
                   pl.BlockSpec((1, MEM_LEN, D_MODEL), lambda b: (b, 0, 0))],
        out_shape=[jax.ShapeDtypeStruct((BATCH, D_MODEL, MEM_LEN), BF16),
                   jax.ShapeDtypeStruct((BATCH, MEM_LEN, D_MODEL), BF16)],
        compiler_params=_params(1),
        name="xattn_kv",
    )(mem, g, wkv)


def _merge_xattn_body(x_ref, a_ref, g_ref, l_ref, mixg_ref, wmix_ref, bgate_ref,
                      wa_ref, wg_ref, wl_ref, wo_ref,
                      xg_ref, wq_ref, kt_ref, v_ref, xwo_ref, o_ref):
    x = x_ref[...]
    hn = _rms(x, mixg_ref[...]).astype(BF16)
    merged = jnp.zeros((MERGE_TM, D_MODEL), F32)
    for k, (br, w) in enumerate(((a_ref, wa_ref), (g_ref, wg_ref), (l_ref, wl_ref))):
        lo = MIX_GATE_LO + k * D_MODEL
        gate = jax.nn.sigmoid(_dot(hn, wmix_ref[:, lo:lo + D_MODEL])
                              + bgate_ref[:, k * D_MODEL:(k + 1) * D_MODEL])
        merged = merged + gate * _dot(br[...], w[...])
    x = x + _dot(merged.astype(BF16), wo_ref[...])

    q = _dot(_rms(x, xg_ref[...]).astype(BF16), wq_ref[...]).astype(BF16)
    heads = []
    for h in range(XATTN_HEADS):
        cols = slice(h * XATTN_HEAD_DIM, (h + 1) * XATTN_HEAD_DIM)
        s = _dot(q[:, cols], kt_ref[0, cols, :])
        p = jnp.exp(s - jnp.max(s, axis=-1, keepdims=True))
        l = jnp.sum(p, axis=-1, keepdims=True)
        heads.append((_dot(p.astype(BF16), v_ref[0, :, cols]) * (1.0 / l)).astype(BF16))
    o_ref[...] = x + _dot(jnp.concatenate(heads, axis=1), xwo_ref[...])


def _merge_xattn(x, attn_o, gmlp_o, lru_o, mixg, wmix, bgate, wa, wg, wl, wo,
                 xg, wq, kt, v, xwo, l, casts):
    tiles = SEQ // MERGE_TM
    row = lambda w: pl.BlockSpec((MERGE_TM, w), lambda i: (i, 0))
    (y,), cast = _call_with_casts(
        _merge_xattn_body,
        steps=TOKENS // MERGE_TM,
        in_specs=[row(D_MODEL), row(ATTN_Q_W), row(GMLP_W), row(LRU_W),
                  _layer(mixg, l), _resident(wmix.shape), _layer(bgate, l),
                  _resident(wa.shape), _resident(wg.shape), _resident(wl.shape),
                  _resident(wo.shape),
                  _layer(xg, l), _resident(wq.shape),
                  pl.BlockSpec((1, D_MODEL, MEM_LEN), lambda i: (i // tiles, 0, 0)),
                  pl.BlockSpec((1, MEM_LEN, D_MODEL), lambda i: (i // tiles, 0, 0)),
                  _resident(xwo.shape)],
        out_specs=[row(D_MODEL)],
        out_shape=[jax.ShapeDtypeStruct((TOKENS, D_MODEL), F32)],
        args=(x, attn_o, gmlp_o, lru_o, mixg, wmix, bgate, wa, wg, wl, wo,
              xg, wq, kt, v, xwo),
        casts=casts,
        name="merge_xattn",
    )
    return y, cast


def _rope_tables():
    rows = SEQ // GRID_W
    row = jnp.repeat(jnp.arange(rows), GRID_W).astype(F32)
    col = jnp.tile(jnp.arange(GRID_W), rows).astype(F32)
    n_freq = HEAD_DIM // 4
    inv_freq = ROPE_THETA ** (-jnp.arange(n_freq, dtype=F32) / n_freq)
    ang_r = row[:, None] * inv_freq[None, :]
    ang_c = col[:, None] * inv_freq[None, :]
    cos = jnp.concatenate([jnp.cos(ang_r)] * 2 + [jnp.cos(ang_c)] * 2, axis=1)
    sin = jnp.concatenate([-jnp.sin(ang_r), jnp.sin(ang_r),
                           -jnp.sin(ang_c), jnp.sin(ang_c)], axis=1)
    rep = LANES // HEAD_DIM
    return jnp.tile(cos, (1, rep)), jnp.tile(sin, (1, rep))


def _block_diag(w):
    n, a, b = w.shape[-3:]
    eye = jnp.eye(n, dtype=w.dtype)
    out = eye[:, None, :, None] * w[..., :, :, None, :]
    return out.reshape(w.shape[:-3] + (n * a, n * b))


def kernel(x, mem, ffn1_norm, ffn1_w_in, ffn1_w_out, mix_norm, w_mix_in, b_gate,
           q_norm, k_norm, attn_up, gmlp_v_norm, gmlp_ws, gmlp_bs, gmlp_up,
           lru_conv_w, lru_conv_b, lru_wa, lru_ba, lru_wi, lru_bi, lru_lambda, lru_up,
           w_mix_out, xattn_norm, mem_norm, xattn_wq, xattn_wkv, xattn_wo,
           ffn2_norm, ffn2_w_in, ffn2_w_out, final_norm):
    assert x.shape == (BATCH, SEQ, D_MODEL) and mem.shape == (BATCH, MEM_LEN, D_MODEL)
    vec = lambda a: a.reshape(DEPTH, 1, -1)
    cos, sin = _rope_tables()
    fin = final_norm.reshape(1, D_MODEL)
    gsum = _block_diag(jnp.full((QK_W // HEAD_DIM, HEAD_DIM, HEAD_DIM),
                                1.0 / HEAD_DIM, BF16))
    qkg = vec(jnp.concatenate([jnp.tile(q_norm, (1, ATTN_HEADS)),
                               jnp.tile(k_norm, (1, ATTN_KV_HEADS))], axis=1))
    gm_bias = jnp.repeat(jnp.swapaxes(gmlp_bs, 1, 2), GMLP_GROUP_W, axis=2)
    gm_ws_f32 = gmlp_ws.reshape(DEPTH, GMLP_W, GMLP_CHUNK)

    n_blk = LRU_W // LRU_BLOCK
    heads_per_blk = LRU_BLOCK // LRU_HEAD_W

    def blk(w):
        return _block_diag(
            w.reshape(DEPTH, n_blk, heads_per_blk, LRU_HEAD_W, LRU_HEAD_W))
    lru_w = jnp.concatenate(
        [blk(lru_wa[:, 0]), blk(lru_wi[:, 0]), blk(lru_wa[:, 1]), blk(lru_wi[:, 1])],
        axis=-1).astype(BF16)
    lru_b = jnp.stack(
        [b.reshape(DEPTH, n_blk, LRU_BLOCK)
         for b in (lru_ba[:, 0], lru_bi[:, 0], lru_ba[:, 1], lru_bi[:, 1])],
        axis=2).reshape(DEPTH, 1, -1)

    mixer_f32 = (w_mix_in, xattn_wkv, attn_up, gmlp_up, lru_up, w_mix_out,
                 xattn_wq, xattn_wo, gm_ws_f32)
    ffn1_w = [_to_bf16(ffn1_w_in, 0), _to_bf16(ffn1_w_out, 0)]
    mixer_w = None

    h = x.reshape(TOKENS, D_MODEL)
    for l in range(DEPTH):
        nxt = l + 1 if l + 1 < DEPTH else None
        casts = [(ffn2_w_in, l), (ffn2_w_out, l)]
        if mixer_w is None:
            casts += [(w, l) for w in mixer_f32]
        h, cast = _ffn(h, vec(ffn1_norm), *ffn1_w, fin, l, casts, final=False)
        ffn2_w = cast[:2]
        if mixer_w is None:
            mixer_w = cast[2:]
        w_mix, w_xkv, w_attn_up, w_gmlp_up, w_lru_up, w_out, w_xq, w_xo, gm_ws = mixer_w

        (qt, k, vt, gmlp_o, lr), _ = _mix_in(
            h, vec(mix_norm), w_mix, qkg, gsum, cos, sin,
            vec(gmlp_v_norm), gm_ws.reshape(gmlp_ws.shape[1:]), gm_bias, l, [])
        attn_o = _attention(qkg, qt, k, vt, l)
        lru_o = _lru(lr, lru_conv_w, vec(lru_conv_b), lru_w, lru_b, lru_lambda, l)
        xkt, xv = _xkv(mem, vec(mem_norm), w_xkv, l)
        h, ffn1_w = _merge_xattn(
            h, attn_o, gmlp_o, lru_o, vec(mix_norm), w_mix, vec(b_gate),
            w_attn_up, w_gmlp_up, w_lru_up, w_out, vec(xattn_norm), w_xq, xkt, xv, w_xo,
            l, [] if nxt is None else [(ffn1_w_in, nxt), (ffn1_w_out, nxt)])
        h, mixer_w = _ffn(h, vec(ffn2_norm), *ffn2_w, fin, l,
                          [] if nxt is None else [(w, nxt) for w in mixer_f32],
                          final=nxt is None)
    return h.reshape(BATCH, SEQ, D_MODEL)
```

```python
import functools

import jax
import jax.numpy as jnp
from jax import lax
from jax.experimental import pallas as pl
from jax.experimental.pallas import tpu as pltpu

F32 = jnp.float32
BF16 = jnp.bfloat16

D_MODEL = 1024
BATCH = 4
SEQ = 4096
DEPTH = 2
TOKENS = BATCH * SEQ
MEM_LEN = 256
GRID_W = 64
EPS = 1e-6

ATTN_HEADS = 8
ATTN_KV_HEADS = 2
ATTN_GROUP = ATTN_HEADS // ATTN_KV_HEADS
HEAD_DIM = 64
ATTN_Q_W = ATTN_HEADS * HEAD_DIM
ATTN_KV_W = ATTN_KV_HEADS * HEAD_DIM
QK_W = ATTN_Q_W + ATTN_KV_W
GROUP_Q_W = ATTN_GROUP * HEAD_DIM
ROPE_THETA = 10000.0
LOG2_E = 1.4426950408889634

GMLP_W = 512
GMLP_GROUPS = 4
GMLP_GROUP_W = GMLP_W // GMLP_GROUPS
GMLP_CHUNK = 128

LRU_W = 512
LRU_HEADS = 8
LRU_HEAD_W = LRU_W // LRU_HEADS
CONV_W = 4
LRU_C = 8.0
N_DIR = 2
N_BRANCH = 3

XATTN_HEADS = 4
XATTN_HEAD_DIM = D_MODEL // XATTN_HEADS
D_FF = 2816

MIX_V_LO = ATTN_Q_W + ATTN_KV_W
MIX_GM_LO = MIX_V_LO + ATTN_KV_W
MIX_LR_LO = MIX_GM_LO + 2 * GMLP_W
MIX_GATE_LO = MIX_LR_LO + 2 * LRU_W

LANES = 128
SUBLANES = 8
BF16_ROWS = 16
VMEM_LIMIT_BYTES = 56 * 1024 * 1024

CAST_ROWS = 256
FFN_TM = 1024
FFN_CHUNK = 256
MIX_TM = 1024
ATTN_TQ = 512
ATTN_CHAIN_Q = 256
ATTN_SLOTS = 4
ATTN_PAIR = 2
MERGE_TM = 512
LRU_BLOCK = 256
LRU_SLABS = LRU_BLOCK // LANES
LRU_ROWS = 512
LRU_SCAN_UNROLL = 4
LRU_SEG = 516
LRU_PAD_SEQ = SUBLANES * LRU_SEG
LRU_HALO = SUBLANES


def _resident(shape):
    nd = len(shape)
    return pl.BlockSpec(shape, lambda *_: (0,) * nd, pipeline_mode=pl.Buffered(1))


def _layer(arr, l, cols=None):
    shape = arr.shape[1:] if cols is None else arr.shape[1:-1] + (cols,)
    nd = len(shape)
    return pl.BlockSpec((None,) + shape, lambda *_: (l,) + (0,) * nd,
                        pipeline_mode=pl.Buffered(1))


def _params(n_grid_axes):
    return pltpu.CompilerParams(
        dimension_semantics=("arbitrary",) * n_grid_axes,
        vmem_limit_bytes=VMEM_LIMIT_BYTES,
    )


def _rms(x, g):
    ms = jnp.mean(x * x, axis=-1, keepdims=True)
    return x * lax.rsqrt(ms + EPS) * g


def _dot(a, b):
    return jnp.dot(a, b, preferred_element_type=F32)


def _cast_body(w_ref, o_ref):
    o_ref[...] = w_ref[...].astype(BF16)


def _to_bf16(w, l):
    _, rows, cols = w.shape
    return pl.pallas_call(
        _cast_body,
        grid=(rows // CAST_ROWS,),
        in_specs=[pl.BlockSpec((None, CAST_ROWS, cols), lambda i: (l, i, 0))],
        out_specs=pl.BlockSpec((CAST_ROWS, cols), lambda i: (i, 0)),
        out_shape=jax.ShapeDtypeStruct((rows, cols), BF16),
        compiler_params=_params(1),
        name="cast_bf16",
    )(w)


def _cast_block_rows(rows, steps):
    for n in range(min(steps, rows // BF16_ROWS), 0, -1):
        if rows % n == 0 and (rows // n) % BF16_ROWS == 0:
            return rows // n
    raise ValueError((rows, steps))


def _call_with_casts(body, *, steps, in_specs, out_specs, out_shape, args, casts,
                     name, scratch_shapes=()):
    n_in, n_out, n_cast = len(in_specs), len(out_specs), len(casts)
    c_in, c_out, c_shape = [], [], []
    for w, l in casts:
        _, rows, cols = w.shape
        blk = _cast_block_rows(rows, steps)
        last = rows // blk - 1
        c_in.append(pl.BlockSpec((None, blk, cols),
                                 lambda i, l=l, last=last: (l, jnp.minimum(i, last), 0)))
        c_out.append(pl.BlockSpec((blk, cols),
                                  lambda i, last=last: (jnp.minimum(i, last), 0)))
        c_shape.append(jax.ShapeDtypeStruct((rows, cols), BF16))

    def with_casts(*refs):
        ins, srcs = refs[:n_in], refs[n_in:n_in + n_cast]
        outs = refs[n_in + n_cast:n_in + n_cast + n_out]
        dsts = refs[n_in + n_cast + n_out:n_in + 2 * n_cast + n_out]
        for src, dst in zip(srcs, dsts):
            dst[...] = src[...].astype(BF16)
        body(*ins, *outs, *refs[n_in + 2 * n_cast + n_out:])

    res = pl.pallas_call(
        with_casts,
        grid=(steps,),
        in_specs=list(in_specs) + c_in,
        out_specs=list(out_specs) + c_out,
        out_shape=list(out_shape) + c_shape,
        scratch_shapes=list(scratch_shapes),
        compiler_params=_params(1),
        name=name,
    )(*args, *[w for w, _ in casts])
    return res[:n_out], res[n_out:]


def _ffn_body(x_ref, g_ref, win_ref, wout_ref, fin_ref, o_ref, *, final):
    x = x_ref[...]
    xn = _rms(x, g_ref[...]).astype(BF16)
    acc = jnp.zeros(x.shape, F32)
    for c in range(D_FF // FFN_CHUNK):
        lo = c * FFN_CHUNK
        a = _dot(xn, win_ref[:, lo:lo + FFN_CHUNK])
        b = _dot(xn, win_ref[:, D_FF + lo:D_FF + lo + FFN_CHUNK])
        h = (a * jax.nn.sigmoid(a) * b).astype(BF16)
        acc = acc + _dot(h, wout_ref[lo:lo + FFN_CHUNK, :])
    y = x + 0.5 * acc
    if final:
        y = _rms(y, fin_ref[...])
    o_ref[...] = y


def _ffn(x, g, w_in, w_out, fin, l, casts, *, final):
    tile = pl.BlockSpec((FFN_TM, D_MODEL), lambda i: (i, 0))
    (y,), cast = _call_with_casts(
        functools.partial(_ffn_body, final=final),
        steps=TOKENS // FFN_TM,
        in_specs=[tile, _layer(g, l), _resident(w_in.shape), _resident(w_out.shape),
                  _resident((1, D_MODEL))],
        out_specs=[tile],
        out_shape=[jax.ShapeDtypeStruct((TOKENS, D_MODEL), F32)],
        args=(x, g, w_in, w_out, fin),
        casts=casts,
        name="ffn_final" if final else "ffn",
    )
    return y, cast


def _lane_partner(x, d):
    lane = lax.broadcasted_iota(jnp.int32, x.shape, 1)
    up = pltpu.roll(x, LANES - d, 1)
    down = pltpu.roll(x, d, 1)
    return jnp.where((lane & d) == 0, up, down)


def _mix_in_body(x_ref, g_ref, w_ref, qkg_ref, gsum_ref, cos_ref, sin_ref,
                 vn_ref, ws_ref, gbias_ref,
                 qt_ref, k_ref, vt_ref, gmlp_ref, lr_ref):
    xn = _rms(x_ref[...], g_ref[...]).astype(BF16)

    def proj(lo, hi):
        return _dot(xn, w_ref[:, lo:hi])

    qk = proj(0, MIX_V_LO)
    sq = qk * qk
    sq_hi = sq.astype(BF16)
    sq_lo = (sq - sq_hi.astype(F32)).astype(BF16)
    ms = _dot(sq_hi, gsum_ref[...]) + _dot(sq_lo, gsum_ref[...])

    u = jax.nn.gelu(proj(MIX_GM_LO, MIX_GM_LO + GMLP_W))
    v = _rms(jax.nn.gelu(proj(MIX_GM_LO + GMLP_W, MIX_LR_LO)), vn_ref[...]).astype(BF16)
    vt_ref[0] = proj(MIX_V_LO, MIX_GM_LO).T.astype(BF16)
    lr_ref[...] = proj(MIX_LR_LO, MIX_GATE_LO)

    gbias = gbias_ref[...]
    for c in range(MIX_TM // GMLP_CHUNK):
        rows = slice(c * GMLP_CHUNK, (c + 1) * GMLP_CHUNK)
        sv = jnp.concatenate(
            [_dot(ws_ref[g], v[rows, g * GMLP_GROUP_W:(g + 1) * GMLP_GROUP_W])
             for g in range(GMLP_GROUPS)], axis=1)
        gmlp_ref[rows, :] = (u[rows] * (sv + gbias)).astype(BF16)

    qkn = qk * lax.rsqrt(ms + EPS) * qkg_ref[...]
    cos = cos_ref[...]
    sin = sin_ref[...]
    cols = []
    for c in range(QK_W // LANES):
        t = qkn[:, c * LANES:(c + 1) * LANES]
        cols.append(t * cos + _lane_partner(t, HEAD_DIM // 4) * sin)
    q = jnp.concatenate(cols[:ATTN_Q_W // LANES], axis=1) * (HEAD_DIM ** -0.5 * LOG2_E)
    qt_ref[0] = q.T.astype(BF16)
    k_ref[...] = cols[-1].astype(BF16)


def _mix_in(x, g, w, qkg, gsum, cos, sin, vn, ws, gbias, l, casts):
    seq_tiles = SEQ // MIX_TM
    row = lambda w_: pl.BlockSpec((MIX_TM, w_), lambda i: (i, 0))
    tab = pl.BlockSpec((MIX_TM, LANES), lambda i: (i % seq_tiles, 0))
    return _call_with_casts(
        _mix_in_body,
        steps=TOKENS // MIX_TM,
        in_specs=[row(D_MODEL), _layer(g, l),
                  _resident((D_MODEL, MIX_GATE_LO)),
                  _layer(qkg, l), _resident(gsum.shape), tab, tab,
                  _layer(vn, l), _resident(ws.shape), _layer(gbias, l)],
        out_specs=[
            pl.BlockSpec((1, ATTN_Q_W, MIX_TM),
                         lambda i: (i // seq_tiles, 0, i % seq_tiles)),
            row(ATTN_KV_W),
            pl.BlockSpec((1, ATTN_KV_W, MIX_TM),
                         lambda i: (i // seq_tiles, 0, i % seq_tiles)),
            row(GMLP_W), row(2 * LRU_W)],
        out_shape=[
            jax.ShapeDtypeStruct((BATCH, ATTN_Q_W, SEQ), BF16),
            jax.ShapeDtypeStruct((TOKENS, ATTN_KV_W), BF16),
            jax.ShapeDtypeStruct((BATCH, ATTN_KV_W, SEQ), BF16),
            jax.ShapeDtypeStruct((TOKENS, GMLP_W), BF16),
            jax.ShapeDtypeStruct((TOKENS, 2 * LRU_W), F32)],
        args=(x, g, w, qkg, gsum, cos, sin, vn, ws, gbias),
        casts=casts,
        name="mix_in",
    )


ATTN_SUM_ROWS = 16
ATTN_MAX_EXPONENT_SPAN = 64.0
ATTN_ROUNDING_SLACK = 1.0 + 2.0 ** -8


def _attn_body(qg_ref, qt_ref, k_ref, vt_ref, o_ref,
               vt_aug, k_aug, shift_ref, st_buf, pt_buf):
    own_lanes = (lax.broadcasted_iota(jnp.int32, (1, ATTN_KV_W), 1) // HEAD_DIM
                 == pl.program_id(1))

    @pl.when(pl.program_id(2) == 0)
    def _():
        vt_aug[0:HEAD_DIM, :] = vt_ref[0]
        vt_aug[HEAD_DIM:, :] = jnp.ones((ATTN_SUM_ROWS, SEQ), BF16)
        kb = k_ref[...]
        k_aug[:, :ATTN_KV_W] = kb
        k_aug[:, ATTN_KV_W:] = jnp.ones((SEQ, ATTN_KV_W), BF16)
        kf = kb.astype(F32)
        norm2 = jnp.sum(jnp.where(own_lanes, kf * kf, 0.0), axis=1, keepdims=True)
        kmax = jnp.sqrt(jnp.max(norm2, axis=0, keepdims=True))
        qmax = jnp.max(jnp.abs(qg_ref[:, :ATTN_Q_W]), axis=1, keepdims=True)
        shift_ref[0] = jnp.max(qmax * kmax) * (LOG2_E * ATTN_ROUNDING_SLACK)

    kv_head = lax.broadcasted_iota(jnp.int32, (ATTN_KV_W, 1), 0) // HEAD_DIM
    own_rows = kv_head == pl.program_id(1)
    qt = qt_ref[0]

    n_sub = ATTN_TQ // ATTN_CHAIN_Q
    chains = [(sub, h) for sub in range(n_sub) for h in range(ATTN_GROUP)]

    def q_weights(c):
        sub, h = chains[c]
        qh = qt[h * HEAD_DIM:(h + 1) * HEAD_DIM,
                sub * ATTN_CHAIN_Q:(sub + 1) * ATTN_CHAIN_Q]
        stacked = jnp.concatenate([qh] * ATTN_KV_HEADS, axis=0)
        return jnp.where(own_rows, stacked, jnp.zeros_like(stacked))

    def finish(outs):
        ot_all = jnp.concatenate([jnp.concatenate(o, axis=0) for o in outs], axis=1)
        o_ref[...] = ot_all.T.astype(BF16)

    shift = shift_ref[0]
    shift_is_safe = shift * 2.0 <= ATTN_MAX_EXPONENT_SPAN

    @pl.when(shift_is_safe)
    def _():
        shift_row = lax.broadcasted_iota(jnp.int32, (ATTN_KV_W, ATTN_CHAIN_Q), 0) == 0
        tail = jnp.where(shift_row, -shift, 0.0).astype(BF16)
        outs = [[None] * ATTN_GROUP for _ in range(n_sub)]
        for first in range(0, len(chains), ATTN_PAIR):
            pair = range(first, first + ATTN_PAIR)
            for c in pair:
                w = jnp.concatenate([q_weights(c), tail], axis=0)
                pt_buf[c % ATTN_SLOTS] = jnp.exp2(_dot(k_aug[...], w)).astype(BF16)
            for c in pair:
                sub, h = chains[c]
                ot = _dot(vt_aug[...], pt_buf[c % ATTN_SLOTS])
                outs[sub][h] = ot[:HEAD_DIM] / ot[HEAD_DIM:HEAD_DIM + 1]
        finish(outs)

    @pl.when(jnp.logical_not(shift_is_safe))
    def _():
        k = k_ref[...]
        col_max = {}

        def scores(c):
            st = _dot(k, q_weights(c))
            col_max[c] = jnp.max(st, axis=0, keepdims=True)
            st_buf[c % ATTN_SLOTS] = st

        for c in range(min(ATTN_SLOTS, len(chains))):
            scores(c)
        outs = [[None] * ATTN_GROUP for _ in range(n_sub)]
        for c, (sub, h) in enumerate(chains):
            slot = c % ATTN_SLOTS
            pt_buf[slot] = jnp.exp2(st_buf[slot] - col_max[c]).astype(BF16)
            ot = _dot(vt_aug[...], pt_buf[slot])
            outs[sub][h] = ot[:HEAD_DIM] / ot[HEAD_DIM:HEAD_DIM + 1]
            if c + ATTN_SLOTS < len(chains):
                scores(c + ATTN_SLOTS)
        finish(outs)


def _attention(qkg, qt, k, vt, l):
    q_tiles = SEQ // ATTN_TQ
    return pl.pallas_call(
        _attn_body,
        grid=(BATCH, ATTN_KV_HEADS, q_tiles),
        in_specs=[
            _layer(qkg, l),
            pl.BlockSpec((1, GROUP_Q_W, ATTN_TQ), lambda b, j, i: (b, j, i)),
            pl.BlockSpec((SEQ, ATTN_KV_W), lambda b, j, i: (b, 0)),
            pl.BlockSpec((1, HEAD_DIM, SEQ), lambda b, j, i: (b, j, 0)),
        ],
        out_specs=pl.BlockSpec((ATTN_TQ, GROUP_Q_W),
                               lambda b, j, i: (b * q_tiles + i, j)),
        out_shape=jax.ShapeDtypeStruct((TOKENS, ATTN_Q_W), BF16),
        scratch_shapes=[pltpu.VMEM((HEAD_DIM + ATTN_SUM_ROWS, SEQ), BF16),
                        pltpu.VMEM((SEQ, 2 * ATTN_KV_W), BF16),
                        pltpu.SMEM((1,), F32),
                        pltpu.VMEM((ATTN_SLOTS, SEQ, ATTN_CHAIN_Q), F32),
                        pltpu.VMEM((ATTN_SLOTS, SEQ, ATTN_CHAIN_Q), BF16)],
        compiler_params=_params(3),
        name="gqa_attention",
    )(qkg, qt, k, vt)


def _lru_body(lx_ref, ly_ref, cw_ref, cb_ref, w_ref, b_ref, lam_ref, o_ref,
              xpad, a_f, b_f, a_b, b_b, carry_f, carry_b):
    n_chunks = SEQ // LRU_ROWS
    halo_zeros = jnp.zeros((LRU_SLABS, LRU_HALO, LANES), F32)
    xpad[:, 0:LRU_HALO, :] = halo_zeros
    xpad[:, LRU_HALO + SEQ:LRU_HALO + SEQ + LRU_HALO, :] = halo_zeros
    for s in range(LRU_SLABS):
        xpad[s, LRU_HALO:LRU_HALO + SEQ, :] = lx_ref[:, s * LANES:(s + 1) * LANES]

    half_rate = -(0.5 * LRU_C * LOG2_E) * jax.nn.softplus(-lam_ref[...])
    cw = cw_ref[...]
    cb = cb_ref[...]
    bias = b_ref[...]
    scan_bufs = ((a_f, b_f), (a_b, b_b))

    def dense(c, carry):
        r0 = pl.multiple_of(c * LRU_ROWS, LRU_ROWS)
        slabs = []
        for s in range(LRU_SLABS):
            lanes = slice(s * LANES, (s + 1) * LANES)
            acc = cb[:, lanes]
            for j in range(CONV_W):
                off = LRU_HALO + j - CONV_W // 2
                acc = acc + cw[j:j + 1, lanes] * xpad[s, pl.ds(r0 + off, LRU_ROWS), :]
            slabs.append(acc)
        xc = jnp.concatenate(slabs, axis=1)
        pre = _dot(xc.astype(BF16), w_ref[...]) + bias
        half_xc = 0.5 * xc
        for d in range(N_DIR):
            t_r = jnp.tanh(pre[:, (2 * d) * LRU_BLOCK:(2 * d + 1) * LRU_BLOCK])
            t_i = jnp.tanh(pre[:, (2 * d + 1) * LRU_BLOCK:(2 * d + 2) * LRU_BLOCK])
            rate_d = half_rate[d:d + 1, :]
            a = jnp.exp2(t_r * rate_d + rate_d)
            g2 = 1.0 - a * a
            gain = jnp.where(g2 > 0.0, g2 * lax.rsqrt(g2), 0.0)
            b = gain * (t_i * half_xc + half_xc)
            a_buf, b_buf = scan_bufs[d]
            for s in range(LRU_SLABS):
                lanes = slice(s * LANES, (s + 1) * LANES)
                a_buf[s, pl.ds(r0, LRU_ROWS), :] = a[:, lanes]
                b_buf[s, pl.ds(r0, LRU_ROWS), :] = b[:, lanes]
        return carry

    lax.fori_loop(0, n_chunks, dense, 0)
    tail = LRU_PAD_SEQ - SEQ
    for a_buf, b_buf in scan_bufs:
        a_buf[:, SEQ:LRU_PAD_SEQ, :] = jnp.ones((LRU_SLABS, tail, LANES), F32)
        b_buf[:, SEQ:LRU_PAD_SEQ, :] = jnp.zeros((LRU_SLABS, tail, LANES), F32)

    def seg(t):
        return pl.ds(t, SUBLANES, stride=LRU_SEG)

    def scan_step(k, carry):
        out = []
        for (a_buf, b_buf), t, chains in ((scan_bufs[0], k, carry[0]),
                                          (scan_bufs[1], LRU_SEG - 1 - k, carry[1])):
            new = []
            for s, (h, p) in enumerate(chains):
                a = a_buf[s, seg(t), :]
                h = a * h + b_buf[s, seg(t), :]
                p = a * p
                b_buf[s, seg(t), :] = h
                a_buf[s, seg(t), :] = p
                new.append((h, p))
            out.append(tuple(new))
        return tuple(out)

    z = jnp.zeros((SUBLANES, LANES), F32)
    o = jnp.ones((SUBLANES, LANES), F32)
    init = tuple(tuple((z, o) for _ in range(LRU_SLABS)) for _ in range(N_DIR))
    lax.fori_loop(0, LRU_SEG, scan_step, init, unroll=LRU_SCAN_UNROLL)

    row0 = jnp.zeros((1, LANES), F32)
    for s in range(LRU_SLABS):
        carry_f[s, 0:1, :] = row0
        c = row0
        for g in range(1, SUBLANES):
            last = g * LRU_SEG - 1
            c = b_f[s, last:last + 1, :] + a_f[s, last:last + 1, :] * c
            carry_f[s, g:g + 1, :] = c
        carry_b[s, SUBLANES - 1:SUBLANES, :] = row0
        c = row0
        for g in range(SUBLANES - 2, -1, -1):
            first = (g + 1) * LRU_SEG
            c = b_b[s, first:first + 1, :] + a_b[s, first:first + 1, :] * c
            carry_b[s, g:g + 1, :] = c

    seg_shift = LRU_SEG - LRU_ROWS
    row_in_chunk = lax.broadcasted_iota(jnp.int32, (LRU_ROWS, LANES), 0)

    def emit(c, carry):
        r0 = pl.multiple_of(c * LRU_ROWS, LRU_ROWS)
        rows = pl.ds(r0, LRU_ROWS)
        prev = pl.ds(jnp.maximum(c - 1, 0), 1)
        in_prev = row_in_chunk < c * seg_shift
        hs = []
        for s in range(LRU_SLABS):
            cf = jnp.where(in_prev, carry_f[s, prev, :], carry_f[s, pl.ds(c, 1), :])
            cb_ = jnp.where(in_prev, carry_b[s, prev, :], carry_b[s, pl.ds(c, 1), :])
            hs.append((b_f[s, rows, :] + a_f[s, rows, :] * cf)
                      + (b_b[s, rows, :] + a_b[s, rows, :] * cb_))
        h = jnp.concatenate(hs, axis=1)
        o_ref[rows, :] = (h * jax.nn.gelu(ly_ref[rows, :])).astype(BF16)
        return carry

    lax.fori_loop(0, n_chunks, emit, 0)


def _lru(lr, cw, cb, w, b, lam, l):
    n_blk = LRU_W // LRU_BLOCK
    scan_buf = pltpu.VMEM((LRU_SLABS, LRU_PAD_SEQ, LANES), F32)
    carry_buf = pltpu.VMEM((LRU_SLABS, SUBLANES, LANES), F32)
    return pl.pallas_call(
        _lru_body,
        grid=(BATCH, n_blk),
        in_specs=[
            pl.BlockSpec((SEQ, LRU_BLOCK), lambda b_, c: (b_, c)),
            pl.BlockSpec((SEQ, LRU_BLOCK), lambda b_, c: (b_, n_blk + c)),
            pl.BlockSpec((None, CONV_W, LRU_BLOCK), lambda b_, c: (l, 0, c)),
            pl.BlockSpec((None, 1, LRU_BLOCK), lambda b_, c: (l, 0, c)),
            pl.BlockSpec((None, None, LRU_BLOCK, 2 * N_DIR * LRU_BLOCK),
                         lambda b_, c: (l, c, 0, 0)),
            pl.BlockSpec((None, 1, 2 * N_DIR * LRU_BLOCK), lambda b_, c: (l, 0, c)),
            pl.BlockSpec((None, N_DIR, LRU_BLOCK), lambda b_, c: (l, 0, c)),
        ],
        out_specs=pl.BlockSpec((SEQ, LRU_BLOCK), lambda b_, c: (b_, c)),
        out_shape=jax.ShapeDtypeStruct((TOKENS, LRU_W), BF16),
        scratch_shapes=[
            pltpu.VMEM((LRU_SLABS, SEQ + 2 * LRU_HALO, LANES), F32),
            scan_buf, scan_buf, scan_buf, scan_buf,
            carry_buf, carry_buf,
        ],
        compiler_params=_params(2),
        name="rg_lru",
    )(lr, lr, cw, cb, w, b, lam)


def _xkv_body(mem_ref, g_ref, wkv_ref, kt_ref, v_ref):
    mn = _rms(mem_ref[0], g_ref[...]).astype(BF16)
    kv = _dot(mn, wkv_ref[...])
    kt_ref[0] = (kv[:, :D_MODEL] * (XATTN_HEAD_DIM ** -0.5)).T.astype(BF16)
    v_ref[0] = kv[:, D_MODEL:].astype(BF16)


def _xkv(mem, g, wkv, l):
    return pl.pallas_call(
        _xkv_body,
        grid=(BATCH,),
        in_specs=[pl.BlockSpec((1, MEM_LEN, D_MODEL), lambda b: (b, 0, 0)),
                  _layer(g, l), _resident(wkv.shape)],
        out_specs=[pl.BlockSpec((1, D_MODEL, MEM_LEN), lambda b: (b, 0, 0)),
                   pl.BlockSpec((1, MEM_LEN, D_MODEL), lambda b: (b, 0, 0))],
        out_shape=[jax.ShapeDtypeStruct((BATCH, D_MODEL, MEM_LEN), BF16),
                   jax.ShapeDtypeStruct((BATCH, MEM_LEN, D_MODEL), BF16)],
        compiler_params=_params(1),
        name="xattn_kv",
    )(mem, g, wkv)


def _merge_xattn_body(x_ref, a_ref, g_ref, l_ref, mixg_ref, wmix_ref, bgate_ref,
                      wa_ref, wg_ref, wl_ref, wo_ref,
                      xg_ref, wq_ref, kt_ref, v_ref, xwo_ref, o_ref):
    x = x_ref[...]
    hn = _rms(x, mixg_ref[...]).astype(BF16)
    merged = jnp.zeros((MERGE_TM, D_MODEL), F32)
    for k, (br, w) in enumerate(((a_ref, wa_ref), (g_ref, wg_ref), (l_ref, wl_ref))):
        lo = MIX_GATE_LO + k * D_MODEL
        gate = jax.nn.sigmoid(_dot(hn, wmix_ref[:, lo:lo + D_MODEL])
                              + bgate_ref[:, k * D_MODEL:(k + 1) * D_MODEL])
        merged = merged + gate * _dot(br[...], w[...])
    x = x + _dot(merged.astype(BF16), wo_ref[...])

    q = _dot(_rms(x, xg_ref[...]).astype(BF16), wq_ref[...]).astype(BF16)
    heads = []
    for h in range(XATTN_HEADS):
        cols = slice(h * XATTN_HEAD_DIM, (h + 1) * XATTN_HEAD_DIM)
        s = _dot(q[:, cols], kt_ref[0, cols, :])
        p = jnp.exp(s - jnp.max(s, axis=-1, keepdims=True))
        l = jnp.sum(p, axis=-1, keepdims=True)
        heads.append((_dot(p.astype(BF16), v_ref[0, :, cols]) * (1.0 / l)).astype(BF16))
    o_ref[...] = x + _dot(jnp.concatenate(heads, axis=1), xwo_ref[...])


def _merge_xattn(x, attn_o, gmlp_o, lru_o, mixg, wmix, bgate, wa, wg, wl, wo,
                 xg, wq, kt, v, xwo, l, casts):
    tiles = SEQ // MERGE_TM
    row = lambda w: pl.BlockSpec((MERGE_TM, w), lambda i: (i, 0))
    (y,), cast = _call_with_casts(
        _merge_xattn_body,
        steps=TOKENS // MERGE_TM,
        in_specs=[row(D_MODEL), row(ATTN_Q_W), row(GMLP_W), row(LRU_W),
                  _layer(mixg, l), _resident(wmix.shape), _layer(bgate, l),
                  _resident(wa.shape), _resident(wg.shape), _resident(wl.shape),
                  _resident(wo.shape),
                  _layer(xg, l), _resident(wq.shape),
                  pl.BlockSpec((1, D_MODEL, MEM_LEN), lambda i: (i // tiles, 0, 0)),
                  pl.BlockSpec((1, MEM_LEN, D_MODEL), lambda i: (i // tiles, 0, 0)),
                  _resident(xwo.shape)],
        out_specs=[row(D_MODEL)],
        out_shape=[jax.ShapeDtypeStruct((TOKENS, D_MODEL), F32)],
        args=(x, attn_o, gmlp_o, lru_o, mixg, wmix, bgate, wa, wg, wl, wo,
              xg, wq, kt, v, xwo),
        casts=casts,
        name="merge_xattn",
    )
    return y, cast


def _rope_tables():
    rows = SEQ // GRID_W
    row = jnp.repeat(jnp.arange(rows), GRID_W).astype(F32)
    col = jnp.tile(jnp.arange(GRID_W), rows).astype(F32)
    n_freq = HEAD_DIM // 4
    inv_freq = ROPE_THETA ** (-jnp.arange(n_freq, dtype=F32) / n_freq)
    ang_r = row[:, None] * inv_freq[None, :]
    ang_c = col[:, None] * inv_freq[None, :]
    cos = jnp.concatenate([jnp.cos(ang_r)] * 2 + [jnp.cos(ang_c)] * 2, axis=1)
    sin = jnp.concatenate([-jnp.sin(ang_r), jnp.sin(ang_r),
                           -jnp.sin(ang_c), jnp.sin(ang_c)], axis=1)
    rep = LANES // HEAD_DIM
    return jnp.tile(cos, (1, rep)), jnp.tile(sin, (1, rep))


def _block_diag(w):
    n, a, b = w.shape[-3:]
    eye = jnp.eye(n, dtype=w.dtype)
    out = eye[:, None, :, None] * w[..., :, :, None, :]
    return out.reshape(w.shape[:-3] + (n * a, n * b))


def kernel(x, mem, ffn1_norm, ffn1_w_in, ffn1_w_out, mix_norm, w_mix_in, b_gate,
           q_norm, k_norm, attn_up, gmlp_v_norm, gmlp_ws, gmlp_bs, gmlp_up,
           lru_conv_w, lru_conv_b, lru_wa, lru_ba, lru_wi, lru_bi, lru_lambda, lru_up,
           w_mix_out, xattn_norm, mem_norm, xattn_wq, xattn_wkv, xattn_wo,
           ffn2_norm, ffn2_w_in, ffn2_w_out, final_norm):
    assert x.shape == (BATCH, SEQ, D_MODEL) and mem.shape == (BATCH, MEM_LEN, D_MODEL)
    vec = lambda a: a.reshape(DEPTH, 1, -1)
    cos, sin = _rope_tables()
    fin = final_norm.reshape(1, D_MODEL)
    gsum = _block_diag(jnp.full((QK_W // HEAD_DIM, HEAD_DIM, HEAD_DIM),
                                1.0 / HEAD_DIM, BF16))
    qkg = vec(jnp.concatenate([jnp.tile(q_norm, (1, ATTN_HEADS)),
                               jnp.tile(k_norm, (1, ATTN_KV_HEADS))], axis=1))
    gm_bias = jnp.repeat(jnp.swapaxes(gmlp_bs, 1, 2), GMLP_GROUP_W, axis=2)
    gm_ws_f32 = gmlp_ws.reshape(DEPTH, GMLP_W, GMLP_CHUNK)

    n_blk = LRU_W // LRU_BLOCK
    heads_per_blk = LRU_BLOCK // LRU_HEAD_W

    def blk(w):
        return _block_diag(
            w.reshape(DEPTH, n_blk, heads_per_blk, LRU_HEAD_W, LRU_HEAD_W))
    lru_w = (0.5 * jnp.concatenate(
        [blk(lru_wa[:, 0]), blk(lru_wi[:, 0]), blk(lru_wa[:, 1]), blk(lru_wi[:, 1])],
        axis=-1)).astype(BF16)
    lru_b = 0.5 * jnp.stack(
        [b.reshape(DEPTH, n_blk, LRU_BLOCK)
         for b in (lru_ba[:, 0], lru_bi[:, 0], lru_ba[:, 1], lru_bi[:, 1])],
        axis=2).reshape(DEPTH, 1, -1)

    mixer_f32 = (w_mix_in, xattn_wkv, attn_up, gmlp_up, lru_up, w_mix_out,
                 xattn_wq, xattn_wo, gm_ws_f32)
    ffn1_w = [_to_bf16(ffn1_w_in, 0), _to_bf16(ffn1_w_out, 0)]
    mixer_w = None

    h = x.reshape(TOKENS, D_MODEL)
    for l in range(DEPTH):
        nxt = l + 1 if l + 1 < DEPTH else None
        casts = [(ffn2_w_in, l), (ffn2_w_out, l)]
        if mixer_w is None:
            casts += [(w, l) for w in mixer_f32]
        h, cast = _ffn(h, vec(ffn1_norm), *ffn1_w, fin, l, casts, final=False)
        ffn2_w = cast[:2]
        if mixer_w is None:
            mixer_w = cast[2:]
        w_mix, w_xkv, w_attn_up, w_gmlp_up, w_lru_up, w_out, w_xq, w_xo, gm_ws = mixer_w

        (qt, k, vt, gmlp_o, lr), _ = _mix_in(
            h, vec(mix_norm), w_mix, qkg, gsum, cos, sin,
            vec(gmlp_v_norm), gm_ws.reshape(gmlp_ws.shape[1:]), gm_bias, l, [])
        attn_o = _attention(qkg, qt, k, vt, l)
        lru_o = _lru(lr, lru_conv_w, vec(lru_conv_b), lru_w, lru_b, lru_lambda, l)
        xkt, xv = _xkv(mem, vec(mem_norm), w_xkv, l)
        h, ffn1_w = _merge_xattn(
            h, attn_o, gmlp_o, lru_o, vec(mix_norm), w_mix, vec(b_gate),
            w_attn_up, w_gmlp_up, w_lru_up, w_out, vec(xattn_norm), w_xq, xkt, xv, w_xo,
            l, [] if nxt is None else [(ffn1_w_in, nxt), (ffn1_w_out, nxt)])
        h, mixer_w = _ffn(h, vec(ffn2_norm), *ffn2_w, fin, l,
                          [] if nxt is None else [(w, nxt) for w in mixer_f32],
                          final=nxt is None)
    return h.reshape(BATCH, SEQ, D_MODEL)
```

```python
import functools

import jax
import jax.numpy as jnp
from jax import lax
from jax.experimental import pallas as pl
from jax.experimental.pallas import tpu as pltpu

F32 = jnp.float32
BF16 = jnp.bfloat16

D_MODEL = 1024
BATCH = 4
SEQ = 4096
DEPTH = 2
TOKENS = BATCH * SEQ
MEM_LEN = 256
GRID_W = 64
EPS = 1e-6

ATTN_HEADS = 8
ATTN_KV_HEADS = 2
ATTN_GROUP = ATTN_HEADS // ATTN_KV_HEADS
HEAD_DIM = 64
ATTN_Q_W = ATTN_HEADS * HEAD_DIM
ATTN_KV_W = ATTN_KV_HEADS * HEAD_DIM
QK_W = ATTN_Q_W + ATTN_KV_W
GROUP_Q_W = ATTN_GROUP * HEAD_DIM
ROPE_THETA = 10000.0
LOG2_E = 1.4426950408889634

GMLP_W = 512
GMLP_GROUPS = 4
GMLP_GROUP_W = GMLP_W // GMLP_GROUPS
GMLP_CHUNK = 128

LRU_W = 512
LRU_HEADS = 8
LRU_HEAD_W = LRU_W // LRU_HEADS
CONV_W = 4
LRU_C = 8.0
N_DIR = 2
N_BRANCH = 3

XATTN_HEADS = 4
XATTN_HEAD_DIM = D_MODEL // XATTN_HEADS
D_FF = 2816

MIX_V_LO = ATTN_Q_W + ATTN_KV_W
MIX_GM_LO = MIX_V_LO + ATTN_KV_W
MIX_LR_LO = MIX_GM_LO + 2 * GMLP_W
MIX_GATE_LO = MIX_LR_LO + 2 * LRU_W

LANES = 128
SUBLANES = 8
BF16_ROWS = 16
VMEM_LIMIT_BYTES = 56 * 1024 * 1024

CAST_ROWS = 256
FFN_TM = 1024
FFN_CHUNK = 256
MIX_TM = 1024
ATTN_TQ = 512
ATTN_CHAIN_Q = 256
ATTN_SLOTS = 4
ATTN_PAIR = 2
MERGE_TM = 512
LRU_BLOCK = 256
LRU_SLABS = LRU_BLOCK // LANES
LRU_ROWS = 512
LRU_SCAN_UNROLL = 4
LRU_SEG = 516
LRU_PAD_SEQ = SUBLANES * LRU_SEG
LRU_HALO = SUBLANES


def _resident(shape):
    nd = len(shape)
    return pl.BlockSpec(shape, lambda *_: (0,) * nd, pipeline_mode=pl.Buffered(1))


def _layer(arr, l, cols=None):
    shape = arr.shape[1:] if cols is None else arr.shape[1:-1] + (cols,)
    nd = len(shape)
    return pl.BlockSpec((None,) + shape, lambda *_: (l,) + (0,) * nd,
                        pipeline_mode=pl.Buffered(1))


def _params(n_grid_axes):
    return pltpu.CompilerParams(
        dimension_semantics=("arbitrary",) * n_grid_axes,
        vmem_limit_bytes=VMEM_LIMIT_BYTES,
    )


def _rms(x, g):
    ms = jnp.mean(x * x, axis=-1, keepdims=True)
    return x * lax.rsqrt(ms + EPS) * g


def _dot(a, b):
    return jnp.dot(a, b, preferred_element_type=F32)


def _cast_body(w_ref, o_ref):
    o_ref[...] = w_ref[...].astype(BF16)


def _to_bf16(w, l):
    _, rows, cols = w.shape
    return pl.pallas_call(
        _cast_body,
        grid=(rows // CAST_ROWS,),
        in_specs=[pl.BlockSpec((None, CAST_ROWS, cols), lambda i: (l, i, 0))],
        out_specs=pl.BlockSpec((CAST_ROWS, cols), lambda i: (i, 0)),
        out_shape=jax.ShapeDtypeStruct((rows, cols), BF16),
        compiler_params=_params(1),
        name="cast_bf16",
    )(w)


def _cast_block_rows(rows, steps):
    for n in range(min(steps, rows // BF16_ROWS), 0, -1):
        if rows % n == 0 and (rows // n) % BF16_ROWS == 0:
            return rows // n
    raise ValueError((rows, steps))


def _call_with_casts(body, *, steps, in_specs, out_specs, out_shape, args, casts,
                     name, scratch_shapes=()):
    n_in, n_out, n_cast = len(in_specs), len(out_specs), len(casts)
    c_in, c_out, c_shape = [], [], []
    for w, l in casts:
        _, rows, cols = w.shape
        blk = _cast_block_rows(rows, steps)
        last = rows // blk - 1
        c_in.append(pl.BlockSpec((None, blk, cols),
                                 lambda i, l=l, last=last: (l, jnp.minimum(i, last), 0)))
        c_out.append(pl.BlockSpec((blk, cols),
                                  lambda i, last=last: (jnp.minimum(i, last), 0)))
        c_shape.append(jax.ShapeDtypeStruct((rows, cols), BF16))

    def with_casts(*refs):
        ins, srcs = refs[:n_in], refs[n_in:n_in + n_cast]
        outs = refs[n_in + n_cast:n_in + n_cast + n_out]
        dsts = refs[n_in + n_cast + n_out:n_in + 2 * n_cast + n_out]
        for src, dst in zip(srcs, dsts):
            dst[...] = src[...].astype(BF16)
        body(*ins, *outs, *refs[n_in + 2 * n_cast + n_out:])

    res = pl.pallas_call(
        with_casts,
        grid=(steps,),
        in_specs=list(in_specs) + c_in,
        out_specs=list(out_specs) + c_out,
        out_shape=list(out_shape) + c_shape,
        scratch_shapes=list(scratch_shapes),
        compiler_params=_params(1),
        name=name,
    )(*args, *[w for w, _ in casts])
    return res[:n_out], res[n_out:]


def _ffn_body(x_ref, g_ref, win_ref, wout_ref, fin_ref, o_ref, *, final):
    x = x_ref[...]
    xn = _rms(x, g_ref[...]).astype(BF16)
    acc = jnp.zeros(x.shape, F32)
    for c in range(D_FF // FFN_CHUNK):
        lo = c * FFN_CHUNK
        a = _dot(xn, win_ref[:, lo:lo + FFN_CHUNK])
        b = _dot(xn, win_ref[:, D_FF + lo:D_FF + lo + FFN_CHUNK])
        h = (a * jax.nn.sigmoid(a) * b).astype(BF16)
        acc = acc + _dot(h, wout_ref[lo:lo + FFN_CHUNK, :])
    y = x + 0.5 * acc
    if final:
        y = _rms(y, fin_ref[...])
    o_ref[...] = y


def _ffn(x, g, w_in, w_out, fin, l, casts, *, final):
    tile = pl.BlockSpec((FFN_TM, D_MODEL), lambda i: (i, 0))
    (y,), cast = _call_with_casts(
        functools.partial(_ffn_body, final=final),
        steps=TOKENS // FFN_TM,
        in_specs=[tile, _layer(g, l), _resident(w_in.shape), _resident(w_out.shape),
                  _resident((1, D_MODEL))],
        out_specs=[tile],
        out_shape=[jax.ShapeDtypeStruct((TOKENS, D_MODEL), F32)],
        args=(x, g, w_in, w_out, fin),
        casts=casts,
        name="ffn_final" if final else "ffn",
    )
    return y, cast


def _lane_partner(x, d):
    lane = lax.broadcasted_iota(jnp.int32, x.shape, 1)
    up = pltpu.roll(x, LANES - d, 1)
    down = pltpu.roll(x, d, 1)
    return jnp.where((lane & d) == 0, up, down)


def _mix_in_body(x_ref, g_ref, w_ref, qkg_ref, gsum_ref, cos_ref, sin_ref,
                 vn_ref, ws_ref, gbias_ref,
                 qt_ref, k_ref, vt_ref, gmlp_ref, lr_ref):
    xn = _rms(x_ref[...], g_ref[...]).astype(BF16)

    def proj(lo, hi):
        return _dot(xn, w_ref[:, lo:hi])

    qk = proj(0, MIX_V_LO)
    sq = qk * qk
    sq_hi = sq.astype(BF16)
    sq_lo = (sq - sq_hi.astype(F32)).astype(BF16)
    ms = _dot(sq_hi, gsum_ref[...]) + _dot(sq_lo, gsum_ref[...])

    u = jax.nn.gelu(proj(MIX_GM_LO, MIX_GM_LO + GMLP_W))
    v = _rms(jax.nn.gelu(proj(MIX_GM_LO + GMLP_W, MIX_LR_LO)), vn_ref[...]).astype(BF16)
    vt_ref[0] = proj(MIX_V_LO, MIX_GM_LO).T.astype(BF16)
    lr_ref[...] = proj(MIX_LR_LO, MIX_GATE_LO)

    gbias = gbias_ref[...]
    for c in range(MIX_TM // GMLP_CHUNK):
        rows = slice(c * GMLP_CHUNK, (c + 1) * GMLP_CHUNK)
        sv = jnp.concatenate(
            [_dot(ws_ref[g], v[rows, g * GMLP_GROUP_W:(g + 1) * GMLP_GROUP_W])
             for g in range(GMLP_GROUPS)], axis=1)
        gmlp_ref[rows, :] = (u[rows] * (sv + gbias)).astype(BF16)

    qkn = qk * lax.rsqrt(ms + EPS) * qkg_ref[...]
    cos = cos_ref[...]
    sin = sin_ref[...]
    cols = []
    for c in range(QK_W // LANES):
        t = qkn[:, c * LANES:(c + 1) * LANES]
        cols.append(t * cos + _lane_partner(t, HEAD_DIM // 4) * sin)
    q = jnp.concatenate(cols[:ATTN_Q_W // LANES], axis=1) * (HEAD_DIM ** -0.5 * LOG2_E)
    for u in range(MIX_TM // ATTN_TQ):
        qt_ref[0, u] = q[u * ATTN_TQ:(u + 1) * ATTN_TQ].T.astype(BF16)
    k_ref[...] = cols[-1].astype(BF16)


def _mix_in(x, g, w, qkg, gsum, cos, sin, vn, ws, gbias, l, casts):
    seq_tiles = SEQ // MIX_TM
    row = lambda w_: pl.BlockSpec((MIX_TM, w_), lambda i: (i, 0))
    tab = pl.BlockSpec((MIX_TM, LANES), lambda i: (i % seq_tiles, 0))
    return _call_with_casts(
        _mix_in_body,
        steps=TOKENS // MIX_TM,
        in_specs=[row(D_MODEL), _layer(g, l),
                  _resident((D_MODEL, MIX_GATE_LO)),
                  _layer(qkg, l), _resident(gsum.shape), tab, tab,
                  _layer(vn, l), _resident(ws.shape), _layer(gbias, l)],
        out_specs=[
            pl.BlockSpec((1, MIX_TM // ATTN_TQ, ATTN_Q_W, ATTN_TQ),
                         lambda i: (i // seq_tiles, i % seq_tiles, 0, 0)),
            row(ATTN_KV_W),
            pl.BlockSpec((1, ATTN_KV_W, MIX_TM),
                         lambda i: (i // seq_tiles, 0, i % seq_tiles)),
            row(GMLP_W), row(2 * LRU_W)],
        out_shape=[
            jax.ShapeDtypeStruct((BATCH, SEQ // ATTN_TQ, ATTN_Q_W, ATTN_TQ), BF16),
            jax.ShapeDtypeStruct((TOKENS, ATTN_KV_W), BF16),
            jax.ShapeDtypeStruct((BATCH, ATTN_KV_W, SEQ), BF16),
            jax.ShapeDtypeStruct((TOKENS, GMLP_W), BF16),
            jax.ShapeDtypeStruct((TOKENS, 2 * LRU_W), F32)],
        args=(x, g, w, qkg, gsum, cos, sin, vn, ws, gbias),
        casts=casts,
        name="mix_in",
    )


ATTN_SUM_ROWS = 16
ATTN_MAX_EXPONENT_SPAN = 64.0
ATTN_ROUNDING_SLACK = 1.0 + 2.0 ** -8


def _attn_body(qg_ref, qt_ref, k_ref, vt_ref, o_ref, vt_aug, k_aug, st_buf, pt_buf):
    own_lanes = (lax.broadcasted_iota(jnp.int32, (1, ATTN_KV_W), 1) // HEAD_DIM
                 == pl.program_id(1))
    vt_aug[0:HEAD_DIM, :] = vt_ref[0]
    vt_aug[HEAD_DIM:, :] = jnp.ones((ATTN_SUM_ROWS, SEQ), BF16)
    kb = k_ref[...]
    k_aug[:, :ATTN_KV_W] = kb
    k_aug[:, ATTN_KV_W:] = jnp.ones((SEQ, ATTN_KV_W), BF16)

    kf = kb.astype(F32)
    norm2 = jnp.sum(jnp.where(own_lanes, kf * kf, 0.0), axis=1, keepdims=True)
    kmax = jnp.sqrt(jnp.max(norm2, axis=0, keepdims=True))
    qmax = jnp.max(jnp.abs(qg_ref[:, :ATTN_Q_W]), axis=1, keepdims=True)
    shift = jnp.max(qmax * kmax) * (LOG2_E * ATTN_ROUNDING_SLACK)

    kv_head = lax.broadcasted_iota(jnp.int32, (ATTN_KV_W, 1), 0) // HEAD_DIM
    own_rows = kv_head == pl.program_id(1)

    n_sub = ATTN_TQ // ATTN_CHAIN_Q
    chains = [(sub, h) for sub in range(n_sub) for h in range(ATTN_GROUP)]

    def q_weights(qt, c):
        sub, h = chains[c]
        qh = qt[h * HEAD_DIM:(h + 1) * HEAD_DIM,
                sub * ATTN_CHAIN_Q:(sub + 1) * ATTN_CHAIN_Q]
        stacked = jnp.concatenate([qh] * ATTN_KV_HEADS, axis=0)
        return jnp.where(own_rows, stacked, jnp.zeros_like(stacked))

    def finish(i, outs):
        ot_all = jnp.concatenate([jnp.concatenate(o, axis=0) for o in outs], axis=1)
        rows = pl.ds(pl.multiple_of(i * ATTN_TQ, ATTN_TQ), ATTN_TQ)
        o_ref[rows, :] = ot_all.T.astype(BF16)

    def shifted_tile(i, carry):
        qt = qt_ref[0, i]
        shift_row = lax.broadcasted_iota(jnp.int32, (ATTN_KV_W, ATTN_CHAIN_Q), 0) == 0
        tail = jnp.where(shift_row, -shift, 0.0).astype(BF16)
        outs = [[None] * ATTN_GROUP for _ in range(n_sub)]
        for first in range(0, len(chains), ATTN_PAIR):
            pair = range(first, first + ATTN_PAIR)
            for c in pair:
                w = jnp.concatenate([q_weights(qt, c), tail], axis=0)
                pt_buf[c % ATTN_SLOTS] = jnp.exp2(_dot(k_aug[...], w)).astype(BF16)
            for c in pair:
                sub, h = chains[c]
                ot = _dot(vt_aug[...], pt_buf[c % ATTN_SLOTS])
                outs[sub][h] = ot[:HEAD_DIM] / ot[HEAD_DIM:HEAD_DIM + 1]
        finish(i, outs)
        return carry

    def exact_tile(i, carry):
        qt = qt_ref[0, i]
        col_max = {}

        def scores(c):
            st = _dot(kb, q_weights(qt, c))
            col_max[c] = jnp.max(st, axis=0, keepdims=True)
            st_buf[c % ATTN_SLOTS] = st

        for c in range(min(ATTN_SLOTS, len(chains))):
            scores(c)
        outs = [[None] * ATTN_GROUP for _ in range(n_sub)]
        for c, (sub, h) in enumerate(chains):
            slot = c % ATTN_SLOTS
            pt_buf[slot] = jnp.exp2(st_buf[slot] - col_max[c]).astype(BF16)
            ot = _dot(vt_aug[...], pt_buf[slot])
            outs[sub][h] = ot[:HEAD_DIM] / ot[HEAD_DIM:HEAD_DIM + 1]
            if c + ATTN_SLOTS < len(chains):
                scores(c + ATTN_SLOTS)
        finish(i, outs)
        return carry

    shift_is_safe = shift * 2.0 <= ATTN_MAX_EXPONENT_SPAN
    q_tiles = SEQ // ATTN_TQ

    @pl.when(shift_is_safe)
    def _():
        lax.fori_loop(0, q_tiles, shifted_tile, 0)

    @pl.when(jnp.logical_not(shift_is_safe))
    def _():
        lax.fori_loop(0, q_tiles, exact_tile, 0)


def _attention(qkg, qt, k, vt, l):
    q_tiles = SEQ // ATTN_TQ
    return pl.pallas_call(
        _attn_body,
        grid=(BATCH, ATTN_KV_HEADS),
        in_specs=[
            _layer(qkg, l),
            pl.BlockSpec((1, q_tiles, GROUP_Q_W, ATTN_TQ), lambda b, j: (b, 0, j, 0)),
            pl.BlockSpec((SEQ, ATTN_KV_W), lambda b, j: (b, 0)),
            pl.BlockSpec((1, HEAD_DIM, SEQ), lambda b, j: (b, j, 0)),
        ],
        out_specs=pl.BlockSpec((SEQ, GROUP_Q_W), lambda b, j: (b, j)),
        out_shape=jax.ShapeDtypeStruct((TOKENS, ATTN_Q_W), BF16),
        scratch_shapes=[pltpu.VMEM((HEAD_DIM + ATTN_SUM_ROWS, SEQ), BF16),
                        pltpu.VMEM((SEQ, 2 * ATTN_KV_W), BF16),
                        pltpu.VMEM((ATTN_SLOTS, SEQ, ATTN_CHAIN_Q), F32),
                        pltpu.VMEM((ATTN_SLOTS, SEQ, ATTN_CHAIN_Q), BF16)],
        compiler_params=_params(2),
        name="gqa_attention",
    )(qkg, qt, k, vt)


def _lru_body(lx_ref, ly_ref, cw_ref, cb_ref, w_ref, b_ref, lam_ref, o_ref,
              xpad, a_f, b_f, a_b, b_b, carry_f, carry_b):
    n_chunks = SEQ // LRU_ROWS
    halo_zeros = jnp.zeros((LRU_SLABS, LRU_HALO, LANES), F32)
    xpad[:, 0:LRU_HALO, :] = halo_zeros
    xpad[:, LRU_HALO + SEQ:LRU_HALO + SEQ + LRU_HALO, :] = halo_zeros
    for s in range(LRU_SLABS):
        xpad[s, LRU_HALO:LRU_HALO + SEQ, :] = lx_ref[:, s * LANES:(s + 1) * LANES]

    half_rate = -(0.5 * LRU_C * LOG2_E) * jax.nn.softplus(-lam_ref[...])
    cw = cw_ref[...]
    cb = cb_ref[...]
    bias = b_ref[...]
    scan_bufs = ((a_f, b_f), (a_b, b_b))

    def dense(c, carry):
        r0 = pl.multiple_of(c * LRU_ROWS, LRU_ROWS)
        slabs = []
        for s in range(LRU_SLABS):
            lanes = slice(s * LANES, (s + 1) * LANES)
            acc = cb[:, lanes]
            for j in range(CONV_W):
                off = LRU_HALO + j - CONV_W // 2
                acc = acc + cw[j:j + 1, lanes] * xpad[s, pl.ds(r0 + off, LRU_ROWS), :]
            slabs.append(acc)
        xc = jnp.concatenate(slabs, axis=1)
        pre = _dot(xc.astype(BF16), w_ref[...]) + bias
        half_xc = 0.5 * xc
        for d in range(N_DIR):
            t_r = jnp.tanh(pre[:, (2 * d) * LRU_BLOCK:(2 * d + 1) * LRU_BLOCK])
            t_i = jnp.tanh(pre[:, (2 * d + 1) * LRU_BLOCK:(2 * d + 2) * LRU_BLOCK])
            rate_d = half_rate[d:d + 1, :]
            a = jnp.exp2(t_r * rate_d + rate_d)
            g2 = 1.0 - a * a
            gain = jnp.where(g2 > 0.0, g2 * lax.rsqrt(g2), 0.0)
            b = gain * (t_i * half_xc + half_xc)
            a_buf, b_buf = scan_bufs[d]
            for s in range(LRU_SLABS):
                lanes = slice(s * LANES, (s + 1) * LANES)
                a_buf[s, pl.ds(r0, LRU_ROWS), :] = a[:, lanes]
                b_buf[s, pl.ds(r0, LRU_ROWS), :] = b[:, lanes]
        return carry

    lax.fori_loop(0, n_chunks, dense, 0)
    tail = LRU_PAD_SEQ - SEQ
    for a_buf, b_buf in scan_bufs:
        a_buf[:, SEQ:LRU_PAD_SEQ, :] = jnp.ones((LRU_SLABS, tail, LANES), F32)
        b_buf[:, SEQ:LRU_PAD_SEQ, :] = jnp.zeros((LRU_SLABS, tail, LANES), F32)

    def seg(t):
        return pl.ds(t, SUBLANES, stride=LRU_SEG)

    def scan_step(k, carry):
        out = []
        for (a_buf, b_buf), t, chains in ((scan_bufs[0], k, carry[0]),
                                          (scan_bufs[1], LRU_SEG - 1 - k, carry[1])):
            new = []
            for s, (h, p) in enumerate(chains):
                a = a_buf[s, seg(t), :]
                h = a * h + b_buf[s, seg(t), :]
                p = a * p
                b_buf[s, seg(t), :] = h
                a_buf[s, seg(t), :] = p
                new.append((h, p))
            out.append(tuple(new))
        return tuple(out)

    z = jnp.zeros((SUBLANES, LANES), F32)
    o = jnp.ones((SUBLANES, LANES), F32)
    init = tuple(tuple((z, o) for _ in range(LRU_SLABS)) for _ in range(N_DIR))
    lax.fori_loop(0, LRU_SEG, scan_step, init, unroll=LRU_SCAN_UNROLL)

    row0 = jnp.zeros((1, LANES), F32)
    for s in range(LRU_SLABS):
        carry_f[s, 0:1, :] = row0
        c = row0
        for g in range(1, SUBLANES):
            last = g * LRU_SEG - 1
            c = b_f[s, last:last + 1, :] + a_f[s, last:last + 1, :] * c
            carry_f[s, g:g + 1, :] = c
        carry_b[s, SUBLANES - 1:SUBLANES, :] = row0
        c = row0
        for g in range(SUBLANES - 2, -1, -1):
            first = (g + 1) * LRU_SEG
            c = b_b[s, first:first + 1, :] + a_b[s, first:first + 1, :] * c
            carry_b[s, g:g + 1, :] = c

    seg_shift = LRU_SEG - LRU_ROWS
    row_in_chunk = lax.broadcasted_iota(jnp.int32, (LRU_ROWS, LANES), 0)

    def emit(c, carry):
        r0 = pl.multiple_of(c * LRU_ROWS, LRU_ROWS)
        rows = pl.ds(r0, LRU_ROWS)
        prev = pl.ds(jnp.maximum(c - 1, 0), 1)
        in_prev = row_in_chunk < c * seg_shift
        hs = []
        for s in range(LRU_SLABS):
            cf = jnp.where(in_prev, carry_f[s, prev, :], carry_f[s, pl.ds(c, 1), :])
            cb_ = jnp.where(in_prev, carry_b[s, prev, :], carry_b[s, pl.ds(c, 1), :])
            hs.append((b_f[s, rows, :] + a_f[s, rows, :] * cf)
                      + (b_b[s, rows, :] + a_b[s, rows, :] * cb_))
        h = jnp.concatenate(hs, axis=1)
        o_ref[rows, :] = (h * jax.nn.gelu(ly_ref[rows, :])).astype(BF16)
        return carry

    lax.fori_loop(0, n_chunks, emit, 0)


def _lru(lr, cw, cb, w, b, lam, l):
    n_blk = LRU_W // LRU_BLOCK
    scan_buf = pltpu.VMEM((LRU_SLABS, LRU_PAD_SEQ, LANES), F32)
    carry_buf = pltpu.VMEM((LRU_SLABS, SUBLANES, LANES), F32)
    return pl.pallas_call(
        _lru_body,
        grid=(BATCH, n_blk),
        in_specs=[
            pl.BlockSpec((SEQ, LRU_BLOCK), lambda b_, c: (b_, c)),
            pl.BlockSpec((SEQ, LRU_BLOCK), lambda b_, c: (b_, n_blk + c)),
            pl.BlockSpec((None, CONV_W, LRU_BLOCK), lambda b_, c: (l, 0, c)),
            pl.BlockSpec((None, 1, LRU_BLOCK), lambda b_, c: (l, 0, c)),
            pl.BlockSpec((None, None, LRU_BLOCK, 2 * N_DIR * LRU_BLOCK),
                         lambda b_, c: (l, c, 0, 0)),
            pl.BlockSpec((None, 1, 2 * N_DIR * LRU_BLOCK), lambda b_, c: (l, 0, c)),
            pl.BlockSpec((None, N_DIR, LRU_BLOCK), lambda b_, c: (l, 0, c)),
        ],
        out_specs=pl.BlockSpec((SEQ, LRU_BLOCK), lambda b_, c: (b_, c)),
        out_shape=jax.ShapeDtypeStruct((TOKENS, LRU_W), BF16),
        scratch_shapes=[
            pltpu.VMEM((LRU_SLABS, SEQ + 2 * LRU_HALO, LANES), F32),
            scan_buf, scan_buf, scan_buf, scan_buf,
            carry_buf, carry_buf,
        ],
        compiler_params=_params(2),
        name="rg_lru",
    )(lr, lr, cw, cb, w, b, lam)


def _xkv_body(mem_ref, g_ref, wkv_ref, kt_ref, v_ref):
    mn = _rms(mem_ref[0], g_ref[...]).astype(BF16)
    kv = _dot(mn, wkv_ref[...])
    kt_ref[0] = (kv[:, :D_MODEL] * (XATTN_HEAD_DIM ** -0.5)).T.astype(BF16)
    v_ref[0] = kv[:, D_MODEL:].astype(BF16)


def _xkv(mem, g, wkv, l):
    return pl.pallas_call(
        _xkv_body,
        grid=(BATCH,),
        in_specs=[pl.BlockSpec((1, MEM_LEN, D_MODEL), lambda b: (b, 0, 0)),
                  _layer(g, l), _resident(wkv.shape)],
        out_specs=[pl.BlockSpec((1, D_MODEL, MEM_LEN), lambda b: (b, 0, 0)),
                   pl.BlockSpec((1, MEM_LEN, D_MODEL), lambda b: (b, 0, 0))],
        out_shape=[jax.ShapeDtypeStruct((BATCH, D_MODEL, MEM_LEN), BF16),
                   jax.ShapeDtypeStruct((BATCH, MEM_LEN, D_MODEL), BF16)],
        compiler_params=_params(1),
        name="xattn_kv",
    )(mem, g, wkv)


def _merge_xattn_body(x_ref, a_ref, g_ref, l_ref, mixg_ref, wmix_ref, bgate_ref,
                      wa_ref, wg_ref, wl_ref, wo_ref,
                      xg_ref, wq_ref, kt_ref, v_ref, xwo_ref, o_ref):
    x = x_ref[...]
    hn = _rms(x, mixg_ref[...]).astype(BF16)
    merged = jnp.zeros((MERGE_TM, D_MODEL), F32)
    for k, (br, w) in enumerate(((a_ref, wa_ref), (g_ref, wg_ref), (l_ref, wl_ref))):
        lo = MIX_GATE_LO + k * D_MODEL
        gate = jax.nn.sigmoid(_dot(hn, wmix_ref[:, lo:lo + D_MODEL])
                              + bgate_ref[:, k * D_MODEL:(k + 1) * D_MODEL])
        merged = merged + gate * _dot(br[...], w[...])
    x = x + _dot(merged.astype(BF16), wo_ref[...])

    q = _dot(_rms(x, xg_ref[...]).astype(BF16), wq_ref[...]).astype(BF16)
    heads = []
    for h in range(XATTN_HEADS):
        cols = slice(h * XATTN_HEAD_DIM, (h + 1) * XATTN_HEAD_DIM)
        s = _dot(q[:, cols], kt_ref[0, cols, :])
        p = jnp.exp(s - jnp.max(s, axis=-1, keepdims=True))
        l = jnp.sum(p, axis=-1, keepdims=True)
        heads.append((_dot(p.astype(BF16), v_ref[0, :, cols]) * (1.0 / l)).astype(BF16))
    o_ref[...] = x + _dot(jnp.concatenate(heads, axis=1), xwo_ref[...])


def _merge_xattn(x, attn_o, gmlp_o, lru_o, mixg, wmix, bgate, wa, wg, wl, wo,
                 xg, wq, kt, v, xwo, l, casts):
    tiles = SEQ // MERGE_TM
    row = lambda w: pl.BlockSpec((MERGE_TM, w), lambda i: (i, 0))
    (y,), cast = _call_with_casts(
        _merge_xattn_body,
        steps=TOKENS // MERGE_TM,
        in_specs=[row(D_MODEL), row(ATTN_Q_W), row(GMLP_W), row(LRU_W),
                  _layer(mixg, l), _resident(wmix.shape), _layer(bgate, l),
                  _resident(wa.shape), _resident(wg.shape), _resident(wl.shape),
                  _resident(wo.shape),
                  _layer(xg, l), _resident(wq.shape),
                  pl.BlockSpec((1, D_MODEL, MEM_LEN), lambda i: (i // tiles, 0, 0)),
                  pl.BlockSpec((1, MEM_LEN, D_MODEL), lambda i: (i // tiles, 0, 0)),
                  _resident(xwo.shape)],
        out_specs=[row(D_MODEL)],
        out_shape=[jax.ShapeDtypeStruct((TOKENS, D_MODEL), F32)],
        args=(x, attn_o, gmlp_o, lru_o, mixg, wmix, bgate, wa, wg, wl, wo,
              xg, wq, kt, v, xwo),
        casts=casts,
        name="merge_xattn",
    )
    return y, cast


def _rope_tables():
    rows = SEQ // GRID_W
    row = jnp.repeat(jnp.arange(rows), GRID_W).astype(F32)
    col = jnp.tile(jnp.arange(GRID_W), rows).astype(F32)
    n_freq = HEAD_DIM // 4
    inv_freq = ROPE_THETA ** (-jnp.arange(n_freq, dtype=F32) / n_freq)
    ang_r = row[:, None] * inv_freq[None, :]
    ang_c = col[:, None] * inv_freq[None, :]
    cos = jnp.concatenate([jnp.cos(ang_r)] * 2 + [jnp.cos(ang_c)] * 2, axis=1)
    sin = jnp.concatenate([-jnp.sin(ang_r), jnp.sin(ang_r),
                           -jnp.sin(ang_c), jnp.sin(ang_c)], axis=1)
    rep = LANES // HEAD_DIM
    return jnp.tile(cos, (1, rep)), jnp.tile(sin, (1, rep))


def _block_diag(w):
    n, a, b = w.shape[-3:]
    eye = jnp.eye(n, dtype=w.dtype)
    out = eye[:, None, :, None] * w[..., :, :, None, :]
    return out.reshape(w.shape[:-3] + (n * a, n * b))


def kernel(x, mem, ffn1_norm, ffn1_w_in, ffn1_w_out, mix_norm, w_mix_in, b_gate,
           q_norm, k_norm, attn_up, gmlp_v_norm, gmlp_ws, gmlp_bs, gmlp_up,
           lru_conv_w, lru_conv_b, lru_wa, lru_ba, lru_wi, lru_bi, lru_lambda, lru_up,
           w_mix_out, xattn_norm, mem_norm, xattn_wq, xattn_wkv, xattn_wo,
           ffn2_norm, ffn2_w_in, ffn2_w_out, final_norm):
    assert x.shape == (BATCH, SEQ, D_MODEL) and mem.shape == (BATCH, MEM_LEN, D_MODEL)
    vec = lambda a: a.reshape(DEPTH, 1, -1)
    cos, sin = _rope_tables()
    fin = final_norm.reshape(1, D_MODEL)
    gsum = _block_diag(jnp.full((QK_W // HEAD_DIM, HEAD_DIM, HEAD_DIM),
                                1.0 / HEAD_DIM, BF16))
    qkg = vec(jnp.concatenate([jnp.tile(q_norm, (1, ATTN_HEADS)),
                               jnp.tile(k_norm, (1, ATTN_KV_HEADS))], axis=1))
    gm_bias = jnp.repeat(jnp.swapaxes(gmlp_bs, 1, 2), GMLP_GROUP_W, axis=2)
    gm_ws_f32 = gmlp_ws.reshape(DEPTH, GMLP_W, GMLP_CHUNK)

    n_blk = LRU_W // LRU_BLOCK
    heads_per_blk = LRU_BLOCK // LRU_HEAD_W

    def blk(w):
        return _block_diag(
            w.reshape(DEPTH, n_blk, heads_per_blk, LRU_HEAD_W, LRU_HEAD_W))
    lru_w = (0.5 * jnp.concatenate(
        [blk(lru_wa[:, 0]), blk(lru_wi[:, 0]), blk(lru_wa[:, 1]), blk(lru_wi[:, 1])],
        axis=-1)).astype(BF16)
    lru_b = 0.5 * jnp.stack(
        [b.reshape(DEPTH, n_blk, LRU_BLOCK)
         for b in (lru_ba[:, 0], lru_bi[:, 0], lru_ba[:, 1], lru_bi[:, 1])],
        axis=2).reshape(DEPTH, 1, -1)

    mixer_f32 = (w_mix_in, xattn_wkv, attn_up, gmlp_up, lru_up, w_mix_out,
                 xattn_wq, xattn_wo, gm_ws_f32)
    ffn1_w = [_to_bf16(ffn1_w_in, 0), _to_bf16(ffn1_w_out, 0)]
    mixer_w = None

    h = x.reshape(TOKENS, D_MODEL)
    for l in range(DEPTH):
        nxt = l + 1 if l + 1 < DEPTH else None
        casts = [(ffn2_w_in, l), (ffn2_w_out, l)]
        if mixer_w is None:
            casts += [(w, l) for w in mixer_f32]
        h, cast = _ffn(h, vec(ffn1_norm), *ffn1_w, fin, l, casts, final=False)
        ffn2_w = cast[:2]
        if mixer_w is None:
            mixer_w = cast[2:]
        w_mix, w_xkv, w_attn_up, w_gmlp_up, w_lru_up, w_out, w_xq, w_xo, gm_ws = mixer_w

        (qt, k, vt, gmlp_o, lr), _ = _mix_in(
            h, vec(mix_norm), w_mix, qkg, gsum, cos, sin,
            vec(gmlp_v_norm), gm_ws.reshape(gmlp_ws.shape[1:]), gm_bias, l, [])
        attn_o = _attention(qkg, qt, k, vt, l)
        lru_o = _lru(lr, lru_conv_w, vec(lru_conv_b), lru_w, lru_b, lru_lambda, l)
        xkt, xv = _xkv(mem, vec(mem_norm), w_xkv, l)
        h, ffn1_w = _merge_xattn(
            h, attn_o, gmlp_o, lru_o, vec(mix_norm), w_mix, vec(b_gate),
            w_attn_up, w_gmlp_up, w_lru_up, w_out, vec(xattn_norm), w_xq, xkt, xv, w_xo,
            l, [] if nxt is None else [(ffn1_w_in, nxt), (ffn1_w_out, nxt)])
        h, mixer_w = _ffn(h, vec(ffn2_norm), *ffn2_w, fin, l,
                          [] if nxt is None else [(w, nxt) for w in mixer_f32],
                          final=nxt is None)
    return h.reshape(BATCH, SEQ, D_MODEL)
```

```python
import functools

import jax
import jax.numpy as jnp
from jax import lax
from jax.experimental import pallas as pl
from jax.experimental.pallas import tpu as pltpu

F32 = jnp.float32
BF16 = jnp.bfloat16

D_MODEL = 1024
BATCH = 4
SEQ = 4096
DEPTH = 2
TOKENS = BATCH * SEQ
MEM_LEN = 256
GRID_W = 64
EPS = 1e-6

ATTN_HEADS = 8
ATTN_KV_HEADS = 2
ATTN_GROUP = ATTN_HEADS // ATTN_KV_HEADS
HEAD_DIM = 64
ATTN_Q_W = ATTN_HEADS * HEAD_DIM
ATTN_KV_W = ATTN_KV_HEADS * HEAD_DIM
QK_W = ATTN_Q_W + ATTN_KV_W
GROUP_Q_W = ATTN_GROUP * HEAD_DIM
ROPE_THETA = 10000.0
LOG2_E = 1.4426950408889634

GMLP_W = 512
GMLP_GROUPS = 4
GMLP_GROUP_W = GMLP_W // GMLP_GROUPS
GMLP_CHUNK = 128

LRU_W = 512
LRU_HEADS = 8
LRU_HEAD_W = LRU_W // LRU_HEADS
CONV_W = 4
LRU_C = 8.0
N_DIR = 2
N_BRANCH = 3

XATTN_HEADS = 4
XATTN_HEAD_DIM = D_MODEL // XATTN_HEADS
D_FF = 2816

MIX_V_LO = ATTN_Q_W + ATTN_KV_W
MIX_GM_LO = MIX_V_LO + ATTN_KV_W
MIX_LR_LO = MIX_GM_LO + 2 * GMLP_W
MIX_GATE_LO = MIX_LR_LO + 2 * LRU_W

LANES = 128
SUBLANES = 8
BF16_ROWS = 16
VMEM_LIMIT_BYTES = 56 * 1024 * 1024

CAST_ROWS = 256
FFN_TM = 1024
FFN_CHUNK = 256
MIX_TM = 1024
ATTN_TQ = 512
ATTN_CHAIN_Q = 256
ATTN_SLOTS = 4
ATTN_PAIR = 2
MERGE_TM = 512
LRU_BLOCK = 256
LRU_SLABS = LRU_BLOCK // LANES
LRU_ROWS = 512
LRU_SCAN_UNROLL = 4
LRU_SEG = 516
LRU_PAD_SEQ = SUBLANES * LRU_SEG
LRU_HALO = SUBLANES


def _resident(shape):
    nd = len(shape)
    return pl.BlockSpec(shape, lambda *_: (0,) * nd, pipeline_mode=pl.Buffered(1))


def _layer(arr, l, cols=None):
    shape = arr.shape[1:] if cols is None else arr.shape[1:-1] + (cols,)
    nd = len(shape)
    return pl.BlockSpec((None,) + shape, lambda *_: (l,) + (0,) * nd,
                        pipeline_mode=pl.Buffered(1))


def _params(n_grid_axes):
    return pltpu.CompilerParams(
        dimension_semantics=("arbitrary",) * n_grid_axes,
        vmem_limit_bytes=VMEM_LIMIT_BYTES,
    )


def _rms(x, g):
    ms = jnp.mean(x * x, axis=-1, keepdims=True)
    return x * lax.rsqrt(ms + EPS) * g


def _dot(a, b):
    return jnp.dot(a, b, preferred_element_type=F32)


def _cast_body(w_ref, o_ref):
    o_ref[...] = w_ref[...].astype(BF16)


def _to_bf16(w, l):
    _, rows, cols = w.shape
    return pl.pallas_call(
        _cast_body,
        grid=(rows // CAST_ROWS,),
        in_specs=[pl.BlockSpec((None, CAST_ROWS, cols), lambda i: (l, i, 0))],
        out_specs=pl.BlockSpec((CAST_ROWS, cols), lambda i: (i, 0)),
        out_shape=jax.ShapeDtypeStruct((rows, cols), BF16),
        compiler_params=_params(1),
        name="cast_bf16",
    )(w)


def _cast_block_rows(rows, steps):
    for n in range(min(steps, rows // BF16_ROWS), 0, -1):
        if rows % n == 0 and (rows // n) % BF16_ROWS == 0:
            return rows // n
    raise ValueError((rows, steps))


def _call_with_casts(body, *, steps, in_specs, out_specs, out_shape, args, casts,
                     name, scratch_shapes=()):
    n_in, n_out, n_cast = len(in_specs), len(out_specs), len(casts)
    c_in, c_out, c_shape = [], [], []
    for w, l in casts:
        _, rows, cols = w.shape
        blk = _cast_block_rows(rows, steps)
        last = rows // blk - 1
        c_in.append(pl.BlockSpec((None, blk, cols),
                                 lambda i, l=l, last=last: (l, jnp.minimum(i, last), 0)))
        c_out.append(pl.BlockSpec((blk, cols),
                                  lambda i, last=last: (jnp.minimum(i, last), 0)))
        c_shape.append(jax.ShapeDtypeStruct((rows, cols), BF16))

    def with_casts(*refs):
        ins, srcs = refs[:n_in], refs[n_in:n_in + n_cast]
        outs = refs[n_in + n_cast:n_in + n_cast + n_out]
        dsts = refs[n_in + n_cast + n_out:n_in + 2 * n_cast + n_out]
        for src, dst in zip(srcs, dsts):
            dst[...] = src[...].astype(BF16)
        body(*ins, *outs, *refs[n_in + 2 * n_cast + n_out:])

    res = pl.pallas_call(
        with_casts,
        grid=(steps,),
        in_specs=list(in_specs) + c_in,
        out_specs=list(out_specs) + c_out,
        out_shape=list(out_shape) + c_shape,
        scratch_shapes=list(scratch_shapes),
        compiler_params=_params(1),
        name=name,
    )(*args, *[w for w, _ in casts])
    return res[:n_out], res[n_out:]


def _ffn_body(x_ref, g_ref, win_ref, wout_ref, fin_ref, o_ref, *, final):
    x = x_ref[...]
    xn = _rms(x, g_ref[...]).astype(BF16)
    acc = jnp.zeros(x.shape, F32)
    for c in range(D_FF // FFN_CHUNK):
        lo = c * FFN_CHUNK
        a = _dot(xn, win_ref[:, lo:lo + FFN_CHUNK])
        b = _dot(xn, win_ref[:, D_FF + lo:D_FF + lo + FFN_CHUNK])
        h = (a * jax.nn.sigmoid(a) * b).astype(BF16)
        acc = acc + _dot(h, wout_ref[lo:lo + FFN_CHUNK, :])
    y = x + 0.5 * acc
    if final:
        y = _rms(y, fin_ref[...])
    o_ref[...] = y


def _ffn(x, g, w_in, w_out, fin, l, casts, *, final):
    tile = pl.BlockSpec((FFN_TM, D_MODEL), lambda i: (i, 0))
    (y,), cast = _call_with_casts(
        functools.partial(_ffn_body, final=final),
        steps=TOKENS // FFN_TM,
        in_specs=[tile, _layer(g, l), _resident(w_in.shape), _resident(w_out.shape),
                  _resident((1, D_MODEL))],
        out_specs=[tile],
        out_shape=[jax.ShapeDtypeStruct((TOKENS, D_MODEL), F32)],
        args=(x, g, w_in, w_out, fin),
        casts=casts,
        name="ffn_final" if final else "ffn",
    )
    return y, cast


def _lane_partner(x, d):
    lane = lax.broadcasted_iota(jnp.int32, x.shape, 1)
    up = pltpu.roll(x, LANES - d, 1)
    down = pltpu.roll(x, d, 1)
    return jnp.where((lane & d) == 0, up, down)


def _mix_in_body(x_ref, g_ref, w_ref, qkg_ref, gsum_ref, cos_ref, sin_ref,
                 vn_ref, ws_ref, gbias_ref,
                 qt_ref, k_ref, vt_ref, gmlp_ref, lr_ref):
    xn = _rms(x_ref[...], g_ref[...]).astype(BF16)

    def proj(lo, hi):
        return _dot(xn, w_ref[:, lo:hi])

    qk = proj(0, MIX_V_LO)
    sq = qk * qk
    sq_hi = sq.astype(BF16)
    sq_lo = (sq - sq_hi.astype(F32)).astype(BF16)
    ms = _dot(sq_hi, gsum_ref[...]) + _dot(sq_lo, gsum_ref[...])

    u = jax.nn.gelu(proj(MIX_GM_LO, MIX_GM_LO + GMLP_W))
    v = _rms(jax.nn.gelu(proj(MIX_GM_LO + GMLP_W, MIX_LR_LO)), vn_ref[...]).astype(BF16)
    vt_ref[0] = proj(MIX_V_LO, MIX_GM_LO).T.astype(BF16)
    lr_ref[...] = proj(MIX_LR_LO, MIX_GATE_LO)

    gbias = gbias_ref[...]
    for c in range(MIX_TM // GMLP_CHUNK):
        rows = slice(c * GMLP_CHUNK, (c + 1) * GMLP_CHUNK)
        sv = jnp.concatenate(
            [_dot(ws_ref[g], v[rows, g * GMLP_GROUP_W:(g + 1) * GMLP_GROUP_W])
             for g in range(GMLP_GROUPS)], axis=1)
        gmlp_ref[rows, :] = (u[rows] * (sv + gbias)).astype(BF16)

    qkn = qk * lax.rsqrt(ms + EPS) * qkg_ref[...]
    cos = cos_ref[...]
    sin = sin_ref[...]
    cols = []
    for c in range(QK_W // LANES):
        t = qkn[:, c * LANES:(c + 1) * LANES]
        cols.append(t * cos + _lane_partner(t, HEAD_DIM // 4) * sin)
    q = jnp.concatenate(cols[:ATTN_Q_W // LANES], axis=1) * (HEAD_DIM ** -0.5 * LOG2_E)
    for u in range(MIX_TM // ATTN_TQ):
        qt_ref[0, u] = q[u * ATTN_TQ:(u + 1) * ATTN_TQ].T.astype(BF16)
    k_ref[...] = cols[-1].astype(BF16)


def _mix_in(x, g, w, qkg, gsum, cos, sin, vn, ws, gbias, l, casts):
    seq_tiles = SEQ // MIX_TM
    row = lambda w_: pl.BlockSpec((MIX_TM, w_), lambda i: (i, 0))
    tab = pl.BlockSpec((MIX_TM, LANES), lambda i: (i % seq_tiles, 0))
    return _call_with_casts(
        _mix_in_body,
        steps=TOKENS // MIX_TM,
        in_specs=[row(D_MODEL), _layer(g, l),
                  _resident((D_MODEL, MIX_GATE_LO)),
                  _layer(qkg, l), _resident(gsum.shape), tab, tab,
                  _layer(vn, l), _resident(ws.shape), _layer(gbias, l)],
        out_specs=[
            pl.BlockSpec((1, MIX_TM // ATTN_TQ, ATTN_Q_W, ATTN_TQ),
                         lambda i: (i // seq_tiles, i % seq_tiles, 0, 0)),
            row(ATTN_KV_W),
            pl.BlockSpec((1, ATTN_KV_W, MIX_TM),
                         lambda i: (i // seq_tiles, 0, i % seq_tiles)),
            row(GMLP_W), row(2 * LRU_W)],
        out_shape=[
            jax.ShapeDtypeStruct((BATCH, SEQ // ATTN_TQ, ATTN_Q_W, ATTN_TQ), BF16),
            jax.ShapeDtypeStruct((TOKENS, ATTN_KV_W), BF16),
            jax.ShapeDtypeStruct((BATCH, ATTN_KV_W, SEQ), BF16),
            jax.ShapeDtypeStruct((TOKENS, GMLP_W), BF16),
            jax.ShapeDtypeStruct((TOKENS, 2 * LRU_W), F32)],
        args=(x, g, w, qkg, gsum, cos, sin, vn, ws, gbias),
        casts=casts,
        name="mix_in",
    )


ATTN_SUM_ROWS = 64
ATTN_MAX_EXPONENT_SPAN = 64.0
ATTN_ROUNDING_SLACK = 1.0 + 2.0 ** -8


def _attn_body(qg_ref, qt_ref, k_ref, vt_ref, o_ref, vt_aug, k_aug, st_buf, pt_buf):
    own_lanes = (lax.broadcasted_iota(jnp.int32, (1, ATTN_KV_W), 1) // HEAD_DIM
                 == pl.program_id(1))
    vt_aug[0:HEAD_DIM, :] = vt_ref[0]
    vt_aug[HEAD_DIM:, :] = jnp.ones((ATTN_SUM_ROWS, SEQ), BF16)
    kb = k_ref[...]
    k_aug[:, :ATTN_KV_W] = kb
    k_aug[:, ATTN_KV_W:] = jnp.ones((SEQ, ATTN_KV_W), BF16)

    kf = kb.astype(F32)
    norm2 = jnp.sum(jnp.where(own_lanes, kf * kf, 0.0), axis=1, keepdims=True)
    kmax = jnp.sqrt(jnp.max(norm2, axis=0, keepdims=True))
    qmax = jnp.max(jnp.abs(qg_ref[:, :ATTN_Q_W]), axis=1, keepdims=True)
    shift = jnp.max(qmax * kmax) * (LOG2_E * ATTN_ROUNDING_SLACK)

    kv_head = lax.broadcasted_iota(jnp.int32, (ATTN_KV_W, 1), 0) // HEAD_DIM
    own_rows = kv_head == pl.program_id(1)

    n_sub = ATTN_TQ // ATTN_CHAIN_Q
    chains = [(sub, h) for sub in range(n_sub) for h in range(ATTN_GROUP)]

    def q_weights(qt, c):
        sub, h = chains[c]
        qh = qt[h * HEAD_DIM:(h + 1) * HEAD_DIM,
                sub * ATTN_CHAIN_Q:(sub + 1) * ATTN_CHAIN_Q]
        stacked = jnp.concatenate([qh] * ATTN_KV_HEADS, axis=0)
        return jnp.where(own_rows, stacked, jnp.zeros_like(stacked))

    def finish(i, outs):
        ot_all = jnp.concatenate([jnp.concatenate(o, axis=0) for o in outs], axis=1)
        rows = pl.ds(pl.multiple_of(i * ATTN_TQ, ATTN_TQ), ATTN_TQ)
        o_ref[rows, :] = ot_all.T.astype(BF16)

    def shifted_tile(i, carry):
        qt = qt_ref[0, i]
        shift_row = lax.broadcasted_iota(jnp.int32, (ATTN_KV_W, ATTN_CHAIN_Q), 0) == 0
        tail = jnp.where(shift_row, -shift, 0.0).astype(BF16)
        outs = [[None] * ATTN_GROUP for _ in range(n_sub)]
        for first in range(0, len(chains), ATTN_PAIR):
            pair = range(first, first + ATTN_PAIR)
            for c in pair:
                w = jnp.concatenate([q_weights(qt, c), tail], axis=0)
                pt_buf[c % ATTN_SLOTS] = jnp.exp2(_dot(k_aug[...], w)).astype(BF16)
            for c in pair:
                sub, h = chains[c]
                ot = _dot(vt_aug[...], pt_buf[c % ATTN_SLOTS])
                outs[sub][h] = ot[:HEAD_DIM] / ot[HEAD_DIM:HEAD_DIM + 1]
        finish(i, outs)
        return carry

    def exact_tile(i, carry):
        qt = qt_ref[0, i]
        col_max = {}

        def scores(c):
            st = _dot(kb, q_weights(qt, c))
            col_max[c] = jnp.max(st, axis=0, keepdims=True)
            st_buf[c % ATTN_SLOTS] = st

        for c in range(min(ATTN_SLOTS, len(chains))):
            scores(c)
        outs = [[None] * ATTN_GROUP for _ in range(n_sub)]
        for c, (sub, h) in enumerate(chains):
            slot = c % ATTN_SLOTS
            pt_buf[slot] = jnp.exp2(st_buf[slot] - col_max[c]).astype(BF16)
            ot = _dot(vt_aug[...], pt_buf[slot])
            outs[sub][h] = ot[:HEAD_DIM] / ot[HEAD_DIM:HEAD_DIM + 1]
            if c + ATTN_SLOTS < len(chains):
                scores(c + ATTN_SLOTS)
        finish(i, outs)
        return carry

    shift_is_safe = shift * 2.0 <= ATTN_MAX_EXPONENT_SPAN
    q_tiles = SEQ // ATTN_TQ

    @pl.when(shift_is_safe)
    def _():
        lax.fori_loop(0, q_tiles, shifted_tile, 0)

    @pl.when(jnp.logical_not(shift_is_safe))
    def _():
        lax.fori_loop(0, q_tiles, exact_tile, 0)


def _attention(qkg, qt, k, vt, l):
    q_tiles = SEQ // ATTN_TQ
    return pl.pallas_call(
        _attn_body,
        grid=(BATCH, ATTN_KV_HEADS),
        in_specs=[
            _layer(qkg, l),
            pl.BlockSpec((1, q_tiles, GROUP_Q_W, ATTN_TQ), lambda b, j: (b, 0, j, 0)),
            pl.BlockSpec((SEQ, ATTN_KV_W), lambda b, j: (b, 0)),
            pl.BlockSpec((1, HEAD_DIM, SEQ), lambda b, j: (b, j, 0)),
        ],
        out_specs=pl.BlockSpec((SEQ, GROUP_Q_W), lambda b, j: (b, j)),
        out_shape=jax.ShapeDtypeStruct((TOKENS, ATTN_Q_W), BF16),
        scratch_shapes=[pltpu.VMEM((HEAD_DIM + ATTN_SUM_ROWS, SEQ), BF16),
                        pltpu.VMEM((SEQ, 2 * ATTN_KV_W), BF16),
                        pltpu.VMEM((ATTN_SLOTS, SEQ, ATTN_CHAIN_Q), F32),
                        pltpu.VMEM((ATTN_SLOTS, SEQ, ATTN_CHAIN_Q), BF16)],
        compiler_params=_params(2),
        name="gqa_attention",
    )(qkg, qt, k, vt)


def _lru_body(lx_ref, ly_ref, cw_ref, cb_ref, w_ref, b_ref, lam_ref, o_ref,
              xpad, a_f, b_f, a_b, b_b, carry_f, carry_b):
    n_chunks = SEQ // LRU_ROWS
    halo_zeros = jnp.zeros((LRU_SLABS, LRU_HALO, LANES), F32)
    xpad[:, 0:LRU_HALO, :] = halo_zeros
    xpad[:, LRU_HALO + SEQ:LRU_HALO + SEQ + LRU_HALO, :] = halo_zeros
    for s in range(LRU_SLABS):
        xpad[s, LRU_HALO:LRU_HALO + SEQ, :] = lx_ref[:, s * LANES:(s + 1) * LANES]

    half_rate = -(0.5 * LRU_C * LOG2_E) * jax.nn.softplus(-lam_ref[...])
    cw = cw_ref[...]
    cb = cb_ref[...]
    bias = b_ref[...]
    scan_bufs = ((a_f, b_f), (a_b, b_b))

    def dense(c, carry):
        r0 = pl.multiple_of(c * LRU_ROWS, LRU_ROWS)
        slabs = []
        for s in range(LRU_SLABS):
            lanes = slice(s * LANES, (s + 1) * LANES)
            acc = cb[:, lanes]
            for j in range(CONV_W):
                off = LRU_HALO + j - CONV_W // 2
                acc = acc + cw[j:j + 1, lanes] * xpad[s, pl.ds(r0 + off, LRU_ROWS), :]
            slabs.append(acc)
        xc = jnp.concatenate(slabs, axis=1)
        pre = _dot(xc.astype(BF16), w_ref[...]) + bias
        half_xc = 0.5 * xc
        for d in range(N_DIR):
            t_r = jnp.tanh(pre[:, (2 * d) * LRU_BLOCK:(2 * d + 1) * LRU_BLOCK])
            t_i = jnp.tanh(pre[:, (2 * d + 1) * LRU_BLOCK:(2 * d + 2) * LRU_BLOCK])
            rate_d = half_rate[d:d + 1, :]
            a = jnp.exp2(t_r * rate_d + rate_d)
            g2 = 1.0 - a * a
            gain = jnp.where(g2 > 0.0, g2 * lax.rsqrt(g2), 0.0)
            b = gain * (t_i * half_xc + half_xc)
            a_buf, b_buf = scan_bufs[d]
            for s in range(LRU_SLABS):
                lanes = slice(s * LANES, (s + 1) * LANES)
                a_buf[s, pl.ds(r0, LRU_ROWS), :] = a[:, lanes]
                b_buf[s, pl.ds(r0, LRU_ROWS), :] = b[:, lanes]
        return carry

    lax.fori_loop(0, n_chunks, dense, 0)
    tail = LRU_PAD_SEQ - SEQ
    for a_buf, b_buf in scan_bufs:
        a_buf[:, SEQ:LRU_PAD_SEQ, :] = jnp.ones((LRU_SLABS, tail, LANES), F32)
        b_buf[:, SEQ:LRU_PAD_SEQ, :] = jnp.zeros((LRU_SLABS, tail, LANES), F32)

    def seg(t):
        return pl.ds(t, SUBLANES, stride=LRU_SEG)

    def scan_step(k, carry):
        out = []
        for (a_buf, b_buf), t, chains in ((scan_bufs[0], k, carry[0]),
                                          (scan_bufs[1], LRU_SEG - 1 - k, carry[1])):
            new = []
            for s, (h, p) in enumerate(chains):
                a = a_buf[s, seg(t), :]
                h = a * h + b_buf[s, seg(t), :]
                p = a * p
                b_buf[s, seg(t), :] = h
                a_buf[s, seg(t), :] = p
                new.append((h, p))
            out.append(tuple(new))
        return tuple(out)

    z = jnp.zeros((SUBLANES, LANES), F32)
    o = jnp.ones((SUBLANES, LANES), F32)
    init = tuple(tuple((z, o) for _ in range(LRU_SLABS)) for _ in range(N_DIR))
    lax.fori_loop(0, LRU_SEG, scan_step, init, unroll=LRU_SCAN_UNROLL)

    row0 = jnp.zeros((1, LANES), F32)
    for s in range(LRU_SLABS):
        carry_f[s, 0:1, :] = row0
        c = row0
        for g in range(1, SUBLANES):
            last = g * LRU_SEG - 1
            c = b_f[s, last:last + 1, :] + a_f[s, last:last + 1, :] * c
            carry_f[s, g:g + 1, :] = c
        carry_b[s, SUBLANES - 1:SUBLANES, :] = row0
        c = row0
        for g in range(SUBLANES - 2, -1, -1):
            first = (g + 1) * LRU_SEG
            c = b_b[s, first:first + 1, :] + a_b[s, first:first + 1, :] * c
            carry_b[s, g:g + 1, :] = c

    seg_shift = LRU_SEG - LRU_ROWS
    row_in_chunk = lax.broadcasted_iota(jnp.int32, (LRU_ROWS, LANES), 0)

    def emit(c, carry):
        r0 = pl.multiple_of(c * LRU_ROWS, LRU_ROWS)
        rows = pl.ds(r0, LRU_ROWS)
        prev = pl.ds(jnp.maximum(c - 1, 0), 1)
        in_prev = row_in_chunk < c * seg_shift
        hs = []
        for s in range(LRU_SLABS):
            cf = jnp.where(in_prev, carry_f[s, prev, :], carry_f[s, pl.ds(c, 1), :])
            cb_ = jnp.where(in_prev, carry_b[s, prev, :], carry_b[s, pl.ds(c, 1), :])
            hs.append((b_f[s, rows, :] + a_f[s, rows, :] * cf)
                      + (b_b[s, rows, :] + a_b[s, rows, :] * cb_))
        h = jnp.concatenate(hs, axis=1)
        o_ref[rows, :] = (h * jax.nn.gelu(ly_ref[rows, :])).astype(BF16)
        return carry

    lax.fori_loop(0, n_chunks, emit, 0)


def _lru(lr, cw, cb, w, b, lam, l):
    n_blk = LRU_W // LRU_BLOCK
    scan_buf = pltpu.VMEM((LRU_SLABS, LRU_PAD_SEQ, LANES), F32)
    carry_buf = pltpu.VMEM((LRU_SLABS, SUBLANES, LANES), F32)
    return pl.pallas_call(
        _lru_body,
        grid=(BATCH, n_blk),
        in_specs=[
            pl.BlockSpec((SEQ, LRU_BLOCK), lambda b_, c: (b_, c)),
            pl.BlockSpec((SEQ, LRU_BLOCK), lambda b_, c: (b_, n_blk + c)),
            pl.BlockSpec((None, CONV_W, LRU_BLOCK), lambda b_, c: (l, 0, c)),
            pl.BlockSpec((None, 1, LRU_BLOCK), lambda b_, c: (l, 0, c)),
            pl.BlockSpec((None, None, LRU_BLOCK, 2 * N_DIR * LRU_BLOCK),
                         lambda b_, c: (l, c, 0, 0)),
            pl.BlockSpec((None, 1, 2 * N_DIR * LRU_BLOCK), lambda b_, c: (l, 0, c)),
            pl.BlockSpec((None, N_DIR, LRU_BLOCK), lambda b_, c: (l, 0, c)),
        ],
        out_specs=pl.BlockSpec((SEQ, LRU_BLOCK), lambda b_, c: (b_, c)),
        out_shape=jax.ShapeDtypeStruct((TOKENS, LRU_W), BF16),
        scratch_shapes=[
            pltpu.VMEM((LRU_SLABS, SEQ + 2 * LRU_HALO, LANES), F32),
            scan_buf, scan_buf, scan_buf, scan_buf,
            carry_buf, carry_buf,
        ],
        compiler_params=_params(2),
        name="rg_lru",
    )(lr, lr, cw, cb, w, b, lam)


def _xkv_body(mem_ref, g_ref, wkv_ref, kt_ref, v_ref):
    mn = _rms(mem_ref[0], g_ref[...]).astype(BF16)
    kv = _dot(mn, wkv_ref[...])
    kt_ref[0] = (kv[:, :D_MODEL] * (XATTN_HEAD_DIM ** -0.5)).T.astype(BF16)
    v_ref[0] = kv[:, D_MODEL:].astype(BF16)


def _xkv(mem, g, wkv, l):
    return pl.pallas_call(
        _xkv_body,
        grid=(BATCH,),
        in_specs=[pl.BlockSpec((1, MEM_LEN, D_MODEL), lambda b: (b, 0, 0)),
                  _layer(g, l), _resident(wkv.shape)],
        out_specs=[pl.BlockSpec((1, D_MODEL, MEM_LEN), lambda b: (b, 0, 0)),
                   pl.BlockSpec((1, MEM_LEN, D_MODEL), lambda b: (b, 0, 0))],
        out_shape=[jax.ShapeDtypeStruct((BATCH, D_MODEL, MEM_LEN), BF16),
                   jax.ShapeDtypeStruct((BATCH, MEM_LEN, D_MODEL), BF16)],
        compiler_params=_params(1),
        name="xattn_kv",
    )(mem, g, wkv)


def _merge_xattn_body(x_ref, a_ref, g_ref, l_ref, mixg_ref, wmix_ref, bgate_ref,
                      wa_ref, wg_ref, wl_ref, wo_ref,
                      xg_ref, wq_ref, kt_ref, v_ref, xwo_ref, o_ref):
    x = x_ref[...]
    hn = _rms(x, mixg_ref[...]).astype(BF16)
    merged = jnp.zeros((MERGE_TM, D_MODEL), F32)
    for k, (br, w) in enumerate(((a_ref, wa_ref), (g_ref, wg_ref), (l_ref, wl_ref))):
        lo = MIX_GATE_LO + k * D_MODEL
        gate = jax.nn.sigmoid(_dot(hn, wmix_ref[:, lo:lo + D_MODEL])
                              + bgate_ref[:, k * D_MODEL:(k + 1) * D_MODEL])
        merged = merged + gate * _dot(br[...], w[...])
    x = x + _dot(merged.astype(BF16), wo_ref[...])

    q = _dot(_rms(x, xg_ref[...]).astype(BF16), wq_ref[...]).astype(BF16)
    heads = []
    for h in range(XATTN_HEADS):
        cols = slice(h * XATTN_HEAD_DIM, (h + 1) * XATTN_HEAD_DIM)
        s = _dot(q[:, cols], kt_ref[0, cols, :])
        p = jnp.exp(s - jnp.max(s, axis=-1, keepdims=True))
        l = jnp.sum(p, axis=-1, keepdims=True)
        heads.append((_dot(p.astype(BF16), v_ref[0, :, cols]) * (1.0 / l)).astype(BF16))
    o_ref[...] = x + _dot(jnp.concatenate(heads, axis=1), xwo_ref[...])


def _merge_xattn(x, attn_o, gmlp_o, lru_o, mixg, wmix, bgate, wa, wg, wl, wo,
                 xg, wq, kt, v, xwo, l, casts):
    tiles = SEQ // MERGE_TM
    row = lambda w: pl.BlockSpec((MERGE_TM, w), lambda i: (i, 0))
    (y,), cast = _call_with_casts(
        _merge_xattn_body,
        steps=TOKENS // MERGE_TM,
        in_specs=[row(D_MODEL), row(ATTN_Q_W), row(GMLP_W), row(LRU_W),
                  _layer(mixg, l), _resident(wmix.shape), _layer(bgate, l),
                  _resident(wa.shape), _resident(wg.shape), _resident(wl.shape),
                  _resident(wo.shape),
                  _layer(xg, l), _resident(wq.shape),
                  pl.BlockSpec((1, D_MODEL, MEM_LEN), lambda i: (i // tiles, 0, 0)),
                  pl.BlockSpec((1, MEM_LEN, D_MODEL), lambda i: (i // tiles, 0, 0)),
                  _resident(xwo.shape)],
        out_specs=[row(D_MODEL)],
        out_shape=[jax.ShapeDtypeStruct((TOKENS, D_MODEL), F32)],
        args=(x, attn_o, gmlp_o, lru_o, mixg, wmix, bgate, wa, wg, wl, wo,
              xg, wq, kt, v, xwo),
        casts=casts,
        name="merge_xattn",
    )
    return y, cast


def _rope_tables():
    rows = SEQ // GRID_W
    row = jnp.repeat(jnp.arange(rows), GRID_W).astype(F32)
    col = jnp.tile(jnp.arange(GRID_W), rows).astype(F32)
    n_freq = HEAD_DIM // 4
    inv_freq = ROPE_THETA ** (-jnp.arange(n_freq, dtype=F32) / n_freq)
    ang_r = row[:, None] * inv_freq[None, :]
    ang_c = col[:, None] * inv_freq[None, :]
    cos = jnp.concatenate([jnp.cos(ang_r)] * 2 + [jnp.cos(ang_c)] * 2, axis=1)
    sin = jnp.concatenate([-jnp.sin(ang_r), jnp.sin(ang_r),
                           -jnp.sin(ang_c), jnp.sin(ang_c)], axis=1)
    rep = LANES // HEAD_DIM
    return jnp.tile(cos, (1, rep)), jnp.tile(sin, (1, rep))


def _block_diag(w):
    n, a, b = w.shape[-3:]
    eye = jnp.eye(n, dtype=w.dtype)
    out = eye[:, None, :, None] * w[..., :, :, None, :]
    return out.reshape(w.shape[:-3] + (n * a, n * b))


def kernel(x, mem, ffn1_norm, ffn1_w_in, ffn1_w_out, mix_norm, w_mix_in, b_gate,
           q_norm, k_norm, attn_up, gmlp_v_norm, gmlp_ws, gmlp_bs, gmlp_up,
           lru_conv_w, lru_conv_b, lru_wa, lru_ba, lru_wi, lru_bi, lru_lambda, lru_up,
           w_mix_out, xattn_norm, mem_norm, xattn_wq, xattn_wkv, xattn_wo,
           ffn2_norm, ffn2_w_in, ffn2_w_out, final_norm):
    assert x.shape == (BATCH, SEQ, D_MODEL) and mem.shape == (BATCH, MEM_LEN, D_MODEL)
    vec = lambda a: a.reshape(DEPTH, 1, -1)
    cos, sin = _rope_tables()
    fin = final_norm.reshape(1, D_MODEL)
    gsum = _block_diag(jnp.full((QK_W // HEAD_DIM, HEAD_DIM, HEAD_DIM),
                                1.0 / HEAD_DIM, BF16))
    qkg = vec(jnp.concatenate([jnp.tile(q_norm, (1, ATTN_HEADS)),
                               jnp.tile(k_norm, (1, ATTN_KV_HEADS))], axis=1))
    gm_bias = jnp.repeat(jnp.swapaxes(gmlp_bs, 1, 2), GMLP_GROUP_W, axis=2)
    gm_ws_f32 = gmlp_ws.reshape(DEPTH, GMLP_W, GMLP_CHUNK)

    n_blk = LRU_W // LRU_BLOCK
    heads_per_blk = LRU_BLOCK // LRU_HEAD_W

    def blk(w):
        return _block_diag(
            w.reshape(DEPTH, n_blk, heads_per_blk, LRU_HEAD_W, LRU_HEAD_W))
    lru_w = (0.5 * jnp.concatenate(
        [blk(lru_wa[:, 0]), blk(lru_wi[:, 0]), blk(lru_wa[:, 1]), blk(lru_wi[:, 1])],
        axis=-1)).astype(BF16)
    lru_b = 0.5 * jnp.stack(
        [b.reshape(DEPTH, n_blk, LRU_BLOCK)
         for b in (lru_ba[:, 0], lru_bi[:, 0], lru_ba[:, 1], lru_bi[:, 1])],
        axis=2).reshape(DEPTH, 1, -1)

    mixer_f32 = (w_mix_in, xattn_wkv, attn_up, gmlp_up, lru_up, w_mix_out,
                 xattn_wq, xattn_wo, gm_ws_f32)
    ffn1_w = [_to_bf16(ffn1_w_in, 0), _to_bf16(ffn1_w_out, 0)]
    mixer_w = None

    h = x.reshape(TOKENS, D_MODEL)
    for l in range(DEPTH):
        nxt = l + 1 if l + 1 < DEPTH else None
        casts = [(ffn2_w_in, l), (ffn2_w_out, l)]
        if mixer_w is None:
            casts += [(w, l) for w in mixer_f32]
        h, cast = _ffn(h, vec(ffn1_norm), *ffn1_w, fin, l, casts, final=False)
        ffn2_w = cast[:2]
        if mixer_w is None:
            mixer_w = cast[2:]
        w_mix, w_xkv, w_attn_up, w_gmlp_up, w_lru_up, w_out, w_xq, w_xo, gm_ws = mixer_w

        (qt, k, vt, gmlp_o, lr), _ = _mix_in(
            h, vec(mix_norm), w_mix, qkg, gsum, cos, sin,
            vec(gmlp_v_norm), gm_ws.reshape(gmlp_ws.shape[1:]), gm_bias, l, [])
        attn_o = _attention(qkg, qt, k, vt, l)
        lru_o = _lru(lr, lru_conv_w, vec(lru_conv_b), lru_w, lru_b, lru_lambda, l)
        xkt, xv = _xkv(mem, vec(mem_norm), w_xkv, l)
        h, ffn1_w = _merge_xattn(
            h, attn_o, gmlp_o, lru_o, vec(mix_norm), w_mix, vec(b_gate),
            w_attn_up, w_gmlp_up, w_lru_up, w_out, vec(xattn_norm), w_xq, xkt, xv, w_xo,
            l, [] if nxt is None else [(ffn1_w_in, nxt), (ffn1_w_out, nxt)])
        h, mixer_w = _ffn(h, vec(ffn2_norm), *ffn2_w, fin, l,
                          [] if nxt is None else [(w, nxt) for w in mixer_f32],
                          final=nxt is None)
    return h.reshape(BATCH, SEQ, D_MODEL)
```

```python
import functools

import jax
import jax.numpy as jnp
from jax import lax
from jax.experimental import pallas as pl
from jax.experimental.pallas import tpu as pltpu

F32 = jnp.float32
BF16 = jnp.bfloat16

D_MODEL = 1024
BATCH = 4
SEQ = 4096
DEPTH = 2
TOKENS = BATCH * SEQ
MEM_LEN = 256
GRID_W = 64
EPS = 1e-6

ATTN_HEADS = 8
ATTN_KV_HEADS = 2
ATTN_GROUP = ATTN_HEADS // ATTN_KV_HEADS
HEAD_DIM = 64
ATTN_Q_W = ATTN_HEADS * HEAD_DIM
ATTN_KV_W = ATTN_KV_HEADS * HEAD_DIM
QK_W = ATTN_Q_W + ATTN_KV_W
GROUP_Q_W = ATTN_GROUP * HEAD_DIM
ROPE_THETA = 10000.0
LOG2_E = 1.4426950408889634

GMLP_W = 512
GMLP_GROUPS = 4
GMLP_GROUP_W = GMLP_W // GMLP_GROUPS
GMLP_CHUNK = 128

LRU_W = 512
LRU_HEADS = 8
LRU_HEAD_W = LRU_W // LRU_HEADS
CONV_W = 4
LRU_C = 8.0
N_DIR = 2
N_BRANCH = 3

XATTN_HEADS = 4
XATTN_HEAD_DIM = D_MODEL // XATTN_HEADS
D_FF = 2816

MIX_V_LO = ATTN_Q_W + ATTN_KV_W
MIX_GM_LO = MIX_V_LO + ATTN_KV_W
MIX_LR_LO = MIX_GM_LO + 2 * GMLP_W
MIX_GATE_LO = MIX_LR_LO + 2 * LRU_W

LANES = 128
SUBLANES = 8
BF16_ROWS = 16
VMEM_LIMIT_BYTES = 56 * 1024 * 1024

CAST_ROWS = 256
FFN_TM = 1024
FFN_CHUNK = 256
MIX_TM = 1024
ATTN_TQ = 512
ATTN_CHAIN_Q = 256
ATTN_SLOTS = 4
ATTN_PAIR = 2
MERGE_TM = 512
LRU_BLOCK = 256
LRU_SLABS = LRU_BLOCK // LANES
LRU_ROWS = 512
LRU_SCAN_UNROLL = 4
LRU_SEG = 516
LRU_PAD_SEQ = SUBLANES * LRU_SEG
LRU_HALO = SUBLANES


def _resident(shape):
    nd = len(shape)
    return pl.BlockSpec(shape, lambda *_: (0,) * nd, pipeline_mode=pl.Buffered(1))


def _layer(arr, l, cols=None):
    shape = arr.shape[1:] if cols is None else arr.shape[1:-1] + (cols,)
    nd = len(shape)
    return pl.BlockSpec((None,) + shape, lambda *_: (l,) + (0,) * nd,
                        pipeline_mode=pl.Buffered(1))


def _params(n_grid_axes):
    return pltpu.CompilerParams(
        dimension_semantics=("arbitrary",) * n_grid_axes,
        vmem_limit_bytes=VMEM_LIMIT_BYTES,
    )


def _rms(x, g):
    ms = jnp.mean(x * x, axis=-1, keepdims=True)
    return x * lax.rsqrt(ms + EPS) * g


def _dot(a, b):
    return jnp.dot(a, b, preferred_element_type=F32)


def _cast_body(w_ref, o_ref):
    o_ref[...] = w_ref[...].astype(BF16)


def _to_bf16(w, l):
    _, rows, cols = w.shape
    return pl.pallas_call(
        _cast_body,
        grid=(rows // CAST_ROWS,),
        in_specs=[pl.BlockSpec((None, CAST_ROWS, cols), lambda i: (l, i, 0))],
        out_specs=pl.BlockSpec((CAST_ROWS, cols), lambda i: (i, 0)),
        out_shape=jax.ShapeDtypeStruct((rows, cols), BF16),
        compiler_params=_params(1),
        name="cast_bf16",
    )(w)


def _cast_block_rows(rows, steps):
    for n in range(min(steps, rows // BF16_ROWS), 0, -1):
        if rows % n == 0 and (rows // n) % BF16_ROWS == 0:
            return rows // n
    raise ValueError((rows, steps))


def _call_with_casts(body, *, steps, in_specs, out_specs, out_shape, args, casts,
                     name, scratch_shapes=()):
    n_in, n_out, n_cast = len(in_specs), len(out_specs), len(casts)
    c_in, c_out, c_shape = [], [], []
    for w, l in casts:
        _, rows, cols = w.shape
        blk = _cast_block_rows(rows, steps)
        last = rows // blk - 1
        c_in.append(pl.BlockSpec((None, blk, cols),
                                 lambda i, l=l, last=last: (l, jnp.minimum(i, last), 0)))
        c_out.append(pl.BlockSpec((blk, cols),
                                  lambda i, last=last: (jnp.minimum(i, last), 0)))
        c_shape.append(jax.ShapeDtypeStruct((rows, cols), BF16))

    def with_casts(*refs):
        ins, srcs = refs[:n_in], refs[n_in:n_in + n_cast]
        outs = refs[n_in + n_cast:n_in + n_cast + n_out]
        dsts = refs[n_in + n_cast + n_out:n_in + 2 * n_cast + n_out]
        for src, dst in zip(srcs, dsts):
            dst[...] = src[...].astype(BF16)
        body(*ins, *outs, *refs[n_in + 2 * n_cast + n_out:])

    res = pl.pallas_call(
        with_casts,
        grid=(steps,),
        in_specs=list(in_specs) + c_in,
        out_specs=list(out_specs) + c_out,
        out_shape=list(out_shape) + c_shape,
        scratch_shapes=list(scratch_shapes),
        compiler_params=_params(1),
        name=name,
    )(*args, *[w for w, _ in casts])
    return res[:n_out], res[n_out:]


def _ffn_body(x_ref, g_ref, win_ref, wout_ref, fin_ref, o_ref, *, final):
    x = x_ref[...]
    xn = _rms(x, g_ref[...]).astype(BF16)
    acc = jnp.zeros(x.shape, F32)
    for c in range(D_FF // FFN_CHUNK):
        lo = c * FFN_CHUNK
        a = _dot(xn, win_ref[:, lo:lo + FFN_CHUNK])
        b = _dot(xn, win_ref[:, D_FF + lo:D_FF + lo + FFN_CHUNK])
        h = (a * jax.nn.sigmoid(a) * b).astype(BF16)
        acc = acc + _dot(h, wout_ref[lo:lo + FFN_CHUNK, :])
    y = x + 0.5 * acc
    if final:
        y = _rms(y, fin_ref[...])
    o_ref[...] = y


def _ffn(x, g, w_in, w_out, fin, l, casts, *, final):
    tile = pl.BlockSpec((FFN_TM, D_MODEL), lambda i: (i, 0))
    (y,), cast = _call_with_casts(
        functools.partial(_ffn_body, final=final),
        steps=TOKENS // FFN_TM,
        in_specs=[tile, _layer(g, l), _resident(w_in.shape), _resident(w_out.shape),
                  _resident((1, D_MODEL))],
        out_specs=[tile],
        out_shape=[jax.ShapeDtypeStruct((TOKENS, D_MODEL), F32)],
        args=(x, g, w_in, w_out, fin),
        casts=casts,
        name="ffn_final" if final else "ffn",
    )
    return y, cast


def _lane_partner(x, d):
    lane = lax.broadcasted_iota(jnp.int32, x.shape, 1)
    up = pltpu.roll(x, LANES - d, 1)
    down = pltpu.roll(x, d, 1)
    return jnp.where((lane & d) == 0, up, down)


def _mix_in_body(x_ref, g_ref, w_ref, qkg_ref, gsum_ref, cos_ref, sin_ref,
                 vn_ref, ws_ref, gbias_ref,
                 qt_ref, k_ref, vt_ref, gmlp_ref, lr_ref):
    xn = _rms(x_ref[...], g_ref[...]).astype(BF16)

    def proj(lo, hi):
        return _dot(xn, w_ref[:, lo:hi])

    qk = proj(0, MIX_V_LO)
    sq = qk * qk
    sq_hi = sq.astype(BF16)
    sq_lo = (sq - sq_hi.astype(F32)).astype(BF16)
    ms = _dot(sq_hi, gsum_ref[...]) + _dot(sq_lo, gsum_ref[...])

    u = jax.nn.gelu(proj(MIX_GM_LO, MIX_GM_LO + GMLP_W))
    v = _rms(jax.nn.gelu(proj(MIX_GM_LO + GMLP_W, MIX_LR_LO)), vn_ref[...]).astype(BF16)
    vt_ref[0] = proj(MIX_V_LO, MIX_GM_LO).T.astype(BF16)
    lr_ref[...] = proj(MIX_LR_LO, MIX_GATE_LO)

    gbias = gbias_ref[...]
    for c in range(MIX_TM // GMLP_CHUNK):
        rows = slice(c * GMLP_CHUNK, (c + 1) * GMLP_CHUNK)
        sv = jnp.concatenate(
            [_dot(ws_ref[g], v[rows, g * GMLP_GROUP_W:(g + 1) * GMLP_GROUP_W])
             for g in range(GMLP_GROUPS)], axis=1)
        gmlp_ref[rows, :] = (u[rows] * (sv + gbias)).astype(BF16)

    qkn = qk * lax.rsqrt(ms + EPS) * qkg_ref[...]
    cos = cos_ref[...]
    sin = sin_ref[...]
    cols = []
    for c in range(QK_W // LANES):
        t = qkn[:, c * LANES:(c + 1) * LANES]
        cols.append(t * cos + _lane_partner(t, HEAD_DIM // 4) * sin)
    q = jnp.concatenate(cols[:ATTN_Q_W // LANES], axis=1) * (HEAD_DIM ** -0.5 * LOG2_E)
    for u in range(MIX_TM // ATTN_TQ):
        qt_ref[0, u] = q[u * ATTN_TQ:(u + 1) * ATTN_TQ].T.astype(BF16)
    k_ref[...] = cols[-1].astype(BF16)


def _mix_in(x, g, w, qkg, gsum, cos, sin, vn, ws, gbias, l, casts):
    seq_tiles = SEQ // MIX_TM
    row = lambda w_: pl.BlockSpec((MIX_TM, w_), lambda i: (i, 0))
    tab = pl.BlockSpec((MIX_TM, LANES), lambda i: (i % seq_tiles, 0))
    return _call_with_casts(
        _mix_in_body,
        steps=TOKENS // MIX_TM,
        in_specs=[row(D_MODEL), _layer(g, l),
                  _resident((D_MODEL, MIX_GATE_LO)),
                  _layer(qkg, l), _resident(gsum.shape), tab, tab,
                  _layer(vn, l), _resident(ws.shape), _layer(gbias, l)],
        out_specs=[
            pl.BlockSpec((1, MIX_TM // ATTN_TQ, ATTN_Q_W, ATTN_TQ),
                         lambda i: (i // seq_tiles, i % seq_tiles, 0, 0)),
            row(ATTN_KV_W),
            pl.BlockSpec((1, ATTN_KV_W, MIX_TM),
                         lambda i: (i // seq_tiles, 0, i % seq_tiles)),
            row(GMLP_W), row(2 * LRU_W)],
        out_shape=[
            jax.ShapeDtypeStruct((BATCH, SEQ // ATTN_TQ, ATTN_Q_W, ATTN_TQ), BF16),
            jax.ShapeDtypeStruct((TOKENS, ATTN_KV_W), BF16),
            jax.ShapeDtypeStruct((BATCH, ATTN_KV_W, SEQ), BF16),
            jax.ShapeDtypeStruct((TOKENS, GMLP_W), BF16),
            jax.ShapeDtypeStruct((TOKENS, 2 * LRU_W), F32)],
        args=(x, g, w, qkg, gsum, cos, sin, vn, ws, gbias),
        casts=casts,
        name="mix_in",
    )


ATTN_SUM_ROWS = 64
ATTN_MAX_EXPONENT_SPAN = 64.0
ATTN_ROUNDING_SLACK = 1.0 + 2.0 ** -8


def _attn_body(qg_ref, qt_ref, k_ref, vt_ref, o_ref, vt_aug, k_aug, st_buf, pt_buf):
    own_lanes = (lax.broadcasted_iota(jnp.int32, (1, ATTN_KV_W), 1) // HEAD_DIM
                 == pl.program_id(1))
    vt_aug[0:HEAD_DIM, :] = vt_ref[0]
    vt_aug[HEAD_DIM:, :] = jnp.ones((ATTN_SUM_ROWS, SEQ), BF16)
    kb = k_ref[...]
    k_aug[:, :ATTN_KV_W] = kb
    k_aug[:, ATTN_KV_W:] = jnp.ones((SEQ, ATTN_KV_W), BF16)

    kf = kb.astype(F32)
    norm2 = jnp.sum(jnp.where(own_lanes, kf * kf, 0.0), axis=1, keepdims=True)
    kmax = jnp.sqrt(jnp.max(norm2, axis=0, keepdims=True))
    qmax = jnp.max(jnp.abs(qg_ref[:, :ATTN_Q_W]), axis=1, keepdims=True)
    shift = jnp.max(qmax * kmax) * (LOG2_E * ATTN_ROUNDING_SLACK)

    kv_head = lax.broadcasted_iota(jnp.int32, (ATTN_KV_W, 1), 0) // HEAD_DIM
    own_rows = kv_head == pl.program_id(1)

    n_sub = ATTN_TQ // ATTN_CHAIN_Q
    chains = [(sub, h) for sub in range(n_sub) for h in range(ATTN_GROUP)]

    def q_weights(qt, c):
        sub, h = chains[c]
        qh = qt[h * HEAD_DIM:(h + 1) * HEAD_DIM,
                sub * ATTN_CHAIN_Q:(sub + 1) * ATTN_CHAIN_Q]
        stacked = jnp.concatenate([qh] * ATTN_KV_HEADS, axis=0)
        return jnp.where(own_rows, stacked, jnp.zeros_like(stacked))

    def finish(i, outs):
        ot_all = jnp.concatenate([jnp.concatenate(o, axis=0) for o in outs], axis=1)
        rows = pl.ds(pl.multiple_of(i * ATTN_TQ, ATTN_TQ), ATTN_TQ)
        o_ref[rows, :] = ot_all.T.astype(BF16)

    def shifted_tile(i, carry):
        qt = qt_ref[0, i]
        shift_row = lax.broadcasted_iota(jnp.int32, (ATTN_KV_W, ATTN_CHAIN_Q), 0) == 0
        tail = jnp.where(shift_row, -shift, 0.0).astype(BF16)
        outs = [[None] * ATTN_GROUP for _ in range(n_sub)]
        for first in range(0, len(chains), ATTN_PAIR):
            pair = range(first, first + ATTN_PAIR)
            for c in pair:
                w = jnp.concatenate([q_weights(qt, c), tail], axis=0)
                pt_buf[c % ATTN_SLOTS] = jnp.exp2(_dot(k_aug[...], w)).astype(BF16)
            for c in pair:
                sub, h = chains[c]
                ot = _dot(vt_aug[...], pt_buf[c % ATTN_SLOTS])
                outs[sub][h] = ot[:HEAD_DIM] / ot[HEAD_DIM:HEAD_DIM + 1]
        finish(i, outs)
        return carry

    def exact_tile(i, carry):
        qt = qt_ref[0, i]
        col_max = {}

        def scores(c):
            st = _dot(kb, q_weights(qt, c))
            col_max[c] = jnp.max(st, axis=0, keepdims=True)
            st_buf[c % ATTN_SLOTS] = st

        for c in range(min(ATTN_SLOTS, len(chains))):
            scores(c)
        outs = [[None] * ATTN_GROUP for _ in range(n_sub)]
        for c, (sub, h) in enumerate(chains):
            slot = c % ATTN_SLOTS
            pt_buf[slot] = jnp.exp2(st_buf[slot] - col_max[c]).astype(BF16)
            ot = _dot(vt_aug[...], pt_buf[slot])
            outs[sub][h] = ot[:HEAD_DIM] / ot[HEAD_DIM:HEAD_DIM + 1]
            if c + ATTN_SLOTS < len(chains):
                scores(c + ATTN_SLOTS)
        finish(i, outs)
        return carry

    shift_is_safe = shift * 2.0 <= ATTN_MAX_EXPONENT_SPAN
    q_tiles = SEQ // ATTN_TQ

    @pl.when(shift_is_safe)
    def _():
        lax.fori_loop(0, q_tiles, shifted_tile, 0)

    @pl.when(jnp.logical_not(shift_is_safe))
    def _():
        lax.fori_loop(0, q_tiles, exact_tile, 0)


def _attention(qkg, qt, k, vt, l):
    q_tiles = SEQ // ATTN_TQ
    return pl.pallas_call(
        _attn_body,
        grid=(BATCH, ATTN_KV_HEADS),
        in_specs=[
            _layer(qkg, l),
            pl.BlockSpec((1, q_tiles, GROUP_Q_W, ATTN_TQ), lambda b, j: (b, 0, j, 0)),
            pl.BlockSpec((SEQ, ATTN_KV_W), lambda b, j: (b, 0)),
            pl.BlockSpec((1, HEAD_DIM, SEQ), lambda b, j: (b, j, 0)),
        ],
        out_specs=pl.BlockSpec((SEQ, GROUP_Q_W), lambda b, j: (b, j)),
        out_shape=jax.ShapeDtypeStruct((TOKENS, ATTN_Q_W), BF16),
        scratch_shapes=[pltpu.VMEM((HEAD_DIM + ATTN_SUM_ROWS, SEQ), BF16),
                        pltpu.VMEM((SEQ, 2 * ATTN_KV_W), BF16),
                        pltpu.VMEM((ATTN_SLOTS, SEQ, ATTN_CHAIN_Q), F32),
                        pltpu.VMEM((ATTN_SLOTS, SEQ, ATTN_CHAIN_Q), BF16)],
        compiler_params=_params(2),
        name="gqa_attention",
    )(qkg, qt, k, vt)


def _lru_body(lx_ref, ly_ref, cw_ref, cb_ref, w_ref, b_ref, lam_ref, o_ref,
              xpad, a_f, b_f, a_b, b_b, carry_f, carry_b):
    n_chunks = SEQ // LRU_ROWS
    halo_zeros = jnp.zeros((LRU_SLABS, LRU_HALO, LANES), F32)
    xpad[:, 0:LRU_HALO, :] = halo_zeros
    xpad[:, LRU_HALO + SEQ:LRU_HALO + SEQ + LRU_HALO, :] = halo_zeros
    for s in range(LRU_SLABS):
        xpad[s, LRU_HALO:LRU_HALO + SEQ, :] = lx_ref[:, s * LANES:(s + 1) * LANES]

    half_rate = -(0.5 * LRU_C * LOG2_E) * jax.nn.softplus(-lam_ref[...])
    cw = cw_ref[...]
    cb = cb_ref[...]
    bias = b_ref[...]
    scan_bufs = ((a_f, b_f), (a_b, b_b))

    def dense(c, carry):
        r0 = pl.multiple_of(c * LRU_ROWS, LRU_ROWS)
        slabs = []
        for s in range(LRU_SLABS):
            lanes = slice(s * LANES, (s + 1) * LANES)
            acc = cb[:, lanes]
            for j in range(CONV_W):
                off = LRU_HALO + j - CONV_W // 2
                acc = acc + cw[j:j + 1, lanes] * xpad[s, pl.ds(r0 + off, LRU_ROWS), :]
            slabs.append(acc)
        xc = jnp.concatenate(slabs, axis=1)
        pre = _dot(xc.astype(BF16), w_ref[...]) + bias
        half_xc = 0.5 * xc
        for d in range(N_DIR):
            t_r = jnp.tanh(pre[:, (2 * d) * LRU_BLOCK:(2 * d + 1) * LRU_BLOCK])
            t_i = jnp.tanh(pre[:, (2 * d + 1) * LRU_BLOCK:(2 * d + 2) * LRU_BLOCK])
            rate_d = half_rate[d:d + 1, :]
            a = jnp.exp2(t_r * rate_d + rate_d)
            g2 = 1.0 - a * a
            gain = jnp.where(g2 > 0.0, g2 * lax.rsqrt(g2), 0.0)
            b = gain * (t_i * half_xc + half_xc)
            a_buf, b_buf = scan_bufs[d]
            for s in range(LRU_SLABS):
                lanes = slice(s * LANES, (s + 1) * LANES)
                a_buf[s, pl.ds(r0, LRU_ROWS), :] = a[:, lanes]
                b_buf[s, pl.ds(r0, LRU_ROWS), :] = b[:, lanes]
        return carry

    lax.fori_loop(0, n_chunks, dense, 0)
    tail = LRU_PAD_SEQ - SEQ
    for a_buf, b_buf in scan_bufs:
        a_buf[:, SEQ:LRU_PAD_SEQ, :] = jnp.ones((LRU_SLABS, tail, LANES), F32)
        b_buf[:, SEQ:LRU_PAD_SEQ, :] = jnp.zeros((LRU_SLABS, tail, LANES), F32)

    def seg(t):
        return pl.ds(t, SUBLANES, stride=LRU_SEG)

    def scan_step(k, carry):
        out = []
        for (a_buf, b_buf), t, chains in ((scan_bufs[0], k, carry[0]),
                                          (scan_bufs[1], LRU_SEG - 1 - k, carry[1])):
            new = []
            for s, (h, p) in enumerate(chains):
                a = a_buf[s, seg(t), :]
                h = a * h + b_buf[s, seg(t), :]
                p = a * p
                b_buf[s, seg(t), :] = h
                a_buf[s, seg(t), :] = p
                new.append((h, p))
            out.append(tuple(new))
        return tuple(out)

    z = jnp.zeros((SUBLANES, LANES), F32)
    o = jnp.ones((SUBLANES, LANES), F32)
    init = tuple(tuple((z, o) for _ in range(LRU_SLABS)) for _ in range(N_DIR))
    lax.fori_loop(0, LRU_SEG, scan_step, init, unroll=LRU_SCAN_UNROLL)

    row0 = jnp.zeros((1, LANES), F32)
    for s in range(LRU_SLABS):
        carry_f[s, 0:1, :] = row0
        c = row0
        for g in range(1, SUBLANES):
            last = g * LRU_SEG - 1
            c = b_f[s, last:last + 1, :] + a_f[s, last:last + 1, :] * c
            carry_f[s, g:g + 1, :] = c
        carry_b[s, SUBLANES - 1:SUBLANES, :] = row0
        c = row0
        for g in range(SUBLANES - 2, -1, -1):
            first = (g + 1) * LRU_SEG
            c = b_b[s, first:first + 1, :] + a_b[s, first:first + 1, :] * c
            carry_b[s, g:g + 1, :] = c

    seg_shift = LRU_SEG - LRU_ROWS
    row_in_chunk = lax.broadcasted_iota(jnp.int32, (LRU_ROWS, LANES), 0)

    def emit(c, carry):
        r0 = pl.multiple_of(c * LRU_ROWS, LRU_ROWS)
        rows = pl.ds(r0, LRU_ROWS)
        prev = pl.ds(jnp.maximum(c - 1, 0), 1)
        in_prev = row_in_chunk < c * seg_shift
        hs = []
        for s in range(LRU_SLABS):
            cf = jnp.where(in_prev, carry_f[s, prev, :], carry_f[s, pl.ds(c, 1), :])
            cb_ = jnp.where(in_prev, carry_b[s, prev, :], carry_b[s, pl.ds(c, 1), :])
            hs.append((b_f[s, rows, :] + a_f[s, rows, :] * cf)
                      + (b_b[s, rows, :] + a_b[s, rows, :] * cb_))
        h = jnp.concatenate(hs, axis=1)
        o_ref[rows, :] = (h * jax.nn.gelu(ly_ref[rows, :])).astype(BF16)
        return carry

    lax.fori_loop(0, n_chunks, emit, 0)


def _lru(lr, cw, cb, w, b, lam, l):
    n_blk = LRU_W // LRU_BLOCK
    scan_buf = pltpu.VMEM((LRU_SLABS, LRU_PAD_SEQ, LANES), F32)
    carry_buf = pltpu.VMEM((LRU_SLABS, SUBLANES, LANES), F32)
    return pl.pallas_call(
        _lru_body,
        grid=(BATCH, n_blk),
        in_specs=[
            pl.BlockSpec((SEQ, LRU_BLOCK), lambda b_, c: (b_, c)),
            pl.BlockSpec((SEQ, LRU_BLOCK), lambda b_, c: (b_, n_blk + c)),
            pl.BlockSpec((None, CONV_W, LRU_BLOCK), lambda b_, c: (l, 0, c)),
            pl.BlockSpec((None, 1, LRU_BLOCK), lambda b_, c: (l, 0, c)),
            pl.BlockSpec((None, None, LRU_BLOCK, 2 * N_DIR * LRU_BLOCK),
                         lambda b_, c: (l, c, 0, 0)),
            pl.BlockSpec((None, 1, 2 * N_DIR * LRU_BLOCK), lambda b_, c: (l, 0, c)),
            pl.BlockSpec((None, N_DIR, LRU_BLOCK), lambda b_, c: (l, 0, c)),
        ],
        out_specs=pl.BlockSpec((SEQ, LRU_BLOCK), lambda b_, c: (b_, c)),
        out_shape=jax.ShapeDtypeStruct((TOKENS, LRU_W), BF16),
        scratch_shapes=[
            pltpu.VMEM((LRU_SLABS, SEQ + 2 * LRU_HALO, LANES), F32),
            scan_buf, scan_buf, scan_buf, scan_buf,
            carry_buf, carry_buf,
        ],
        compiler_params=_params(2),
        name="rg_lru",
    )(lr, lr, cw, cb, w, b, lam)


def _xkv_body(mem_ref, g_ref, wkv_ref, kt_ref, v_ref):
    mn = _rms(mem_ref[0], g_ref[...]).astype(BF16)
    kv = _dot(mn, wkv_ref[...])
    kt_ref[0] = (kv[:, :D_MODEL] * (XATTN_HEAD_DIM ** -0.5)).T.astype(BF16)
    v_ref[0] = kv[:, D_MODEL:].astype(BF16)


def _xkv(mem, g, wkv, l):
    return pl.pallas_call(
        _xkv_body,
        grid=(BATCH,),
        in_specs=[pl.BlockSpec((1, MEM_LEN, D_MODEL), lambda b: (b, 0, 0)),
                  _layer(g, l), _resident(wkv.shape)],
        out_specs=[pl.BlockSpec((1, D_MODEL, MEM_LEN), lambda b: (b, 0, 0)),
                   pl.BlockSpec((1, MEM_LEN, D_MODEL), lambda b: (b, 0, 0))],
        out_shape=[jax.ShapeDtypeStruct((BATCH, D_MODEL, MEM_LEN), BF16),
                   jax.ShapeDtypeStruct((BATCH, MEM_LEN, D_MODEL), BF16)],
        compiler_params=_params(1),
        name="xattn_kv",
    )(mem, g, wkv)


def _merge_xattn_body(x_ref, a_ref, g_ref, l_ref, mixg_ref, wmix_ref, bgate_ref,
                      wa_ref, wg_ref, wl_ref, wo_ref,
                      xg_ref, wq_ref, kt_ref, v_ref, xwo_ref, o_ref):
    x = x_ref[...]
    hn = _rms(x, mixg_ref[...]).astype(BF16)
    merged = jnp.zeros((MERGE_TM, D_MODEL), F32)
    for k, (br, w) in enumerate(((a_ref, wa_ref), (g_ref, wg_ref), (l_ref, wl_ref))):
        lo = MIX_GATE_LO + k * D_MODEL
        gate = jax.nn.sigmoid(_dot(hn, wmix_ref[:, lo:lo + D_MODEL])
                              + bgate_ref[:, k * D_MODEL:(k + 1) * D_MODEL])
        merged = merged + gate * _dot(br[...], w[...])
    x = x + _dot(merged.astype(BF16), wo_ref[...])

    q = _dot(_rms(x, xg_ref[...]).astype(BF16), wq_ref[...]).astype(BF16)
    head_cols = [slice(h * XATTN_HEAD_DIM, (h + 1) * XATTN_HEAD_DIM)
                 for h in range(XATTN_HEADS)]
    scores = [_dot(q[:, cols], kt_ref[0, cols, :]) for cols in head_cols]
    probs, denoms = [], []
    for s in scores:
        p = jnp.exp(s - jnp.max(s, axis=-1, keepdims=True))
        denoms.append(jnp.sum(p, axis=-1, keepdims=True))
        probs.append(p.astype(BF16))
    heads = [(_dot(p, v_ref[0, :, cols]) * (1.0 / l)).astype(BF16)
             for p, l, cols in zip(probs, denoms, head_cols)]
    o_ref[...] = x + _dot(jnp.concatenate(heads, axis=1), xwo_ref[...])


def _merge_xattn(x, attn_o, gmlp_o, lru_o, mixg, wmix, bgate, wa, wg, wl, wo,
                 xg, wq, kt, v, xwo, l, casts):
    tiles = SEQ // MERGE_TM
    row = lambda w: pl.BlockSpec((MERGE_TM, w), lambda i: (i, 0))
    (y,), cast = _call_with_casts(
        _merge_xattn_body,
        steps=TOKENS // MERGE_TM,
        in_specs=[row(D_MODEL), row(ATTN_Q_W), row(GMLP_W), row(LRU_W),
                  _layer(mixg, l), _resident(wmix.shape), _layer(bgate, l),
                  _resident(wa.shape), _resident(wg.shape), _resident(wl.shape),
                  _resident(wo.shape),
                  _layer(xg, l), _resident(wq.shape),
                  pl.BlockSpec((1, D_MODEL, MEM_LEN), lambda i: (i // tiles, 0, 0)),
                  pl.BlockSpec((1, MEM_LEN, D_MODEL), lambda i: (i // tiles, 0, 0)),
                  _resident(xwo.shape)],
        out_specs=[row(D_MODEL)],
        out_shape=[jax.ShapeDtypeStruct((TOKENS, D_MODEL), F32)],
        args=(x, attn_o, gmlp_o, lru_o, mixg, wmix, bgate, wa, wg, wl, wo,
              xg, wq, kt, v, xwo),
        casts=casts,
        name="merge_xattn",
    )
    return y, cast


def _rope_tables():
    rows = SEQ // GRID_W
    row = jnp.repeat(jnp.arange(rows), GRID_W).astype(F32)
    col = jnp.tile(jnp.arange(GRID_W), rows).astype(F32)
    n_freq = HEAD_DIM // 4
    inv_freq = ROPE_THETA ** (-jnp.arange(n_freq, dtype=F32) / n_freq)
    ang_r = row[:, None] * inv_freq[None, :]
    ang_c = col[:, None] * inv_freq[None, :]
    cos = jnp.concatenate([jnp.cos(ang_r)] * 2 + [jnp.cos(ang_c)] * 2, axis=1)
    sin = jnp.concatenate([-jnp.sin(ang_r), jnp.sin(ang_r),
                           -jnp.sin(ang_c), jnp.sin(ang_c)], axis=1)
    rep = LANES // HEAD_DIM
    return jnp.tile(cos, (1, rep)), jnp.tile(sin, (1, rep))


def _block_diag(w):
    n, a, b = w.shape[-3:]
    eye = jnp.eye(n, dtype=w.dtype)
    out = eye[:, None, :, None] * w[..., :, :, None, :]
    return out.reshape(w.shape[:-3] + (n * a, n * b))


def kernel(x, mem, ffn1_norm, ffn1_w_in, ffn1_w_out, mix_norm, w_mix_in, b_gate,
           q_norm, k_norm, attn_up, gmlp_v_norm, gmlp_ws, gmlp_bs, gmlp_up,
           lru_conv_w, lru_conv_b, lru_wa, lru_ba, lru_wi, lru_bi, lru_lambda, lru_up,
           w_mix_out, xattn_norm, mem_norm, xattn_wq, xattn_wkv, xattn_wo,
           ffn2_norm, ffn2_w_in, ffn2_w_out, final_norm):
    assert x.shape == (BATCH, SEQ, D_MODEL) and mem.shape == (BATCH, MEM_LEN, D_MODEL)
    vec = lambda a: a.reshape(DEPTH, 1, -1)
    cos, sin = _rope_tables()
    fin = final_norm.reshape(1, D_MODEL)
    gsum = _block_diag(jnp.full((QK_W // HEAD_DIM, HEAD_DIM, HEAD_DIM),
                                1.0 / HEAD_DIM, BF16))
    qkg = vec(jnp.concatenate([jnp.tile(q_norm, (1, ATTN_HEADS)),
                               jnp.tile(k_norm, (1, ATTN_KV_HEADS))], axis=1))
    gm_bias = jnp.repeat(jnp.swapaxes(gmlp_bs, 1, 2), GMLP_GROUP_W, axis=2)
    gm_ws_f32 = gmlp_ws.reshape(DEPTH, GMLP_W, GMLP_CHUNK)

    n_blk = LRU_W // LRU_BLOCK
    heads_per_blk = LRU_BLOCK // LRU_HEAD_W

    def blk(w):
        return _block_diag(
            w.reshape(DEPTH, n_blk, heads_per_blk, LRU_HEAD_W, LRU_HEAD_W))
    lru_w = (0.5 * jnp.concatenate(
        [blk(lru_wa[:, 0]), blk(lru_wi[:, 0]), blk(lru_wa[:, 1]), blk(lru_wi[:, 1])],
        axis=-1)).astype(BF16)
    lru_b = 0.5 * jnp.stack(
        [b.reshape(DEPTH, n_blk, LRU_BLOCK)
         for b in (lru_ba[:, 0], lru_bi[:, 0], lru_ba[:, 1], lru_bi[:, 1])],
        axis=2).reshape(DEPTH, 1, -1)

    mixer_f32 = (w_mix_in, xattn_wkv, attn_up, gmlp_up, lru_up, w_mix_out,
                 xattn_wq, xattn_wo, gm_ws_f32)
    ffn1_w = [_to_bf16(ffn1_w_in, 0), _to_bf16(ffn1_w_out, 0)]
    mixer_w = None

    h = x.reshape(TOKENS, D_MODEL)
    for l in range(DEPTH):
        nxt = l + 1 if l + 1 < DEPTH else None
        casts = [(ffn2_w_in, l), (ffn2_w_out, l)]
        if mixer_w is None:
            casts += [(w, l) for w in mixer_f32]
        h, cast = _ffn(h, vec(ffn1_norm), *ffn1_w, fin, l, casts, final=False)
        ffn2_w = cast[:2]
        if mixer_w is None:
            mixer_w = cast[2:]
        w_mix, w_xkv, w_attn_up, w_gmlp_up, w_lru_up, w_out, w_xq, w_xo, gm_ws = mixer_w

        (qt, k, vt, gmlp_o, lr), _ = _mix_in(
            h, vec(mix_norm), w_mix, qkg, gsum, cos, sin,
            vec(gmlp_v_norm), gm_ws.reshape(gmlp_ws.shape[1:]), gm_bias, l, [])
        attn_o = _attention(qkg, qt, k, vt, l)
        lru_o = _lru(lr, lru_conv_w, vec(lru_conv_b), lru_w, lru_b, lru_lambda, l)
        xkt, xv = _xkv(mem, vec(mem_norm), w_xkv, l)
        h, ffn1_w = _merge_xattn(
            h, attn_o, gmlp_o, lru_o, vec(mix_norm), w_mix, vec(b_gate),
            w_attn_up, w_gmlp_up, w_lru_up, w_out, vec(xattn_norm), w_xq, xkt, xv, w_xo,
            l, [] if nxt is None else [(ffn1_w_in, nxt), (ffn1_w_out, nxt)])
        h, mixer_w = _ffn(h, vec(ffn2_norm), *ffn2_w, fin, l,
                          [] if nxt is None else [(w, nxt) for w in mixer_f32],
                          final=nxt is None)
    return h.reshape(BATCH, SEQ, D_MODEL)
```
